```python
import math
import jax, jax.numpy as jnp
from jax import lax
import numpy as np

D_MODEL = 4096
BATCH = 32
SEQ = 256
DEPTH = 2
DEC_BATCH = 2
DEC_SEQ = 2048
PAST_LEN = 256

GRID_W = 64
EPS = 1e-6
N_MIXERS = 2
D_FF = 4 * D_MODEL
HG_EXPAND = 128
HG_HEADS = D_MODEL // HG_EXPAND
HG_DK = HG_EXPAND
HG_DV = D_MODEL // HG_HEADS
HG_CHUNK = 16
N_HG = (DEPTH + 1) // 2
DA_HEAD_DIM = 128
DA_HEADS = D_MODEL // (2 * DA_HEAD_DIM)
N_DA = DEPTH // 2
Q_BLOCK = 128
ROPE_BASE = 10000.0

kernel_name = "hybrid_hgrn2_diffattn_diffusion_step"


def rmsnorm(x, g):
    xf = x.astype(jnp.float32)
    y = xf * lax.rsqrt(jnp.mean(xf * xf, axis=-1, keepdims=True) + EPS)
    return (y * g.astype(jnp.float32)).astype(x.dtype)


def modulation(cond, w_mod, b_mod):
    m = jax.nn.silu(cond) @ w_mod + b_mod
    return jnp.split(m, 6, axis=-1)


def squared_relu_mlp(h, w1, w2):
    return jnp.square(jax.nn.relu(h @ w1)) @ w2


def axial_rope(x):
    T, hd = x.shape[1], x.shape[-1]
    rows = T // GRID_W
    r_pos = jnp.repeat(jnp.arange(rows), GRID_W).astype(jnp.float32)
    c_pos = jnp.tile(jnp.arange(GRID_W), rows).astype(jnp.float32)
    half = hd // 2
    inv_freq = ROPE_BASE ** (-jnp.arange(0, half, 2, dtype=jnp.float32) / half)

    def rot(xa, pos):
        ang = pos[:, None] * inv_freq
        cos = jnp.cos(ang)[None, :, None, :]
        sin = jnp.sin(ang)[None, :, None, :]
        xa = xa.astype(jnp.float32)
        x1, x2 = xa[..., : half // 2], xa[..., half // 2:]
        return jnp.concatenate([x1 * cos - x2 * sin, x1 * sin + x2 * cos], axis=-1)

    out = jnp.concatenate([rot(x[..., :half], r_pos), rot(x[..., half:], c_pos)], axis=-1)
    return out.astype(x.dtype)


def hgrn2_chunk_scan(q, k, v, logf, s0):
    B, T, H, _ = q.shape
    n = T // HG_CHUNK

    def chunks(a):
        return jnp.moveaxis(a.reshape(B, n, HG_CHUNK, H, a.shape[-1]), 1, 0)

    incl = jnp.tril(jnp.ones((HG_CHUNK, HG_CHUNK), dtype=bool))[None, :, :, None, None]

    def step(s, inp):
        qc, kc, vc, gc = inp
        b = jnp.cumsum(gc, axis=1)
        decay = jnp.exp(jnp.where(incl, b[:, :, None] - b[:, None, :], -jnp.inf))
        a = jnp.einsum('bthk,bshk,btshk->bhts', qc, kc, decay)
        o = jnp.einsum('bhts,bshv->bthv', a, vc) + jnp.einsum('bthk,bhkv->bthv', qc * jnp.exp(b), s)
        b_last = b[:, -1]
        k_dec = kc * jnp.exp(b_last[:, None] - b)
        s_new = jnp.exp(b_last)[..., None] * s + jnp.einsum('bshk,bshv->bhkv', k_dec, vc)
        return s_new, o

    s_fin, o = lax.scan(step, s0.astype(jnp.float32), (chunks(q), chunks(k), chunks(v), chunks(logf)))
    return jnp.moveaxis(o, 0, 1).reshape(B, T, H, v.shape[-1]), s_fin


def hgrn2_mixer(h, w_in, w_out, g_norm, lb_logits, layer, s0_fwd, s0_bwd):
    B, T, _ = h.shape
    f32 = jnp.float32
    q, i_in, f_fw, f_bw, g = jnp.split(h @ w_in, 5, axis=-1)

    def heads(a, d):
        return a.astype(f32).reshape(B, T, HG_HEADS, d)

    q = heads(jax.nn.silu(q), HG_DK)
    v = heads(i_in, HG_DV)
    lb = jnp.cumsum(jax.nn.softmax(lb_logits.astype(f32), axis=1), axis=1)[:, layer]

    def gates(f, lb_dir):
        fg = lb_dir + (1.0 - lb_dir) * jax.nn.sigmoid(f.astype(f32))
        return heads(1.0 - fg, HG_DK), heads(jnp.log(fg), HG_DK)

    k_fw, g_fw = gates(f_fw, lb[0])
    k_bw, g_bw = gates(f_bw, lb[1])
    o_fw, s_fw = hgrn2_chunk_scan(q, k_fw, v, g_fw, s0_fwd)

    def rev(a):
        return jnp.flip(a, axis=1)

    o_bw, s_bw = hgrn2_chunk_scan(rev(q), rev(k_bw), rev(v), rev(g_bw), s0_bwd)
    o = o_fw + rev(o_bw)
    o = rmsnorm(o, g_norm) * jax.nn.silu(heads(g, HG_DV))
    out = o.reshape(B, T, D_MODEL).astype(h.dtype) @ w_out
    return out, s_fw.astype(h.dtype), s_bw.astype(h.dtype)


def da_project(h, w_in):
    B, T, _ = h.shape
    q, k, v = jnp.split(h @ w_in, 3, axis=-1)
    q = q.reshape(B, T, 2 * DA_HEADS, DA_HEAD_DIM)
    k = k.reshape(B, T, 2 * DA_HEADS, DA_HEAD_DIM)
    v = v.reshape(B, T, DA_HEADS, 2 * DA_HEAD_DIM)
    return q, k, v


def diff_attention(q, k, v, lam):
    B, Tq = q.shape[:2]
    Tk = k.shape[1]
    nb = Tq // Q_BLOCK
    qb = jnp.moveaxis(q.reshape(B, nb, Q_BLOCK, 2 * DA_HEADS, DA_HEAD_DIM), 1, 0)
    scale = DA_HEAD_DIM ** -0.5

    def block(qi):
        s = jnp.einsum('bqnd,bknd->bnqk', qi, k).astype(jnp.float32) * scale
        p = jax.nn.softmax(s, axis=-1).reshape(B, DA_HEADS, 2, Q_BLOCK, Tk)
        a = p[:, :, 0] - lam * p[:, :, 1]
        return jnp.einsum('bhqk,bkhe->bqhe', a.astype(v.dtype), v)

    o = lax.map(block, qb)
    return jnp.moveaxis(o, 0, 1).reshape(B, Tq, DA_HEADS, 2 * DA_HEAD_DIM)


def da_output(o, subln_g, w_out, lam_init):
    B, T = o.shape[:2]
    o = rmsnorm(o, subln_g) * (1.0 - lam_init)
    return o.reshape(B, T, D_MODEL) @ w_out


def trunk(x, cond, latent, ctx_sf, ctx_sb, ctx_k, ctx_v, p):
    B = x.shape[0]
    new_sf, new_sb, new_k, new_v = [], [], [], []
    for layer in range(DEPTH):
        sh1, sc1, gt1, sh2, sc2, gt2 = modulation(cond, p['w_mod'][layer], p['b_mod'][layer])
        h = rmsnorm(x, p['norm_g'][layer, 0]) * (1 + sc1) + sh1
        j = layer // N_MIXERS
        if layer % N_MIXERS == 0:
            if latent:
                s0f, s0b = ctx_sf[:, j], ctx_sb[:, j]
            else:
                s0f = jnp.zeros((B, HG_HEADS, HG_DK, HG_DV), jnp.float32)
                s0b = s0f
            out, sf, sb = hgrn2_mixer(h, p['hg_w_in'][j], p['hg_w_out'][j], p['hg_g_norm'][j],
                                      p['hg_lb_logits'], layer, s0f, s0b)
            new_sf.append(sf)
            new_sb.append(sb)
        else:
            lam_init = 0.8 - 0.6 * math.exp(-0.3 * layer)
            lmb = p['da_lambda'][j].astype(jnp.float32)
            lam = jnp.exp(jnp.sum(lmb[0] * lmb[1])) - jnp.exp(jnp.sum(lmb[2] * lmb[3])) + lam_init
            q, k, v = da_project(h, p['da_w_in'][j])
            if latent:
                q, k = axial_rope(q), axial_rope(k)
                k_all = jnp.concatenate([ctx_k[:, j].astype(k.dtype), k], axis=1)
                v_all = jnp.concatenate([ctx_v[:, j].astype(v.dtype), v], axis=1)
            else:
                new_k.append(k)
                new_v.append(v)
                k_all, v_all = k, v
            o = diff_attention(q, k_all, v_all, lam)
            out = da_output(o, p['da_subln_g'][j], p['da_w_out'][j], lam_init)
        x = x + gt1 * out
        h = rmsnorm(x, p['norm_g'][layer, 1]) * (1 + sc2) + sh2
        x = x + gt2 * squared_relu_mlp(h, p['w_ff1'][layer], p['w_ff2'][layer])
    return rmsnorm(x, p['final_norm_g']), new_sf, new_sb, new_k, new_v


def setup_inputs(seed: int = 0) -> dict:
    key = jax.random.key(seed)
    ks = jax.random.split(key, 24)
    nrm = jax.random.normal
    f32 = jnp.float32
    D = D_MODEL
    return {
        "x_prompt": nrm(ks[0], (BATCH, SEQ, D), f32),
        "x_sample": nrm(ks[1], (DEC_BATCH, DEC_SEQ, D), f32),
        "state_hgrn_fwd": nrm(ks[2], (DEC_BATCH, N_HG, HG_HEADS, HG_DK, HG_DV), f32) * 0.5,
        "state_hgrn_bwd": nrm(ks[3], (DEC_BATCH, N_HG, HG_HEADS, HG_DK, HG_DV), f32) * 0.5,
        "cache_attn_k": nrm(ks[4], (DEC_BATCH, N_DA, PAST_LEN, 2 * DA_HEADS, DA_HEAD_DIM), f32),
        "cache_attn_v": nrm(ks[5], (DEC_BATCH, N_DA, PAST_LEN, DA_HEADS, 2 * DA_HEAD_DIM), f32),
        "c": nrm(ks[6], (DEC_BATCH, D), f32),
        "c_ctx": nrm(ks[7], (D,), f32),
        "w_mod": nrm(ks[8], (DEPTH, D, 6 * D), f32) * D ** -0.5,
        "b_mod": nrm(ks[9], (DEPTH, 6 * D), f32) * 0.01,
        "norm_g": 1.0 + 0.01 * nrm(ks[10], (DEPTH, 2, D), f32),
        "w_ff1": nrm(ks[11], (DEPTH, D, D_FF), f32) * D ** -0.5,
        "w_ff2": nrm(ks[12], (DEPTH, D_FF, D), f32) * D_FF ** -0.5,
        "hg_w_in": nrm(ks[13], (N_HG, D, 5 * D), f32) * D ** -0.5,
        "hg_w_out": nrm(ks[14], (N_HG, D, D), f32) * D ** -0.5,
        "hg_g_norm": 1.0 + 0.01 * nrm(ks[15], (N_HG, HG_DV), f32),
        "hg_lb_logits": nrm(ks[16], (2, DEPTH + 1, HG_HEADS * HG_DK), f32) * 0.1,
        "da_w_in": nrm(ks[17], (N_DA, D, 3 * D), f32) * D ** -0.5,
        "da_w_out": nrm(ks[18], (N_DA, D, D), f32) * D ** -0.5,
        "da_subln_g": 1.0 + 0.01 * nrm(ks[19], (N_DA, 2 * DA_HEAD_DIM), f32),
        "da_lambda": nrm(ks[20], (N_DA, 4, DA_HEAD_DIM), f32) * 0.1,
        "final_norm_g": 1.0 + 0.01 * nrm(ks[21], (D,), f32),
    }


def reference(x_prompt, x_sample, state_hgrn_fwd, state_hgrn_bwd, cache_attn_k, cache_attn_v, c, c_ctx,
              w_mod, b_mod, norm_g, w_ff1, w_ff2, hg_w_in, hg_w_out, hg_g_norm, hg_lb_logits,
              da_w_in, da_w_out, da_subln_g, da_lambda, final_norm_g):
    p = dict(w_mod=w_mod, b_mod=b_mod, norm_g=norm_g, w_ff1=w_ff1, w_ff2=w_ff2,
             hg_w_in=hg_w_in, hg_w_out=hg_w_out, hg_g_norm=hg_g_norm, hg_lb_logits=hg_lb_logits,
             da_w_in=da_w_in, da_w_out=da_w_out, da_subln_g=da_subln_g, da_lambda=da_lambda,
             final_norm_g=final_norm_g)
    y_prompt, sf, sb, ks_, vs_ = trunk(x_prompt, c_ctx[None, None, :], False, None, None, None, None, p)
    new_sf = jnp.stack(sf, axis=1)
    new_sb = jnp.stack(sb, axis=1)
    new_k = jnp.stack(ks_, axis=1)
    new_v = jnp.stack(vs_, axis=1)
    y_sample, _, _, _, _ = trunk(x_sample, c[:, None, :], True, state_hgrn_fwd, state_hgrn_bwd,
                                 cache_attn_k, cache_attn_v, p)
    return (y_prompt, y_sample, new_sf, new_sb, new_k, new_v)
```

```python
import functools
import math

import jax
import jax.numpy as jnp
from jax import lax
from jax.experimental import pallas as pl
from jax.experimental.pallas import tpu as pltpu

F32 = jnp.float32
BF16 = jnp.bfloat16

GRID_W = 64
EPS = 1e-6
ROPE_BASE = 10000.0
N_MIXERS = 2

V7X_SUBLANES = 8
V7X_LANES = 128
V7X_VMEM_LIMIT_BYTES = 56 * 1024 * 1024

MM_BM = 1024
MM_BN = 1024
MM_VMEM_BUDGET_BYTES = 44 * 1024 * 1024
ROW_TILE = 256
MOD_BN = 512
HG_CHUNK = 128
HG_HEADS_PER_STEP = 4
ATTN_TQ = 256


def _tile(full, want):
    t = min(full, want)
    assert full % t == 0, (full, want)
    return t


def _params(sem):
    return pltpu.CompilerParams(dimension_semantics=sem, vmem_limit_bytes=V7X_VMEM_LIMIT_BYTES)


def _silu(x):
    return x * jax.nn.sigmoid(x)


def _dot(a, b):
    return jnp.dot(a, b, preferred_element_type=F32)


def _dot_nt(a, b):
    return lax.dot_general(a, b, (((1,), (1,)), ((), ())), preferred_element_type=F32)


def _dot_tn(a, b):
    return lax.dot_general(a, b, (((0,), (0,)), ((), ())), preferred_element_type=F32)


class _Groups:
    def __init__(self, ctx_rows, dec_batch, dec_seq):
        self.ctx_rows = ctx_rows
        self.dec_batch = dec_batch
        self.dec_seq = dec_seq
        self.rows = ctx_rows + dec_batch * dec_seq

    def of_tile(self, i, tile):
        assert self.ctx_rows % tile == 0 and self.dec_seq % tile == 0
        g = jnp.int32(0)
        for b in range(self.dec_batch):
            g = g + (i >= (self.ctx_rows + b * self.dec_seq) // tile).astype(jnp.int32)
        return g


def _mod_kernel(c_ref, w_ref, b_ref, o_ref):
    a = _silu(c_ref[...]).astype(BF16)
    o_ref[...] = _dot(a, w_ref[...].astype(BF16)) + b_ref[...]


def _modulation(cond, w_mod, b_mod):
    g8, d = cond.shape
    depth, _, n = w_mod.shape
    bn = _tile(n, MOD_BN)
    return pl.pallas_call(
        _mod_kernel,
        grid=(depth, n // bn),
        in_specs=[
            pl.BlockSpec((g8, d), lambda l, j: (0, 0)),
            pl.BlockSpec((None, d, bn), lambda l, j: (l, 0, j)),
            pl.BlockSpec((None, 1, bn), lambda l, j: (l, 0, j)),
        ],
        out_specs=pl.BlockSpec((None, g8, bn), lambda l, j: (l, 0, j)),
        out_shape=jax.ShapeDtypeStruct((depth, g8, n), F32),
        compiler_params=_params(("arbitrary", "arbitrary")),
        name="adaln_modulation",
    )(cond, w_mod, b_mod.reshape(depth, 1, n))


def _norm_kernel(*refs, modulated):
    if modulated:
        x_ref, g_ref, sc_ref, sh_ref, o_ref = refs
    else:
        x_ref, g_ref, o_ref = refs
    x = x_ref[...]
    y = x * lax.rsqrt(jnp.mean(x * x, axis=-1, keepdims=True) + EPS) * g_ref[...]
    if modulated:
        y = y * (1.0 + sc_ref[...]) + sh_ref[...]
    o_ref[...] = y.astype(o_ref.dtype)


def _norm_mod(x, g, mod, layer, sc_chunk, sh_chunk, groups):
    rows, d = x.shape
    tm = _tile(math.gcd(groups.ctx_rows, groups.dec_seq), ROW_TILE)

    def vec(chunk):
        return pl.BlockSpec((None, None, 1, d), lambda i: (layer, groups.of_tile(i, tm), 0, chunk))

    return pl.pallas_call(
        functools.partial(_norm_kernel, modulated=True),
        grid=(rows // tm,),
        in_specs=[
            pl.BlockSpec((tm, d), lambda i: (i, 0)),
            pl.BlockSpec((1, d), lambda i: (0, 0)),
            vec(sc_chunk),
            vec(sh_chunk),
        ],
        out_specs=pl.BlockSpec((tm, d), lambda i: (i, 0)),
        out_shape=jax.ShapeDtypeStruct((rows, d), BF16),
        compiler_params=_params(("parallel",)),
        name="rmsnorm_modulate",
    )(x, g.reshape(1, d), mod, mod)


def _final_norm(x, g, row_off, rows):
    d = x.shape[1]
    tm = _tile(math.gcd(rows, row_off) if row_off else rows, ROW_TILE)
    off = row_off // tm
    return pl.pallas_call(
        functools.partial(_norm_kernel, modulated=False),
        grid=(rows // tm,),
        in_specs=[
            pl.BlockSpec((tm, d), lambda i: (i + off, 0)),
            pl.BlockSpec((1, d), lambda i: (0, 0)),
        ],
        out_specs=pl.BlockSpec((tm, d), lambda i: (i, 0)),
        out_shape=jax.ShapeDtypeStruct((rows, d), F32),
        compiler_params=_params(("parallel",)),
        name="final_rmsnorm",
    )(x, g.reshape(1, d))


def _mm_kernel(*refs, nk, epilogue):
    if epilogue == "residual":
        x_ref, w_ref, res_ref, gate_ref, o_ref = refs
    else:
        x_ref, w_ref, o_ref = refs
    part = _dot(x_ref[...], w_ref[...])

    def finish(acc):
        if epilogue == "relu2":
            r = jnp.maximum(acc, 0.0)
            acc = r * r
        elif epilogue == "residual":
            acc = res_ref[...] + gate_ref[...] * acc
        return acc.astype(o_ref.dtype)

    if nk == 1:
        o_ref[...] = finish(part)
        return

    k = pl.program_id(2)

    @pl.when(k == 0)
    def _():
        o_ref[...] = part

    @pl.when(jnp.logical_and(k > 0, k < nk - 1))
    def _():
        o_ref[...] += part

    @pl.when(k == nk - 1)
    def _():
        o_ref[...] = finish(o_ref[...] + part)


def _matmul(x, w, out_dtype, epilogue="plain", res=None, mod=None, layer=None, gate_chunk=None, groups=None):
    m, kdim = x.shape
    n = w.shape[1]
    bm, bn = _tile(m, MM_BM), _tile(n, MM_BN)
    out_bytes = jnp.dtype(out_dtype).itemsize + (4 if epilogue == "residual" else 0)
    bk = kdim
    while 2 * (2 * bk * (bm + bn) + bm * bn * out_bytes) + 4 * bm * bn > MM_VMEM_BUDGET_BYTES:
        assert bk % 2 == 0
        bk //= 2
    nk = kdim // bk
    assert nk == 1 or out_dtype == F32
    in_specs = [
        pl.BlockSpec((bm, bk), lambda i, j, k: (i, k)),
        pl.BlockSpec((bk, bn), lambda i, j, k: (k, j)),
    ]
    args = [x, w]
    if epilogue == "residual":
        nb = n // bn
        in_specs += [
            pl.BlockSpec((bm, bn), lambda i, j, k: (i, j)),
            pl.BlockSpec((None, None, 1, bn),
                         lambda i, j, k: (layer, groups.of_tile(i, bm), 0, gate_chunk * nb + j)),
        ]
        args += [res, mod]
    return pl.pallas_call(
        functools.partial(_mm_kernel, nk=nk, epilogue=epilogue),
        grid=(m // bm, n // bn, nk),
        in_specs=in_specs,
        out_specs=pl.BlockSpec((bm, bn), lambda i, j, k: (i, j)),
        out_shape=jax.ShapeDtypeStruct((m, n), out_dtype),
        compiler_params=_params(("parallel", "parallel", "arbitrary")),
        name="matmul_" + epilogue,
    )(*args)


def _level_ref(c, half, reverse):
    rows, width = c.shape
    blk = 2 * half
    idx = half if reverse else half - 1
    if blk >= V7X_SUBLANES:
        c3 = c.reshape(rows // blk, blk, width)
        ref = jnp.broadcast_to(c3[:, idx:idx + 1, :], c3.shape)
        return ref.reshape(rows, width)
    pos = lax.broadcasted_iota(jnp.int32, c.shape, 0) & (blk - 1)
    out = c
    for delta in range(-idx, blk - idx):
        if delta != 0:
            out = jnp.where(pos == idx + delta, pltpu.roll(c, delta % rows, 0), out)
    return out


def _hg_scan_kernel(*refs, chunk, heads, dk, dv, reverse, layer, has_s0):
    if has_s0:
        q_ref, v_ref, f_ref, lbl_ref, s0_ref, o_ref, sfin_ref, st_ref = refs
    else:
        q_ref, v_ref, f_ref, lbl_ref, o_ref, sfin_ref, st_ref = refs
    c_idx = pl.program_id(2)
    n_levels = chunk.bit_length() - 1

    @pl.when(c_idx == 0)
    def _():
        for j in range(heads):
            if has_s0:
                st_ref[j] = s0_ref[0, j].T
            else:
                st_ref[j] = jnp.zeros((dv, dk), F32)

    lbl = lbl_ref[...]
    e = jnp.exp(lbl - jnp.max(lbl, axis=0, keepdims=True))
    lb = jnp.sum(e[:layer + 1], axis=0, keepdims=True) / jnp.sum(e, axis=0, keepdims=True)

    fg = lb + (1.0 - lb) * jax.nn.sigmoid(f_ref[...])
    k_all = 1.0 - fg
    logf = jnp.log(fg)

    row = lax.broadcasted_iota(jnp.int32, (chunk, chunk), 0)
    col = lax.broadcasted_iota(jnp.int32, (chunk, chunk), 1)
    before = (col > row) if reverse else (col < row)
    tri = jnp.logical_or(before, col == row).astype(BF16)
    hi = logf.astype(BF16)
    r1 = logf - hi.astype(F32)
    mid = r1.astype(BF16)
    lo = (r1 - mid.astype(F32)).astype(BF16)
    c_all = _dot(tri, hi) + _dot(tri, mid) + _dot(tri, lo)

    x = row ^ col
    lvl = jnp.zeros((chunk, chunk), jnp.int32)
    for b in range(n_levels):
        lvl = lvl + (x >= (1 << b)).astype(jnp.int32)
    lvl = jnp.where(jnp.logical_or(before, x == 0), lvl, -1)

    end = 0 if reverse else chunk - 1
    for j in range(heads):
        sl = slice(j * dk, (j + 1) * dk)
        q = _silu(q_ref[:, sl])
        k = k_all[:, sl]
        c = c_all[:, sl]
        v = v_ref[:, j * dv:(j + 1) * dv].astype(BF16)
        c_end = c[end:end + 1, :]

        a = jnp.where(lvl == 0, _dot_nt(q.astype(BF16), k.astype(BF16)), 0.0)
        for level in range(1, n_levels + 1):
            w = jnp.exp(-jnp.abs(c - _level_ref(c, 1 << (level - 1), reverse)))
            a_l = _dot_nt((q * w).astype(BF16), (k * w).astype(BF16))
            a = jnp.where(lvl == level, a_l, a)

        st = st_ref[j]
        o = _dot(a.astype(BF16), v) + _dot_nt((q * jnp.exp(c)).astype(BF16), st.astype(BF16))
        o_ref[:, j * dv:(j + 1) * dv] = o
        k_dec = (k * jnp.exp(c_end - c)).astype(BF16)
        st_new = st * jnp.exp(c_end) + _dot_tn(v, k_dec)
        st_ref[j] = st_new

        @pl.when(c_idx == pl.num_programs(2) - 1)
        def _():
            sfin_ref[0, j] = st_new.T


def _hg_scan(proj, lb_logits_dir, s0, *, row_off, batch, seq, n_heads, dk, dv, d, f_seg, reverse, layer):
    chunk = _tile(seq, HG_CHUNK)
    heads = _tile(n_heads, HG_HEADS_PER_STEP)
    assert dk == dv and d == n_heads * dk
    w = heads * dk
    nchunks = seq // chunk
    segs = d // w
    off = row_off // chunk

    def rows_of(b, c):
        return off + b * nchunks + (nchunks - 1 - c if reverse else c)

    def seg_spec(seg):
        return pl.BlockSpec((chunk, w), lambda b, h, c: (rows_of(b, c), seg * segs + h))

    n_lb = lb_logits_dir.shape[0]
    in_specs = [seg_spec(0), seg_spec(1), seg_spec(f_seg),
                pl.BlockSpec((n_lb, w), lambda b, h, c: (0, h))]
    args = [proj, proj, proj, lb_logits_dir]
    if s0 is not None:
        in_specs.append(pl.BlockSpec((1, heads, dk, dv), lambda b, h, c: (b, h, 0, 0)))
        args.append(s0)
    o, s_fin = pl.pallas_call(
        functools.partial(_hg_scan_kernel, chunk=chunk, heads=heads, dk=dk, dv=dv,
                          reverse=reverse, layer=layer, has_s0=s0 is not None),
        grid=(batch, n_heads // heads, nchunks),
        in_specs=in_specs,
        out_specs=[
            pl.BlockSpec((chunk, w), lambda b, h, c: (rows_of(b, c) - off, h)),
            pl.BlockSpec((1, heads, dk, dv), lambda b, h, c: (b, h, 0, 0)),
        ],
        out_shape=[
            jax.ShapeDtypeStruct((batch * seq, d), F32),
            jax.ShapeDtypeStruct((batch, n_heads, dk, dv), F32),
        ],
        scratch_shapes=[pltpu.VMEM((heads, dv, dk), F32)],
        compiler_params=_params(("parallel", "parallel", "arbitrary")),
        name="hgrn2_scan_" + ("bwd" if reverse else "fwd"),
    )(*args)
    return o, s_fin


def _hg_out_kernel(of_ref, ob_ref, g_ref, gn_ref, o_ref, *, n_heads, dv):
    gn = gn_ref[...]
    for h in range(n_heads):
        sl = slice(h * dv, (h + 1) * dv)
        o = of_ref[:, sl] + ob_ref[:, sl]
        y = o * lax.rsqrt(jnp.mean(o * o, axis=-1, keepdims=True) + EPS) * gn
        o_ref[:, sl] = (y * _silu(g_ref[:, sl])).astype(o_ref.dtype)


def _hg_out(o_fwd, o_bwd, proj, g_norm, *, n_heads, dv, g_seg):
    rows, d = o_fwd.shape
    tm = _tile(rows, ROW_TILE)
    return pl.pallas_call(
        functools.partial(_hg_out_kernel, n_heads=n_heads, dv=dv),
        grid=(rows // tm,),
        in_specs=[
            pl.BlockSpec((tm, d), lambda i: (i, 0)),
            pl.BlockSpec((tm, d), lambda i: (i, 0)),
            pl.BlockSpec((tm, d), lambda i: (i, g_seg)),
            pl.BlockSpec((1, dv), lambda i: (0, 0)),
        ],
        out_specs=pl.BlockSpec((tm, d), lambda i: (i, 0)),
        out_shape=jax.ShapeDtypeStruct((rows, d), BF16),
        compiler_params=_params(("parallel",)),
        name="hgrn2_out_norm_gate",
    )(o_fwd, o_bwd, proj, g_norm.reshape(1, dv))


def _rope_kernel(x_ref, cos_ref, sin_ref, o_ref, *, n_groups, hd, scale):
    cos = cos_ref[...]
    sin = sin_ref[...]
    quarter = hd // 4
    lane = lax.broadcasted_iota(jnp.int32, cos.shape, 1)
    first = (lane % (2 * quarter)) < quarter
    for n in range(n_groups):
        sl = slice(n * hd, (n + 1) * hd)
        x = x_ref[:, sl]
        partner = jnp.where(first, pltpu.roll(x, hd - quarter, 1), pltpu.roll(x, quarter, 1))
        o_ref[:, sl] = ((x * cos + partner * sin) * scale).astype(o_ref.dtype)


def _rope_tables(seq, hd):
    half = hd // 2
    rows = seq // GRID_W
    r_pos = jnp.repeat(jnp.arange(rows), GRID_W).astype(F32)
    c_pos = jnp.tile(jnp.arange(GRID_W), rows).astype(F32)
    inv_freq = ROPE_BASE ** (-jnp.arange(0, half, 2, dtype=F32) / half)
    ang_r = r_pos[:, None] * inv_freq
    ang_c = c_pos[:, None] * inv_freq
    cos = jnp.concatenate([jnp.cos(ang_r)] * 2 + [jnp.cos(ang_c)] * 2, axis=-1)
    sin = jnp.concatenate([-jnp.sin(ang_r), jnp.sin(ang_r), -jnp.sin(ang_c), jnp.sin(ang_c)], axis=-1)
    return cos, sin


def _rope(proj, cos, sin, *, row_off, rows, seq, col_seg, d, hd, scale):
    tm = _tile(seq, ROW_TILE)
    off = row_off // tm
    per_seq = seq // tm
    return pl.pallas_call(
        functools.partial(_rope_kernel, n_groups=d // hd, hd=hd, scale=scale),
        grid=(rows // tm,),
        in_specs=[
            pl.BlockSpec((tm, d), lambda i: (i + off, col_seg)),
            pl.BlockSpec((tm, hd), lambda i: (i % per_seq, 0)),
            pl.BlockSpec((tm, hd), lambda i: (i % per_seq, 0)),
        ],
        out_specs=pl.BlockSpec((tm, d), lambda i: (i, 0)),
        out_shape=jax.ShapeDtypeStruct((rows, d), BF16),
        compiler_params=_params(("parallel",)),
        name="axial_rope",
    )(proj, cos, sin)


def _attn_kernel(lam_ref, q_ref, k_ref, v_ref, g_ref, o_ref, *, hd, q_scale, lam_init):
    lm = lam_ref[...]
    lam = (jnp.exp(jnp.sum(lm[0:1] * lm[1:2], axis=-1, keepdims=True))
           - jnp.exp(jnp.sum(lm[2:3] * lm[3:4], axis=-1, keepdims=True)) + lam_init)
    q = q_ref[...]
    if q_scale != 1.0:
        q = q * q_scale
    q = q.astype(BF16)
    k = k_ref[...].astype(BF16)

    def softmax_parts(qh, kh):
        s = _dot_nt(qh, kh)
        p = jnp.exp(s - jnp.max(s, axis=-1, keepdims=True))
        return p, 1.0 / jnp.sum(p, axis=-1, keepdims=True)

    p1, r1 = softmax_parts(q[:, :hd], k[:, :hd])
    p2, r2 = softmax_parts(q[:, hd:], k[:, hd:])
    a = p1 * r1 - p2 * (lam * r2)
    o = _dot(a.astype(BF16), v_ref[...].astype(BF16))
    y = o * lax.rsqrt(jnp.mean(o * o, axis=-1, keepdims=True) + EPS) * g_ref[...]
    o_ref[...] = (y * (1.0 - lam_init)).astype(o_ref.dtype)


def _diff_attention(q_arr, k_arr, v_arr, lam_params, subln_g, *, batch, tq_total, tk, n_heads, hd,
                    q_row_off, q_seg, k_row_off, k_seg, v_row_off, v_seg, q_scale, lam_init):
    w = 2 * hd
    tq = _tile(tq_total, ATTN_TQ)
    nq = tq_total // tq
    q_off, k_off, v_off = q_row_off // tq, k_row_off // tk, v_row_off // tk
    assert q_row_off % tq == 0 and k_row_off % tk == 0 and v_row_off % tk == 0
    d = n_heads * w
    return pl.pallas_call(
        functools.partial(_attn_kernel, hd=hd, q_scale=q_scale, lam_init=lam_init),
        grid=(batch, n_heads, nq),
        in_specs=[
            pl.BlockSpec(lam_params.shape, lambda b, h, i: (0, 0)),
            pl.BlockSpec((tq, w), lambda b, h, i: (q_off + b * nq + i, q_seg * n_heads + h)),
            pl.BlockSpec((tk, w), lambda b, h, i: (k_off + b, k_seg * n_heads + h)),
            pl.BlockSpec((tk, w), lambda b, h, i: (v_off + b, v_seg * n_heads + h)),
            pl.BlockSpec((1, w), lambda b, h, i: (0, 0)),
        ],
        out_specs=pl.BlockSpec((tq, w), lambda b, h, i: (b * nq + i, h)),
        out_shape=jax.ShapeDtypeStruct((batch * tq_total, d), BF16),
        compiler_params=_params(("parallel", "parallel", "arbitrary")),
        name="diff_attention",
    )(lam_params, q_arr, k_arr, v_arr, subln_g.reshape(1, w))


def kernel(x_prompt, x_sample, state_hgrn_fwd, state_hgrn_bwd, cache_attn_k, cache_attn_v, c, c_ctx,
           w_mod, b_mod, norm_g, w_ff1, w_ff2, hg_w_in, hg_w_out, hg_g_norm, hg_lb_logits,
           da_w_in, da_w_out, da_subln_g, da_lambda, final_norm_g):
    batch, seq, d = x_prompt.shape
    dec_batch, dec_seq, _ = x_sample.shape
    depth = w_mod.shape[0]
    _, _, hg_heads, hg_dk, hg_dv = state_hgrn_fwd.shape
    _, _, past_len, da_halves, da_hd = cache_attn_k.shape
    da_heads = da_halves // 2
    ctx_rows = batch * seq
    lat_rows = dec_batch * dec_seq
    groups = _Groups(ctx_rows, dec_batch, dec_seq)

    x = jnp.concatenate([x_prompt.reshape(ctx_rows, d), x_sample.reshape(lat_rows, d)], axis=0)

    n_groups = 1 + dec_batch
    g8 = -(-n_groups // V7X_SUBLANES) * V7X_SUBLANES
    cond = jnp.concatenate([c_ctx[None, :], c, jnp.zeros((g8 - n_groups, d), F32)], axis=0)
    mod = _modulation(cond, w_mod, b_mod).reshape(depth, g8, 1, 6 * d)
    SH1, SC1, GT1, SH2, SC2, GT2 = range(6)

    new_sf, new_sb, new_k, new_v = [], [], [], []
    for layer in range(depth):
        h = _norm_mod(x, norm_g[layer, 0], mod, layer, SC1, SH1, groups)
        j = layer // N_MIXERS
        if layer % N_MIXERS == 0:
            proj = _matmul(h, hg_w_in[j].astype(BF16), F32)
            outs = []
            for reverse in (False, True):
                common = dict(n_heads=hg_heads, dk=hg_dk, dv=hg_dv, d=d, f_seg=3 if reverse else 2,
                              reverse=reverse, layer=layer)
                lbl = hg_lb_logits[1 if reverse else 0]
                s_lat = (state_hgrn_bwd if reverse else state_hgrn_fwd)[:, j]
                o_ctx, s_ctx = _hg_scan(proj, lbl, None, row_off=0, batch=batch, seq=seq, **common)
                o_lat, _ = _hg_scan(proj, lbl, s_lat, row_off=ctx_rows, batch=dec_batch, seq=dec_seq,
                                    **common)
                outs.append(jnp.concatenate([o_ctx, o_lat], axis=0))
                (new_sb if reverse else new_sf).append(s_ctx)
            mixed = _hg_out(outs[0], outs[1], proj, hg_g_norm[j], n_heads=hg_heads, dv=hg_dv, g_seg=4)
            w_out = hg_w_out[j]
        else:
            lam_init = 0.8 - 0.6 * math.exp(-0.3 * layer)
            scale = da_hd ** -0.5
            proj = _matmul(h, da_w_in[j].astype(BF16), F32)
            att = dict(n_heads=da_heads, hd=da_hd, lam_init=lam_init)
            o_ctx = _diff_attention(proj, proj, proj, da_lambda[j], da_subln_g[j], batch=batch,
                                    tq_total=seq, tk=seq, q_row_off=0, q_seg=0, k_row_off=0, k_seg=1,
                                    v_row_off=0, v_seg=2, q_scale=scale, **att)
            new_k.append(proj[:ctx_rows, d:2 * d].reshape(batch, seq, da_halves, da_hd))
            new_v.append(proj[:ctx_rows, 2 * d:].reshape(batch, seq, da_heads, 2 * da_hd))
            cos, sin = _rope_tables(dec_seq, da_hd)
            rope = dict(row_off=ctx_rows, rows=lat_rows, seq=dec_seq, d=d, hd=da_hd)
            q_lat = _rope(proj, cos, sin, col_seg=0, scale=scale, **rope)
            k_lat = _rope(proj, cos, sin, col_seg=1, scale=1.0, **rope)
            tk = past_len + dec_seq
            k_all = jnp.concatenate([cache_attn_k[:, j].reshape(dec_batch, past_len, d).astype(BF16),
                                     k_lat.reshape(dec_batch, dec_seq, d)], axis=1).reshape(dec_batch * tk, d)
            v_all = jnp.concatenate([cache_attn_v[:, j].reshape(dec_batch, past_len, d),
                                     proj[ctx_rows:, 2 * d:].reshape(dec_batch, dec_seq, d)],
                                    axis=1).astype(BF16).reshape(dec_batch * tk, d)
            o_lat = _diff_attention(q_lat, k_all, v_all, da_lambda[j], da_subln_g[j], batch=dec_batch,
                                    tq_total=dec_seq, tk=tk, q_row_off=0, q_seg=0, k_row_off=0, k_seg=0,
                                    v_row_off=0, v_seg=0, q_scale=1.0, **att)
            mixed = jnp.concatenate([o_ctx, o_lat], axis=0)
            w_out = da_w_out[j]
        x = _matmul(mixed, w_out.astype(BF16), F32, "residual", res=x, mod=mod, layer=layer,
                    gate_chunk=GT1, groups=groups)
        h = _norm_mod(x, norm_g[layer, 1], mod, layer, SC2, SH2, groups)
        u = _matmul(h, w_ff1[layer].astype(BF16), BF16, "relu2")
        x = _matmul(u, w_ff2[layer].astype(BF16), F32, "residual", res=x, mod=mod, layer=layer,
                    gate_chunk=GT2, groups=groups)

    y_prompt = _final_norm(x, final_norm_g, 0, ctx_rows).reshape(batch, seq, d)
    y_sample = _final_norm(x, final_norm_g, ctx_rows, lat_rows).reshape(dec_batch, dec_seq, d)
    return (y_prompt, y_sample, jnp.stack(new_sf, axis=1), jnp.stack(new_sb, axis=1),
            jnp.stack(new_k, axis=1), jnp.stack(new_v, axis=1))
```

```python
import functools
import math
from typing import NamedTuple

import jax
import jax.numpy as jnp
from jax import lax
from jax.experimental import pallas as pl
from jax.experimental.pallas import tpu as pltpu

F32 = jnp.float32
BF16 = jnp.bfloat16

GRID_W = 64
EPS = 1e-6
ROPE_BASE = 10000.0
N_MIXERS = 2

V7X_SUBLANES = 8
V7X_LANES = 128
V7X_VMEM_LIMIT_BYTES = 56 * 1024 * 1024

MM_BM = 1024
MM_BN = 1024
MM_VMEM_BUDGET_BYTES = 44 * 1024 * 1024
ROW_TILE = 256
MOD_BN = 512
HG_CHUNK = 128
HG_HEADS_PER_STEP = 8
ATTN_TQ = 256
ATTN_CTX_HEADS_PER_STEP = 4


def _tile(full, want):
    t = min(full, want)
    assert full % t == 0, (full, want)
    return t


def _params(sem):
    return pltpu.CompilerParams(dimension_semantics=sem, vmem_limit_bytes=V7X_VMEM_LIMIT_BYTES)


def _silu(x):
    return x * jax.nn.sigmoid(x)


def _dot(a, b):
    return jnp.dot(a, b, preferred_element_type=F32)


def _dot_nt(a, b):
    return lax.dot_general(a, b, (((1,), (1,)), ((), ())), preferred_element_type=F32)


def _dot_tn(a, b):
    return lax.dot_general(a, b, (((0,), (0,)), ((), ())), preferred_element_type=F32)


class _Groups:
    def __init__(self, ctx_rows, dec_batch, dec_seq):
        self.ctx_rows = ctx_rows
        self.dec_batch = dec_batch
        self.dec_seq = dec_seq
        self.rows = ctx_rows + dec_batch * dec_seq

    def of_tile(self, i, tile):
        assert self.ctx_rows % tile == 0 and self.dec_seq % tile == 0
        g = jnp.int32(0)
        for b in range(self.dec_batch):
            g = g + (i >= (self.ctx_rows + b * self.dec_seq) // tile).astype(jnp.int32)
        return g


def _mod_kernel(c_ref, w_ref, b_ref, o_ref):
    a = _silu(c_ref[...]).astype(BF16)
    o_ref[...] = _dot(a, w_ref[...].astype(BF16)) + b_ref[...]


def _modulation(cond, w_mod, b_mod):
    g8, d = cond.shape
    depth, _, n = w_mod.shape
    bn = _tile(n, MOD_BN)
    return pl.pallas_call(
        _mod_kernel,
        grid=(depth, n // bn),
        in_specs=[
            pl.BlockSpec((g8, d), lambda l, j: (0, 0)),
            pl.BlockSpec((None, d, bn), lambda l, j: (l, 0, j)),
            pl.BlockSpec((None, 1, bn), lambda l, j: (l, 0, j)),
        ],
        out_specs=pl.BlockSpec((None, g8, bn), lambda l, j: (l, 0, j)),
        out_shape=jax.ShapeDtypeStruct((depth, g8, n), F32),
        compiler_params=_params(("arbitrary", "arbitrary")),
        name="adaln_modulation",
    )(cond, w_mod, b_mod.reshape(depth, 1, n))


def _norm_kernel(*refs, modulated):
    if modulated:
        x_ref, g_ref, sc_ref, sh_ref, o_ref = refs
    else:
        x_ref, g_ref, o_ref = refs
    x = x_ref[...]
    y = x * lax.rsqrt(jnp.mean(x * x, axis=-1, keepdims=True) + EPS) * g_ref[...]
    if modulated:
        y = y * (1.0 + sc_ref[...]) + sh_ref[...]
    o_ref[...] = y.astype(o_ref.dtype)


def _norm_mod(x, g, mod, layer, sc_chunk, sh_chunk, groups):
    rows, d = x.shape
    tm = _tile(math.gcd(groups.ctx_rows, groups.dec_seq), ROW_TILE)

    def vec(chunk):
        return pl.BlockSpec((None, None, 1, d), lambda i: (layer, groups.of_tile(i, tm), 0, chunk))

    return pl.pallas_call(
        functools.partial(_norm_kernel, modulated=True),
        grid=(rows // tm,),
        in_specs=[
            pl.BlockSpec((tm, d), lambda i: (i, 0)),
            pl.BlockSpec((1, d), lambda i: (0, 0)),
            vec(sc_chunk),
            vec(sh_chunk),
        ],
        out_specs=pl.BlockSpec((tm, d), lambda i: (i, 0)),
        out_shape=jax.ShapeDtypeStruct((rows, d), BF16),
        compiler_params=_params(("parallel",)),
        name="rmsnorm_modulate",
    )(x, g.reshape(1, d), mod, mod)


def _final_norm(x, g, row_off, rows):
    d = x.shape[1]
    tm = _tile(math.gcd(rows, row_off) if row_off else rows, ROW_TILE)
    off = row_off // tm
    return pl.pallas_call(
        functools.partial(_norm_kernel, modulated=False),
        grid=(rows // tm,),
        in_specs=[
            pl.BlockSpec((tm, d), lambda i: (i + off, 0)),
            pl.BlockSpec((1, d), lambda i: (0, 0)),
        ],
        out_specs=pl.BlockSpec((tm, d), lambda i: (i, 0)),
        out_shape=jax.ShapeDtypeStruct((rows, d), F32),
        compiler_params=_params(("parallel",)),
        name="final_rmsnorm",
    )(x, g.reshape(1, d))


def _mm_kernel(*refs, nk, epilogue, n_split, nb_split):
    if epilogue == "residual":
        x_ref, w_ref, res_ref, gate_ref, o_ref = refs
        o_refs = [o_ref]
    else:
        x_ref, w_ref = refs[:2]
        o_refs = refs[2:]

    def finish(acc):
        if epilogue == "relu2":
            r = jnp.maximum(acc, 0.0)
            acc = r * r
        elif epilogue == "residual":
            acc = res_ref[...] + gate_ref[...] * acc
        return acc

    def product_to(o_ref):
        o_ref[...] = finish(_dot(x_ref[...], w_ref[...])).astype(o_ref.dtype)

    if nk == 1 and n_split == 1:
        product_to(o_refs[0])
        return
    if nk == 1:
        for s, o_ref in enumerate(o_refs):
            pl.when(pl.program_id(1) // nb_split == s)(functools.partial(product_to, o_ref))
        return

    o_ref = o_refs[0]
    k = pl.program_id(2)

    @pl.when(k == 0)
    def _():
        o_ref[...] = jnp.zeros(o_ref.shape, o_ref.dtype)

    o_ref[...] += _dot(x_ref[...], w_ref[...])

    @pl.when(k == nk - 1)
    def _():
        o_ref[...] = finish(o_ref[...])


def _matmul(x, w, out_dtype, epilogue="plain", *, row_off=0, rows=None, n_split=1,
            res=None, mod=None, layer=None, gate_chunk=None, groups=None):
    kdim = x.shape[1]
    m = x.shape[0] if rows is None else rows
    n = w.shape[1]
    bm = _tile(m, MM_BM)
    out_bytes = jnp.dtype(out_dtype).itemsize * n_split + (4 if epilogue == "residual" else 0)

    def fits(bn, bk):
        return 2 * (2 * bk * (bm + bn) + bm * bn * out_bytes) + 4 * bm * bn <= MM_VMEM_BUDGET_BYTES

    bn_full = _tile(n // n_split, MM_BN)
    candidates = [(bn_full, kdim), (max(bn_full // 2, V7X_LANES), kdim)]
    candidates += [(bn_full, kdim // s) for s in (2, 4, 8, 16, 32)]
    bn, bk = next((bn_, bk_) for bn_, bk_ in candidates if kdim % bk_ == 0 and fits(bn_, bk_))
    nk = kdim // bk
    assert nk == 1 or (out_dtype == F32 and n_split == 1)
    assert row_off % bm == 0 and (n // n_split) % bn == 0
    off = row_off // bm
    nb_split = n // n_split // bn
    in_specs = [
        pl.BlockSpec((bm, bk), lambda i, j, k: (i + off, k)),
        pl.BlockSpec((bk, bn), lambda i, j, k: (k, j)),
    ]
    args = [x, w]
    if epilogue == "residual":
        assert row_off == 0 and n_split == 1
        nb = n // bn
        in_specs += [
            pl.BlockSpec((bm, bn), lambda i, j, k: (i, j)),
            pl.BlockSpec((None, None, 1, bn),
                         lambda i, j, k: (layer, groups.of_tile(i, bm), 0, gate_chunk * nb + j)),
        ]
        args += [res, mod]

    def out_spec(s):
        return pl.BlockSpec((bm, bn), lambda i, j, k: (i, jnp.clip(j - s * nb_split, 0, nb_split - 1)))

    outs = pl.pallas_call(
        functools.partial(_mm_kernel, nk=nk, epilogue=epilogue, n_split=n_split, nb_split=nb_split),
        grid=(m // bm, n // bn, nk),
        in_specs=in_specs,
        out_specs=[out_spec(s) for s in range(n_split)],
        out_shape=[jax.ShapeDtypeStruct((m, n // n_split), out_dtype)] * n_split,
        compiler_params=_params(("parallel", "arbitrary", "arbitrary")),
        name="matmul_" + epilogue,
    )(*args)
    return outs[0] if n_split == 1 else outs


class _ScanLayout(NamedTuple):
    n_ctx: int
    per_ctx: int
    per_lat: int
    total: int
    batch: int
    reverse: bool

    def pos(self, g):
        r = self.total - 1 - g if self.reverse else g
        is_lat = r >= self.n_ctx
        rl = jnp.maximum(r - self.n_ctx, 0)
        seq_ctx = jnp.minimum(r // self.per_ctx, self.batch - 1)
        seq_lat = rl // self.per_lat
        pos = jnp.where(is_lat, rl % self.per_lat, r % self.per_ctx)
        per = jnp.where(is_lat, self.per_lat, self.per_ctx)
        start, stop = (per - 1, 0) if self.reverse else (0, per - 1)
        return r, is_lat, seq_ctx, seq_lat, pos == start, pos == stop


def _level_ref(c, half, reverse):
    rows, width = c.shape
    blk = 2 * half
    idx = half if reverse else half - 1
    if blk >= V7X_SUBLANES:
        c3 = c.reshape(rows // blk, blk, width)
        ref = jnp.broadcast_to(c3[:, idx:idx + 1, :], c3.shape)
        return ref.reshape(rows, width)
    pos = lax.broadcasted_iota(jnp.int32, c.shape, 0) & (blk - 1)
    out = c
    for delta in range(-idx, blk - idx):
        if delta != 0:
            out = jnp.where(pos == idx + delta, pltpu.roll(c, delta % rows, 0), out)
    return out


def _exp2_neg_abs(d):
    bits = lax.bitcast_convert_type(d, jnp.uint32) | jnp.uint32(0x80000000)
    return jnp.exp2(lax.bitcast_convert_type(bits, F32))


def _hg_scan_kernel(*refs, chunk, heads, dk, dv, layer, lay, fuse_out):
    if fuse_out:
        (q_ref, v_ref, f_ref, lbl_ref, s0_ref, ofwd_ref, g_ref, gn_ref,
         o_ref, sfin_ref, st_ref) = refs
    else:
        q_ref, v_ref, f_ref, lbl_ref, s0_ref, o_ref, sfin_ref, st_ref = refs
    reverse = lay.reverse
    _, is_lat, _, _, first, last = lay.pos(pl.program_id(1))
    n_levels = chunk.bit_length() - 1

    @pl.when(jnp.logical_and(first, is_lat))
    def _():
        for j in range(heads):
            st_ref[j] = s0_ref[0, j].T

    @pl.when(jnp.logical_and(first, jnp.logical_not(is_lat)))
    def _():
        st_ref[...] = jnp.zeros(st_ref.shape, F32)

    lbl = lbl_ref[...]
    e = jnp.exp(lbl - jnp.max(lbl, axis=0, keepdims=True))
    lb = jnp.sum(e[:layer + 1], axis=0, keepdims=True) / jnp.sum(e, axis=0, keepdims=True)

    fg = lb + (1.0 - lb) * jax.nn.sigmoid(f_ref[...])
    k_all = 1.0 - fg
    log2f = jnp.log2(fg)

    row = lax.broadcasted_iota(jnp.int32, (chunk, chunk), 0)
    col = lax.broadcasted_iota(jnp.int32, (chunk, chunk), 1)
    before = (col > row) if reverse else (col < row)
    tri = jnp.logical_or(before, col == row).astype(BF16)
    hi = log2f.astype(BF16)
    r1 = log2f - hi.astype(F32)
    mid = r1.astype(BF16)
    lo = (r1 - mid.astype(F32)).astype(BF16)
    c_all = _dot(tri, hi) + _dot(tri, mid) + _dot(tri, lo)

    x = row ^ col
    lvl = jnp.zeros((chunk, chunk), jnp.int32)
    for b in range(n_levels):
        lvl = lvl + (x >= (1 << b)).astype(jnp.int32)
    lvl = jnp.where(jnp.logical_or(before, x == 0), lvl, -1)

    end = 0 if reverse else chunk - 1
    if fuse_out:
        gn = gn_ref[...]
    for j in range(heads):
        sl = slice(j * dk, (j + 1) * dk)
        sv = slice(j * dv, (j + 1) * dv)
        q = _silu(q_ref[:, sl])
        k = k_all[:, sl]
        c = c_all[:, sl]
        v = v_ref[:, sv].astype(BF16)
        c_end = c[end:end + 1, :]

        a = jnp.where(lvl == 0, _dot_nt(q.astype(BF16), k.astype(BF16)), 0.0)
        for level in range(1, n_levels + 1):
            w = _exp2_neg_abs(c - _level_ref(c, 1 << (level - 1), reverse))
            a_l = _dot_nt((q * w).astype(BF16), (k * w).astype(BF16))
            a = jnp.where(lvl == level, a_l, a)

        st = st_ref[j]
        o = _dot(a.astype(BF16), v) + _dot_nt((q * jnp.exp2(c)).astype(BF16), st.astype(BF16))
        if fuse_out:
            o = o + ofwd_ref[:, sv]
            y = o * lax.rsqrt(jnp.mean(o * o, axis=-1, keepdims=True) + EPS) * gn
            o = y * _silu(g_ref[:, sv])
        o_ref[:, sv] = o.astype(o_ref.dtype)
        k_dec = (k * jnp.exp2(c_end - c)).astype(BF16)
        st_ref[j] = st * jnp.exp2(c_end) + _dot_tn(v, k_dec)

    @pl.when(jnp.logical_and(last, jnp.logical_not(is_lat)))
    def _():
        for j in range(heads):
            sfin_ref[0, j] = st_ref[j].T


def _hg_scan(proj, lb_logits_dir, s0_lat, *, groups, batch, seq, n_heads, dk, dv, d, f_seg, reverse, layer,
             o_fwd=None, g_norm=None, g_seg=None):
    chunk = _tile(math.gcd(seq, groups.dec_seq), HG_CHUNK)
    heads = _tile(n_heads, HG_HEADS_PER_STEP)
    assert dk == dv and d == n_heads * dk
    w = heads * dk
    segs = d // w
    lay = _ScanLayout(n_ctx=groups.ctx_rows // chunk, per_ctx=seq // chunk, per_lat=groups.dec_seq // chunk,
                      total=groups.rows // chunk, batch=batch, reverse=reverse)
    fuse_out = o_fwd is not None

    def seg_spec(seg):
        return pl.BlockSpec((chunk, w), lambda h, g: (lay.pos(g)[0], seg * segs + h))

    n_lb = lb_logits_dir.shape[0]
    in_specs = [seg_spec(0), seg_spec(1), seg_spec(f_seg),
                pl.BlockSpec((n_lb, w), lambda h, g: (0, h)),
                pl.BlockSpec((1, heads, dk, dv), lambda h, g: (lay.pos(g)[3], h, 0, 0))]
    args = [proj, proj, proj, lb_logits_dir, s0_lat]
    if fuse_out:
        in_specs += [seg_spec(0), seg_spec(g_seg), pl.BlockSpec((1, dv), lambda h, g: (0, 0))]
        args += [o_fwd, proj, g_norm.reshape(1, dv)]
    o, s_fin = pl.pallas_call(
        functools.partial(_hg_scan_kernel, chunk=chunk, heads=heads, dk=dk, dv=dv,
                          layer=layer, lay=lay, fuse_out=fuse_out),
        grid=(n_heads // heads, lay.total),
        in_specs=in_specs,
        out_specs=[
            pl.BlockSpec((chunk, w), lambda h, g: (lay.pos(g)[0], h)),
            pl.BlockSpec((1, heads, dk, dv), lambda h, g: (lay.pos(g)[2], h, 0, 0)),
        ],
        out_shape=[
            jax.ShapeDtypeStruct((groups.rows, d), BF16 if fuse_out else F32),
            jax.ShapeDtypeStruct((batch, n_heads, dk, dv), F32),
        ],
        scratch_shapes=[pltpu.VMEM((heads, dv, dk), F32)],
        compiler_params=_params(("parallel", "arbitrary")),
        name="hgrn2_scan_" + ("bwd" if reverse else "fwd"),
    )(*args)
    return o, s_fin


def _rope_tables(seq, hd):
    half = hd // 2
    rows = seq // GRID_W
    r_pos = jnp.repeat(jnp.arange(rows), GRID_W).astype(F32)
    c_pos = jnp.tile(jnp.arange(GRID_W), rows).astype(F32)
    inv_freq = ROPE_BASE ** (-jnp.arange(0, half, 2, dtype=F32) / half)
    ang_r = r_pos[:, None] * inv_freq
    ang_c = c_pos[:, None] * inv_freq
    cos = jnp.concatenate([jnp.cos(ang_r)] * 2 + [jnp.cos(ang_c)] * 2, axis=-1)
    sin = jnp.concatenate([-jnp.sin(ang_r), jnp.sin(ang_r), -jnp.sin(ang_c), jnp.sin(ang_c)], axis=-1)
    return cos, sin


def _rope_tile(x, cos, sin):
    hd = x.shape[1]
    quarter = hd // 4
    lane = lax.broadcasted_iota(jnp.int32, x.shape, 1)
    first = (lane % (2 * quarter)) < quarter
    partner = jnp.where(first, pltpu.roll(x, hd - quarter, 1), pltpu.roll(x, quarter, 1))
    return x * cos + partner * sin


def _attn_kernel(*refs, hd, heads, tq, past, scale, lam_init, latent):
    if latent:
        (lam_ref, q_ref, k_ref, v_ref, kc_ref, vc_ref, cos_ref, sin_ref, g_ref, _,
         o_ref, kall_ref, vall_ref) = refs
    else:
        lam_ref, q_ref, k_ref, v_ref, g_ref, o_ref = refs
    w = 2 * hd
    lm = lam_ref[...]
    lam = (jnp.exp(jnp.sum(lm[0:1] * lm[1:2], axis=-1, keepdims=True))
           - jnp.exp(jnp.sum(lm[2:3] * lm[3:4], axis=-1, keepdims=True)) + lam_init)
    g = g_ref[...]

    if latent:
        i = pl.program_id(2)

        @pl.when(i == 0)
        def _():
            kall_ref[0:past, :] = kc_ref[...].astype(BF16)
            vall_ref[0:past, :] = vc_ref[...].astype(BF16)
            vall_ref[past:, :] = v_ref[...].astype(BF16)
            for half in range(2):
                hs = slice(half * hd, (half + 1) * hd)
                kall_ref[past:, hs] = _rope_tile(k_ref[:, hs], cos_ref[...], sin_ref[...]).astype(BF16)

        rows = pl.ds(pl.multiple_of(i * tq, tq), tq)
        cos_q = cos_ref[rows, :]
        sin_q = sin_ref[rows, :]

    def softmax_parts(qh, kh):
        s = _dot_nt(qh, kh)
        p = jnp.exp(s - jnp.max(s, axis=-1, keepdims=True))
        return p, 1.0 / jnp.sum(p, axis=-1, keepdims=True)

    for j in range(heads):
        parts = []
        for half in range(2):
            hs = slice(j * w + half * hd, j * w + (half + 1) * hd)
            qh = q_ref[:, hs]
            if latent:
                qh = _rope_tile(qh, cos_q, sin_q)
                kh = kall_ref[:, hs]
            else:
                kh = k_ref[:, hs].astype(BF16)
            parts.append(softmax_parts((qh * scale).astype(BF16), kh))
        (p1, r1), (p2, r2) = parts
        a = p1 * r1 - p2 * (lam * r2)
        js = slice(j * w, (j + 1) * w)
        vj = vall_ref[...] if latent else v_ref[:, js].astype(BF16)
        o = _dot(a.astype(BF16), vj)
        y = o * lax.rsqrt(jnp.mean(o * o, axis=-1, keepdims=True) + EPS) * g
        o_ref[:, js] = (y * (1.0 - lam_init)).astype(o_ref.dtype)


def _diff_attention(q, k, v, lam_params, subln_g, *, batch, seq, n_heads, hd, lam_init, out_rows, out_row_off,
                    cache_k=None, cache_v=None, prev=None):
    latent = cache_k is not None
    w = 2 * hd
    d = n_heads * w
    tq = _tile(seq, ATTN_TQ)
    nq = seq // tq
    heads = 1 if latent else _tile(n_heads, ATTN_CTX_HEADS_PER_STEP)
    assert out_row_off % tq == 0
    out_off = out_row_off // tq
    scale = hd ** -0.5
    past = cache_k.shape[0] // batch if latent else 0

    def kv_spec(rows):
        return pl.BlockSpec((rows, heads * w), lambda b, h, i: (b, h))

    in_specs = [pl.BlockSpec(lam_params.shape, lambda b, h, i: (0, 0)),
                pl.BlockSpec((tq, heads * w), lambda b, h, i: (b * nq + i, h)),
                kv_spec(seq), kv_spec(seq)]
    args = [lam_params, q, k, v]
    scratch = []
    if latent:
        cos, sin = _rope_tables(seq, hd)
        table = pl.BlockSpec((seq, hd), lambda b, h, i: (0, 0))
        in_specs += [kv_spec(past), kv_spec(past), table, table]
        args += [cache_k, cache_v, cos, sin]
        scratch = [pltpu.VMEM((past + seq, w), BF16), pltpu.VMEM((past + seq, w), BF16)]
    in_specs.append(pl.BlockSpec((1, w), lambda b, h, i: (0, 0)))
    args.append(subln_g.reshape(1, w))
    aliases = {}
    if prev is not None:
        assert latent
        in_specs.append(pl.BlockSpec(memory_space=pl.ANY))
        args.append(prev)
        aliases = {len(args) - 1: 0}
    return pl.pallas_call(
        functools.partial(_attn_kernel, hd=hd, heads=heads, tq=tq, past=past, scale=scale,
                          lam_init=lam_init, latent=latent),
        grid=(batch, n_heads // heads, nq),
        in_specs=in_specs,
        out_specs=pl.BlockSpec((tq, heads * w), lambda b, h, i: (out_off + b * nq + i, h)),
        out_shape=jax.ShapeDtypeStruct((out_rows, d), BF16),
        scratch_shapes=scratch,
        input_output_aliases=aliases,
        compiler_params=_params(("parallel", "parallel", "arbitrary")),
        name="diff_attention_" + ("latent" if latent else "context"),
    )(*args)


def kernel(x_prompt, x_sample, state_hgrn_fwd, state_hgrn_bwd, cache_attn_k, cache_attn_v, c, c_ctx,
           w_mod, b_mod, norm_g, w_ff1, w_ff2, hg_w_in, hg_w_out, hg_g_norm, hg_lb_logits,
           da_w_in, da_w_out, da_subln_g, da_lambda, final_norm_g):
    batch, seq, d = x_prompt.shape
    dec_batch, dec_seq, _ = x_sample.shape
    depth = w_mod.shape[0]
    _, _, hg_heads, hg_dk, hg_dv = state_hgrn_fwd.shape
    _, _, past_len, da_halves, da_hd = cache_attn_k.shape
    da_heads = da_halves // 2
    ctx_rows = batch * seq
    lat_rows = dec_batch * dec_seq
    groups = _Groups(ctx_rows, dec_batch, dec_seq)

    x = jnp.concatenate([x_prompt.reshape(ctx_rows, d), x_sample.reshape(lat_rows, d)], axis=0)

    n_groups = 1 + dec_batch
    g8 = -(-n_groups // V7X_SUBLANES) * V7X_SUBLANES
    cond = jnp.concatenate([c_ctx[None, :], c, jnp.zeros((g8 - n_groups, d), F32)], axis=0)
    mod = _modulation(cond, w_mod, b_mod).reshape(depth, g8, 1, 6 * d)
    SH1, SC1, GT1, SH2, SC2, GT2 = range(6)

    new_sf, new_sb, new_k, new_v = [], [], [], []
    for layer in range(depth):
        h = _norm_mod(x, norm_g[layer, 0], mod, layer, SC1, SH1, groups)
        j = layer // N_MIXERS
        if layer % N_MIXERS == 0:
            proj = _matmul(h, hg_w_in[j].astype(BF16), F32)
            scan = dict(groups=groups, batch=batch, seq=seq, n_heads=hg_heads, dk=hg_dk, dv=hg_dv, d=d,
                        layer=layer)
            o_fwd, s_fwd = _hg_scan(proj, hg_lb_logits[0], state_hgrn_fwd[:, j], f_seg=2, reverse=False, **scan)
            mixed, s_bwd = _hg_scan(proj, hg_lb_logits[1], state_hgrn_bwd[:, j], f_seg=3, reverse=True,
                                    o_fwd=o_fwd, g_norm=hg_g_norm[j], g_seg=4, **scan)
            new_sf.append(s_fwd)
            new_sb.append(s_bwd)
            w_out = hg_w_out[j]
        else:
            lam_init = 0.8 - 0.6 * math.exp(-0.3 * layer)
            w_in = da_w_in[j].astype(BF16)
            q_ctx, k_ctx, v_ctx = _matmul(h, w_in, F32, row_off=0, rows=ctx_rows, n_split=3)
            q_lat, k_lat, v_lat = _matmul(h, w_in, F32, row_off=ctx_rows, rows=lat_rows, n_split=3)
            new_k.append(k_ctx.reshape(batch, seq, da_halves, da_hd))
            new_v.append(v_ctx.reshape(batch, seq, da_heads, 2 * da_hd))
            att = dict(n_heads=da_heads, hd=da_hd, lam_init=lam_init, out_rows=groups.rows)
            mixed = _diff_attention(q_ctx, k_ctx, v_ctx, da_lambda[j], da_subln_g[j], batch=batch, seq=seq,
                                    out_row_off=0, **att)
            mixed = _diff_attention(q_lat, k_lat, v_lat, da_lambda[j], da_subln_g[j], batch=dec_batch,
                                    seq=dec_seq, out_row_off=ctx_rows,
                                    cache_k=cache_attn_k[:, j].reshape(dec_batch * past_len, d),
                                    cache_v=cache_attn_v[:, j].reshape(dec_batch * past_len, d),
                                    prev=mixed, **att)
            w_out = da_w_out[j]
        x = _matmul(mixed, w_out.astype(BF16), F32, "residual", res=x, mod=mod, layer=layer,
                    gate_chunk=GT1, groups=groups)
        h = _norm_mod(x, norm_g[layer, 1], mod, layer, SC2, SH2, groups)
        u = _matmul(h, w_ff1[layer].astype(BF16), BF16, "relu2")
        x = _matmul(u, w_ff2[layer].astype(BF16), F32, "residual", res=x, mod=mod, layer=layer,
                    gate_chunk=GT2, groups=groups)

    y_prompt = _final_norm(x, final_norm_g, 0, ctx_rows).reshape(batch, seq, d)
    y_sample = _final_norm(x, final_norm_g, ctx_rows, lat_rows).reshape(dec_batch, dec_seq, d)
    return (y_prompt, y_sample, jnp.stack(new_sf, axis=1), jnp.stack(new_sb, axis=1),
            jnp.stack(new_k, axis=1), jnp.stack(new_v, axis=1))
```

```python
import functools
import math
from typing import NamedTuple

import jax
import jax.numpy as jnp
from jax import lax
from jax.experimental import pallas as pl
from jax.experimental.pallas import tpu as pltpu

F32 = jnp.float32
BF16 = jnp.bfloat16

GRID_W = 64
EPS = 1e-6
ROPE_BASE = 10000.0
N_MIXERS = 2

V7X_SUBLANES = 8
V7X_LANES = 128
V7X_VMEM_LIMIT_BYTES = 56 * 1024 * 1024

MM_BM = 1024
MM_BN = 1024
MM_VMEM_BUDGET_BYTES = 44 * 1024 * 1024
ROW_TILE = 256
MOD_BN = 512
HG_CHUNK = 128
HG_HEADS_PER_STEP = 16
ATTN_TQ = 256
ATTN_CTX_HEADS_PER_STEP = 4

LOG2E = 1.4426950408889634


def _tile(full, want):
    t = min(full, want)
    assert full % t == 0, (full, want)
    return t


def _params(sem):
    return pltpu.CompilerParams(dimension_semantics=sem, vmem_limit_bytes=V7X_VMEM_LIMIT_BYTES)


def _sigmoid(x):
    return 0.5 * jnp.tanh(0.5 * x) + 0.5


def _silu(x):
    return x * _sigmoid(x)


def _dot(a, b):
    return jnp.dot(a, b, preferred_element_type=F32)


def _dot_nt(a, b):
    return lax.dot_general(a, b, (((1,), (1,)), ((), ())), preferred_element_type=F32)


def _dot_tn(a, b):
    return lax.dot_general(a, b, (((0,), (0,)), ((), ())), preferred_element_type=F32)


class _Groups:
    def __init__(self, ctx_rows, dec_batch, dec_seq):
        self.ctx_rows = ctx_rows
        self.dec_batch = dec_batch
        self.dec_seq = dec_seq
        self.rows = ctx_rows + dec_batch * dec_seq

    def of_tile(self, i, tile):
        assert self.ctx_rows % tile == 0 and self.dec_seq % tile == 0
        g = jnp.int32(0)
        for b in range(self.dec_batch):
            g = g + (i >= (self.ctx_rows + b * self.dec_seq) // tile).astype(jnp.int32)
        return g


def _mod_kernel(c_ref, w_ref, b_ref, o_ref):
    a = _silu(c_ref[...]).astype(BF16)
    o_ref[...] = _dot(a, w_ref[...].astype(BF16)) + b_ref[...]


def _modulation(cond, w_mod, b_mod):
    g8, d = cond.shape
    depth, _, n = w_mod.shape
    bn = _tile(n, MOD_BN)
    return pl.pallas_call(
        _mod_kernel,
        grid=(depth, n // bn),
        in_specs=[
            pl.BlockSpec((g8, d), lambda l, j: (0, 0)),
            pl.BlockSpec((None, d, bn), lambda l, j: (l, 0, j)),
            pl.BlockSpec((None, 1, bn), lambda l, j: (l, 0, j)),
        ],
        out_specs=pl.BlockSpec((None, g8, bn), lambda l, j: (l, 0, j)),
        out_shape=jax.ShapeDtypeStruct((depth, g8, n), F32),
        compiler_params=_params(("arbitrary", "arbitrary")),
        name="adaln_modulation",
    )(cond, w_mod, b_mod.reshape(depth, 1, n))


def _norm_kernel(*refs, modulated, x_split):
    if modulated:
        *x_refs, g_ref, sc_ref, sh_ref, o_ref = refs
    else:
        *x_refs, g_ref, o_ref = refs
    x = x_refs[0][...]
    if len(x_refs) == 2:
        x = jnp.where(pl.program_id(0) < x_split, x, x_refs[1][...])
    y = x * lax.rsqrt(jnp.mean(x * x, axis=-1, keepdims=True) + EPS) * g_ref[...]
    if modulated:
        y = y * (1.0 + sc_ref[...]) + sh_ref[...]
    o_ref[...] = y.astype(o_ref.dtype)


def _norm_mod(x, g, mod, layer, sc_chunk, sh_chunk, groups):
    rows, d = groups.rows, g.shape[0]
    tm = _tile(math.gcd(groups.ctx_rows, groups.dec_seq), ROW_TILE)

    def vec(chunk):
        return pl.BlockSpec((None, None, 1, d), lambda i: (layer, groups.of_tile(i, tm), 0, chunk))

    if isinstance(x, tuple):
        x_split = x[0].shape[0] // tm
        x_specs = [pl.BlockSpec((tm, d), lambda i: (jnp.minimum(i, x_split - 1), 0)),
                   pl.BlockSpec((tm, d), lambda i: (jnp.maximum(i - x_split, 0), 0))]
        xs = list(x)
    else:
        x_split = 0
        x_specs = [pl.BlockSpec((tm, d), lambda i: (i, 0))]
        xs = [x]
    return pl.pallas_call(
        functools.partial(_norm_kernel, modulated=True, x_split=x_split),
        grid=(rows // tm,),
        in_specs=x_specs + [
            pl.BlockSpec((1, d), lambda i: (0, 0)),
            vec(sc_chunk),
            vec(sh_chunk),
        ],
        out_specs=pl.BlockSpec((tm, d), lambda i: (i, 0)),
        out_shape=jax.ShapeDtypeStruct((rows, d), BF16),
        compiler_params=_params(("parallel",)),
        name="rmsnorm_modulate",
    )(*xs, g.reshape(1, d), mod, mod)


def _final_norm(x, g, row_off, rows):
    d = x.shape[1]
    tm = _tile(math.gcd(rows, row_off) if row_off else rows, ROW_TILE)
    off = row_off // tm
    return pl.pallas_call(
        functools.partial(_norm_kernel, modulated=False, x_split=0),
        grid=(rows // tm,),
        in_specs=[
            pl.BlockSpec((tm, d), lambda i: (i + off, 0)),
            pl.BlockSpec((1, d), lambda i: (0, 0)),
        ],
        out_specs=pl.BlockSpec((tm, d), lambda i: (i, 0)),
        out_shape=jax.ShapeDtypeStruct((rows, d), F32),
        compiler_params=_params(("parallel",)),
        name="final_rmsnorm",
    )(x, g.reshape(1, d))


def _mm_kernel(*refs, nk, epilogue, n_split, nb_split, res_split):
    x_ref, w_ref = refs[:2]
    if epilogue == "residual":
        res_refs = refs[2:-2]
        gate_ref, o_ref = refs[-2:]
        o_refs = [o_ref]
    else:
        o_refs = refs[2:]

    def finish(acc):
        if epilogue == "relu2":
            r = jnp.maximum(acc, 0.0)
            acc = r * r
        elif epilogue == "residual":
            res = res_refs[0][...]
            if len(res_refs) == 2:
                res = jnp.where(pl.program_id(0) < res_split, res, res_refs[1][...])
            acc = res + gate_ref[...] * acc
        return acc

    def product():
        return _dot(x_ref[...], w_ref[...])

    def product_to(o_ref):
        o_ref[...] = finish(product()).astype(o_ref.dtype)

    if nk == 1 and n_split == 1:
        product_to(o_refs[0])
        return
    if nk == 1:
        for s, o_ref in enumerate(o_refs):
            pl.when(pl.program_id(1) // nb_split == s)(functools.partial(product_to, o_ref))
        return

    o_ref = o_refs[0]
    k = pl.program_id(2)

    @pl.when(k == 0)
    def _():
        o_ref[...] = product()

    @pl.when(jnp.logical_and(k > 0, k < nk - 1))
    def _():
        o_ref[...] += product()

    @pl.when(k == nk - 1)
    def _():
        o_ref[...] = finish(o_ref[...] + product())


def _matmul(x, w, out_dtype, epilogue="plain", *, row_off=0, rows=None, n_split=1,
            res=None, mod=None, layer=None, gate_chunk=None, groups=None):
    kdim = x.shape[1]
    m = x.shape[0] if rows is None else rows
    n = w.shape[1]
    bm = _tile(m, MM_BM)
    n_res = 0 if epilogue != "residual" else (2 if isinstance(res, tuple) else 1)
    out_bytes = jnp.dtype(out_dtype).itemsize * n_split + 4 * n_res

    def fits(bn, bk):
        return 2 * (2 * bk * (bm + bn) + bm * bn * out_bytes) + 4 * bm * bn <= MM_VMEM_BUDGET_BYTES

    bn_full = _tile(n // n_split, MM_BN)
    candidates = [(bn_full, kdim), (max(bn_full // 2, V7X_LANES), kdim)]
    candidates += [(bn_full, kdim // s) for s in (2, 4, 8, 16, 32)]
    bn, bk = next((bn_, bk_) for bn_, bk_ in candidates if kdim % bk_ == 0 and fits(bn_, bk_))
    nk = kdim // bk
    assert nk == 1 or (out_dtype == F32 and n_split == 1)
    assert row_off % bm == 0 and (n // n_split) % bn == 0
    off = row_off // bm
    nb_split = n // n_split // bn
    in_specs = [
        pl.BlockSpec((bm, bk), lambda i, j, k: (i + off, k)),
        pl.BlockSpec((bk, bn), lambda i, j, k: (k, j)),
    ]
    args = [x, w]
    res_split = 0
    if epilogue == "residual":
        assert row_off == 0 and n_split == 1
        nb = n // bn
        if isinstance(res, tuple):
            res_a, res_b = res
            res_split = res_a.shape[0] // bm
            assert res_a.shape[0] % bm == 0 and res_a.shape[0] + res_b.shape[0] == m
            in_specs += [
                pl.BlockSpec((bm, bn), lambda i, j, k: (jnp.minimum(i, res_split - 1),
                                                        jnp.where(i < res_split, j, nb - 1))),
                pl.BlockSpec((bm, bn), lambda i, j, k: (jnp.maximum(i - res_split, 0),
                                                        jnp.where(i < res_split, 0, j))),
            ]
            args += [res_a, res_b]
        else:
            in_specs.append(pl.BlockSpec((bm, bn), lambda i, j, k: (i, j)))
            args.append(res)
        in_specs.append(pl.BlockSpec((None, None, 1, bn),
                                     lambda i, j, k: (layer, groups.of_tile(i, bm), 0, gate_chunk * nb + j)))
        args.append(mod)

    def out_spec(s):
        return pl.BlockSpec((bm, bn), lambda i, j, k: (i, jnp.clip(j - s * nb_split, 0, nb_split - 1)))

    outs = pl.pallas_call(
        functools.partial(_mm_kernel, nk=nk, epilogue=epilogue, n_split=n_split, nb_split=nb_split,
                          res_split=res_split),
        grid=(m // bm, n // bn, nk),
        in_specs=in_specs,
        out_specs=[out_spec(s) for s in range(n_split)],
        out_shape=[jax.ShapeDtypeStruct((m, n // n_split), out_dtype)] * n_split,
        compiler_params=_params(("parallel", "arbitrary", "arbitrary")),
        name="matmul_" + epilogue,
    )(*args)
    return outs[0] if n_split == 1 else outs


class _ScanLayout(NamedTuple):
    n_ctx: int
    per_ctx: int
    per_lat: int
    total: int
    batch: int
    reverse: bool

    def pos(self, g):
        r = self.total - 1 - g if self.reverse else g
        is_lat = r >= self.n_ctx
        rl = jnp.maximum(r - self.n_ctx, 0)
        seq_ctx = jnp.minimum(r // self.per_ctx, self.batch - 1)
        seq_lat = rl // self.per_lat
        pos = jnp.where(is_lat, rl % self.per_lat, r % self.per_ctx)
        per = jnp.where(is_lat, self.per_lat, self.per_ctx)
        start, stop = (per - 1, 0) if self.reverse else (0, per - 1)
        return r, is_lat, seq_ctx, seq_lat, pos == start, pos == stop


def _level_ref(c, half, reverse):
    rows, width = c.shape
    blk = 2 * half
    idx = half if reverse else half - 1
    assert blk % V7X_SUBLANES == 0
    c3 = c.reshape(rows // blk, blk, width)
    ref = jnp.broadcast_to(c3[:, idx:idx + 1, :], c3.shape)
    return ref.reshape(rows, width)


def _exp2_neg_abs(d):
    bits = lax.bitcast_convert_type(d, jnp.uint32) | jnp.uint32(0x80000000)
    return jnp.exp2(lax.bitcast_convert_type(bits, F32))


def _level_weight(c, fg, level, reverse):
    rows = c.shape[0]
    if level > 2:
        return _exp2_neg_abs(c - _level_ref(c, 1 << (level - 1), reverse))
    pos = lax.broadcasted_iota(jnp.int32, c.shape, 0) & ((1 << level) - 1)
    prev = pltpu.roll(fg, 1, 0)
    nxt = pltpu.roll(fg, rows - 1, 0)
    if level == 1:
        return jnp.where(pos == (0 if reverse else 1), fg, 1.0)
    if reverse:
        return jnp.where(pos == 3, prev, jnp.where(pos == 2, 1.0, jnp.where(pos == 1, fg, fg * nxt)))
    return jnp.where(pos == 0, nxt, jnp.where(pos == 1, 1.0, jnp.where(pos == 2, fg, prev * fg)))


def _hg_scan_kernel(*refs, chunk, heads, dk, dv, layer, lay, fuse_out):
    if fuse_out:
        (q_ref, v_ref, f_ref, lbl_ref, s0_ref, ofwd_ref, g_ref, gn_ref,
         o_ref, sfin_ref, st_ref) = refs
    else:
        q_ref, v_ref, f_ref, lbl_ref, s0_ref, o_ref, sfin_ref, st_ref = refs
    reverse = lay.reverse
    _, is_lat, _, _, first, last = lay.pos(pl.program_id(1))
    n_levels = chunk.bit_length() - 1
    low_levels = min(n_levels, V7X_SUBLANES.bit_length() - 1)

    @pl.when(jnp.logical_and(first, is_lat))
    def _():
        for j in range(heads):
            st_ref[j] = s0_ref[0, j].T

    @pl.when(jnp.logical_and(first, jnp.logical_not(is_lat)))
    def _():
        st_ref[...] = jnp.zeros(st_ref.shape, F32)

    lbl = lbl_ref[...]
    e = jnp.exp(lbl - jnp.max(lbl, axis=0, keepdims=True))
    lb = jnp.sum(e[:layer + 1], axis=0, keepdims=True) / jnp.sum(e, axis=0, keepdims=True)

    fg_all = lb + (1.0 - lb) * _sigmoid(f_ref[...])
    k_all = 1.0 - fg_all
    log2f = jnp.log2(fg_all)

    row = lax.broadcasted_iota(jnp.int32, (chunk, chunk), 0)
    col = lax.broadcasted_iota(jnp.int32, (chunk, chunk), 1)
    before = (col > row) if reverse else (col < row)
    tri = jnp.logical_or(before, col == row).astype(BF16)
    hi = log2f.astype(BF16)
    r1 = log2f - hi.astype(F32)
    mid = r1.astype(BF16)
    lo = (r1 - mid.astype(F32)).astype(BF16)
    c_all = _dot(tri, hi) + _dot(tri, mid) + _dot(tri, lo)

    x = row ^ col
    lvl = jnp.zeros((chunk, chunk), jnp.int32)
    for b in range(n_levels):
        lvl = lvl + (x >= (1 << b)).astype(jnp.int32)
    lvl = jnp.where(jnp.logical_or(before, x == 0), lvl, -1)

    end = 0 if reverse else chunk - 1
    if fuse_out:
        gn = gn_ref[...]
    for j in range(heads):
        sl = slice(j * dk, (j + 1) * dk)
        sv = slice(j * dv, (j + 1) * dv)
        q = _silu(q_ref[:, sl])
        k = k_all[:, sl]
        c = c_all[:, sl]
        v = v_ref[:, sv].astype(BF16)
        c_end = c[end:end + 1, :]
        fg = fg_all[:, sl]

        a = jnp.where(lvl == 0, _dot_nt(q.astype(BF16), k.astype(BF16)), 0.0)
        for level in range(1, low_levels + 1):
            w = _level_weight(c, fg, level, reverse)
            a_l = _dot_nt((q * w).astype(BF16), (k * w).astype(BF16))
            a = jnp.where(lvl == level, a_l, a)
        for level in range(low_levels + 1, n_levels + 1):
            half = 1 << (level - 1)
            w = _level_weight(c, fg, level, reverse)
            q_rows, k_rows = [], []
            for b0 in range(0, chunk, 2 * half):
                early, late = (b0 + half, b0) if reverse else (b0, b0 + half)
                q_rows.append(q[late:late + half] * w[late:late + half])
                k_scaled = k[early:early + half] * w[early:early + half]
                k_rows += [k[late:late + half], k_scaled] if reverse else [k_scaled, k[late:late + half]]
            p = _dot_nt(jnp.concatenate(q_rows, axis=0).astype(BF16),
                        jnp.concatenate(k_rows, axis=0).astype(BF16))
            pieces = []
            for b, b0 in enumerate(range(0, chunk, 2 * half)):
                early, late = (b0 + half, b0) if reverse else (b0, b0 + half)
                upd = jnp.where(lvl[late:late + half] == level, p[b * half:(b + 1) * half], a[late:late + half])
                pieces += [upd, a[early:early + half]] if reverse else [a[early:early + half], upd]
            a = jnp.concatenate(pieces, axis=0)

        st = st_ref[j]
        o = _dot(a.astype(BF16), v) + _dot_nt((q * jnp.exp2(c)).astype(BF16), st.astype(BF16))
        if fuse_out:
            o = o + ofwd_ref[:, sv]
            y = o * lax.rsqrt(jnp.mean(o * o, axis=-1, keepdims=True) + EPS) * gn
            o = y * _silu(g_ref[:, sv])
        o_ref[:, sv] = o.astype(o_ref.dtype)
        k_dec = (k * jnp.exp2(c_end - c)).astype(BF16)
        st_ref[j] = st * jnp.exp2(c_end) + _dot_tn(v, k_dec)

    @pl.when(jnp.logical_and(last, jnp.logical_not(is_lat)))
    def _():
        for j in range(heads):
            sfin_ref[0, j] = st_ref[j].T


def _hg_scan(proj, lb_logits_dir, s0_lat, *, groups, batch, seq, n_heads, dk, dv, d, f_seg, reverse, layer,
             o_fwd=None, g_norm=None, g_seg=None):
    chunk = _tile(math.gcd(seq, groups.dec_seq), HG_CHUNK)
    heads = _tile(n_heads, HG_HEADS_PER_STEP)
    assert dk == dv and d == n_heads * dk
    w = heads * dk
    segs = d // w
    lay = _ScanLayout(n_ctx=groups.ctx_rows // chunk, per_ctx=seq // chunk, per_lat=groups.dec_seq // chunk,
                      total=groups.rows // chunk, batch=batch, reverse=reverse)
    fuse_out = o_fwd is not None

    def seg_spec(seg):
        return pl.BlockSpec((chunk, w), lambda h, g: (lay.pos(g)[0], seg * segs + h))

    n_lb = lb_logits_dir.shape[0]
    in_specs = [seg_spec(0), seg_spec(1), seg_spec(f_seg),
                pl.BlockSpec((n_lb, w), lambda h, g: (0, h)),
                pl.BlockSpec((1, heads, dk, dv), lambda h, g: (lay.pos(g)[3], h, 0, 0))]
    args = [proj, proj, proj, lb_logits_dir, s0_lat]
    if fuse_out:
        in_specs += [seg_spec(0), seg_spec(g_seg), pl.BlockSpec((1, dv), lambda h, g: (0, 0))]
        args += [o_fwd, proj, g_norm.reshape(1, dv)]
    o, s_fin = pl.pallas_call(
        functools.partial(_hg_scan_kernel, chunk=chunk, heads=heads, dk=dk, dv=dv,
                          layer=layer, lay=lay, fuse_out=fuse_out),
        grid=(n_heads // heads, lay.total),
        in_specs=in_specs,
        out_specs=[
            pl.BlockSpec((chunk, w), lambda h, g: (lay.pos(g)[0], h)),
            pl.BlockSpec((1, heads, dk, dv), lambda h, g: (lay.pos(g)[2], h, 0, 0)),
        ],
        out_shape=[
            jax.ShapeDtypeStruct((groups.rows, d), BF16 if fuse_out else F32),
            jax.ShapeDtypeStruct((batch, n_heads, dk, dv), F32),
        ],
        scratch_shapes=[pltpu.VMEM((heads, dv, dk), F32)],
        compiler_params=_params(("parallel", "arbitrary")),
        name="hgrn2_scan_" + ("bwd" if reverse else "fwd"),
    )(*args)
    return o, s_fin


def _rope_tables(seq, hd):
    half = hd // 2
    rows = seq // GRID_W
    r_pos = jnp.repeat(jnp.arange(rows), GRID_W).astype(F32)
    c_pos = jnp.tile(jnp.arange(GRID_W), rows).astype(F32)
    inv_freq = ROPE_BASE ** (-jnp.arange(0, half, 2, dtype=F32) / half)
    ang_r = r_pos[:, None] * inv_freq
    ang_c = c_pos[:, None] * inv_freq
    cos = jnp.concatenate([jnp.cos(ang_r)] * 2 + [jnp.cos(ang_c)] * 2, axis=-1)
    sin = jnp.concatenate([-jnp.sin(ang_r), jnp.sin(ang_r), -jnp.sin(ang_c), jnp.sin(ang_c)], axis=-1)
    return cos, sin


def _rope_tile(x, cos, sin):
    hd = x.shape[1]
    quarter = hd // 4
    lane = lax.broadcasted_iota(jnp.int32, x.shape, 1)
    first = (lane % (2 * quarter)) < quarter
    partner = jnp.where(first, pltpu.roll(x, hd - quarter, 1), pltpu.roll(x, quarter, 1))
    return x * cos + partner * sin


def _attn_kernel(*refs, hd, heads, tq, past, scale, lam_init, latent):
    if latent:
        (lam_ref, q_ref, k_ref, v_ref, kc_ref, vc_ref, cos_ref, sin_ref, g_ref, _,
         o_ref, kall_ref, vall_ref) = refs
    else:
        lam_ref, q_ref, k_ref, v_ref, g_ref, o_ref = refs
    w = 2 * hd
    lm = lam_ref[...]
    lam = (jnp.exp(jnp.sum(lm[0:1] * lm[1:2], axis=-1, keepdims=True))
           - jnp.exp(jnp.sum(lm[2:3] * lm[3:4], axis=-1, keepdims=True)) + lam_init)
    g = g_ref[...]

    if latent:
        i = pl.program_id(2)

        @pl.when(i == 0)
        def _():
            kall_ref[0:past, :] = kc_ref[...].astype(BF16)
            vall_ref[0:past, :] = vc_ref[...].astype(BF16)
            vall_ref[past:, :] = v_ref[...].astype(BF16)
            for half in range(2):
                hs = slice(half * hd, (half + 1) * hd)
                kall_ref[past:, hs] = _rope_tile(k_ref[:, hs], cos_ref[...], sin_ref[...]).astype(BF16)

        rows = pl.ds(pl.multiple_of(i * tq, tq), tq)
        cos_q = cos_ref[rows, :]
        sin_q = sin_ref[rows, :]

    def softmax_parts(qh, kh):
        s = _dot_nt(qh, kh)
        p = jnp.exp2(s - jnp.max(s, axis=-1, keepdims=True))
        return p.astype(BF16), 1.0 / jnp.sum(p, axis=-1, keepdims=True)

    for j in range(heads):
        parts = []
        for half in range(2):
            hs = slice(j * w + half * hd, j * w + (half + 1) * hd)
            qh = q_ref[:, hs]
            if latent:
                qh = _rope_tile(qh, cos_q, sin_q)
                kh = kall_ref[:, hs]
            else:
                kh = k_ref[:, hs].astype(BF16)
            parts.append(softmax_parts((qh * (scale * LOG2E)).astype(BF16), kh))
        (p1, r1), (p2, r2) = parts
        js = slice(j * w, (j + 1) * w)
        vj = vall_ref[...] if latent else v_ref[:, js].astype(BF16)
        o = r1 * _dot(p1, vj) - (lam * r2) * _dot(p2, vj)
        y = o * lax.rsqrt(jnp.mean(o * o, axis=-1, keepdims=True) + EPS) * g
        o_ref[:, js] = (y * (1.0 - lam_init)).astype(o_ref.dtype)


def _diff_attention(q, k, v, lam_params, subln_g, *, batch, seq, n_heads, hd, lam_init, out_rows, out_row_off,
                    cache_k=None, cache_v=None, prev=None):
    latent = cache_k is not None
    w = 2 * hd
    d = n_heads * w
    tq = _tile(seq, ATTN_TQ)
    nq = seq // tq
    heads = 1 if latent else _tile(n_heads, ATTN_CTX_HEADS_PER_STEP)
    assert out_row_off % tq == 0
    out_off = out_row_off // tq
    scale = hd ** -0.5
    past = cache_k.shape[0] // batch if latent else 0

    def kv_spec(rows):
        return pl.BlockSpec((rows, heads * w), lambda b, h, i: (b, h))

    in_specs = [pl.BlockSpec(lam_params.shape, lambda b, h, i: (0, 0)),
                pl.BlockSpec((tq, heads * w), lambda b, h, i: (b * nq + i, h)),
                kv_spec(seq), kv_spec(seq)]
    args = [lam_params, q, k, v]
    scratch = []
    if latent:
        cos, sin = _rope_tables(seq, hd)
        table = pl.BlockSpec((seq, hd), lambda b, h, i: (0, 0))
        in_specs += [kv_spec(past), kv_spec(past), table, table]
        args += [cache_k, cache_v, cos, sin]
        scratch = [pltpu.VMEM((past + seq, w), BF16), pltpu.VMEM((past + seq, w), BF16)]
    in_specs.append(pl.BlockSpec((1, w), lambda b, h, i: (0, 0)))
    args.append(subln_g.reshape(1, w))
    aliases = {}
    if prev is not None:
        assert latent
        in_specs.append(pl.BlockSpec(memory_space=pl.ANY))
        args.append(prev)
        aliases = {len(args) - 1: 0}
    return pl.pallas_call(
        functools.partial(_attn_kernel, hd=hd, heads=heads, tq=tq, past=past, scale=scale,
                          lam_init=lam_init, latent=latent),
        grid=(batch, n_heads // heads, nq),
        in_specs=in_specs,
        out_specs=pl.BlockSpec((tq, heads * w), lambda b, h, i: (out_off + b * nq + i, h)),
        out_shape=jax.ShapeDtypeStruct((out_rows, d), BF16),
        scratch_shapes=scratch,
        input_output_aliases=aliases,
        compiler_params=_params(("parallel", "parallel", "arbitrary")),
        name="diff_attention_" + ("latent" if latent else "context"),
    )(*args)


def kernel(x_prompt, x_sample, state_hgrn_fwd, state_hgrn_bwd, cache_attn_k, cache_attn_v, c, c_ctx,
           w_mod, b_mod, norm_g, w_ff1, w_ff2, hg_w_in, hg_w_out, hg_g_norm, hg_lb_logits,
           da_w_in, da_w_out, da_subln_g, da_lambda, final_norm_g):
    batch, seq, d = x_prompt.shape
    dec_batch, dec_seq, _ = x_sample.shape
    depth = w_mod.shape[0]
    _, _, hg_heads, hg_dk, hg_dv = state_hgrn_fwd.shape
    _, _, past_len, da_halves, da_hd = cache_attn_k.shape
    da_heads = da_halves // 2
    ctx_rows = batch * seq
    lat_rows = dec_batch * dec_seq
    groups = _Groups(ctx_rows, dec_batch, dec_seq)

    x = (x_prompt.reshape(ctx_rows, d), x_sample.reshape(lat_rows, d))

    n_groups = 1 + dec_batch
    g8 = -(-n_groups // V7X_SUBLANES) * V7X_SUBLANES
    cond = jnp.concatenate([c_ctx[None, :], c, jnp.zeros((g8 - n_groups, d), F32)], axis=0)
    mod = _modulation(cond, w_mod, b_mod).reshape(depth, g8, 1, 6 * d)
    SH1, SC1, GT1, SH2, SC2, GT2 = range(6)

    new_sf, new_sb, new_k, new_v = [], [], [], []
    for layer in range(depth):
        h = _norm_mod(x, norm_g[layer, 0], mod, layer, SC1, SH1, groups)
        j = layer // N_MIXERS
        if layer % N_MIXERS == 0:
            proj = _matmul(h, hg_w_in[j].astype(BF16), F32)
            scan = dict(groups=groups, batch=batch, seq=seq, n_heads=hg_heads, dk=hg_dk, dv=hg_dv, d=d,
                        layer=layer)
            o_fwd, s_fwd = _hg_scan(proj, hg_lb_logits[0], state_hgrn_fwd[:, j], f_seg=2, reverse=False, **scan)
            mixed, s_bwd = _hg_scan(proj, hg_lb_logits[1], state_hgrn_bwd[:, j], f_seg=3, reverse=True,
                                    o_fwd=o_fwd, g_norm=hg_g_norm[j], g_seg=4, **scan)
            new_sf.append(s_fwd)
            new_sb.append(s_bwd)
            w_out = hg_w_out[j]
        else:
            lam_init = 0.8 - 0.6 * math.exp(-0.3 * layer)
            w_in = da_w_in[j].astype(BF16)
            q_ctx, k_ctx, v_ctx = _matmul(h, w_in, F32, row_off=0, rows=ctx_rows, n_split=3)
            q_lat, k_lat, v_lat = _matmul(h, w_in, F32, row_off=ctx_rows, rows=lat_rows, n_split=3)
            new_k.append(k_ctx.reshape(batch, seq, da_halves, da_hd))
            new_v.append(v_ctx.reshape(batch, seq, da_heads, 2 * da_hd))
            att = dict(n_heads=da_heads, hd=da_hd, lam_init=lam_init, out_rows=groups.rows)
            mixed = _diff_attention(q_ctx, k_ctx, v_ctx, da_lambda[j], da_subln_g[j], batch=batch, seq=seq,
                                    out_row_off=0, **att)
            mixed = _diff_attention(q_lat, k_lat, v_lat, da_lambda[j], da_subln_g[j], batch=dec_batch,
                                    seq=dec_seq, out_row_off=ctx_rows,
                                    cache_k=cache_attn_k[:, j].reshape(dec_batch * past_len, d),
                                    cache_v=cache_attn_v[:, j].reshape(dec_batch * past_len, d),
                                    prev=mixed, **att)
            w_out = da_w_out[j]
        x = _matmul(mixed, w_out.astype(BF16), F32, "residual", res=x, mod=mod, layer=layer,
                    gate_chunk=GT1, groups=groups)
        h = _norm_mod(x, norm_g[layer, 1], mod, layer, SC2, SH2, groups)
        u = _matmul(h, w_ff1[layer].astype(BF16), BF16, "relu2")
        x = _matmul(u, w_ff2[layer].astype(BF16), F32, "residual", res=x, mod=mod, layer=layer,
                    gate_chunk=GT2, groups=groups)

    y_prompt = _final_norm(x, final_norm_g, 0, ctx_rows).reshape(batch, seq, d)
    y_sample = _final_norm(x, final_norm_g, ctx_rows, lat_rows).reshape(dec_batch, dec_seq, d)
    return (y_prompt, y_sample, jnp.stack(new_sf, axis=1), jnp.stack(new_sb, axis=1),
            jnp.stack(new_k, axis=1), jnp.stack(new_v, axis=1))
```

```python
import functools
import math
from typing import NamedTuple

import jax
import jax.numpy as jnp
from jax import lax
from jax.experimental import pallas as pl
from jax.experimental.pallas import tpu as pltpu

F32 = jnp.float32
BF16 = jnp.bfloat16

GRID_W = 64
EPS = 1e-6
ROPE_BASE = 10000.0
N_MIXERS = 2

V7X_SUBLANES = 8
V7X_LANES = 128
V7X_VMEM_BYTES = 64 * 1024 * 1024
V7X_VMEM_LIMIT_BYTES = V7X_VMEM_BYTES - 4 * 1024 * 1024

MM_BM = 1024
MM_BN = 1024
MM_VMEM_BUDGET_BYTES = 44 * 1024 * 1024
CAST_VMEM_BUDGET_BYTES = 10 * 1024 * 1024
CAST_TILE_COLS = 1024
ROW_TILE = 256
MOD_BN = 512
HG_CHUNK = 128
HG_HEADS_PER_STEP = 16
ATTN_TQ = 256
ATTN_CTX_HEADS_PER_STEP = 4

LOG2E = 1.4426950408889634


def _tile(full, want):
    t = min(full, want)
    assert full % t == 0, (full, want)
    return t


def _params(sem):
    return pltpu.CompilerParams(dimension_semantics=sem, vmem_limit_bytes=V7X_VMEM_LIMIT_BYTES)


def _sigmoid(x):
    return 0.5 * jnp.tanh(0.5 * x) + 0.5


def _silu(x):
    return x * _sigmoid(x)


def _dot(a, b):
    return jnp.dot(a, b, preferred_element_type=F32)


def _dot_nt(a, b):
    return lax.dot_general(a, b, (((1,), (1,)), ((), ())), preferred_element_type=F32)


def _dot_tn(a, b):
    return lax.dot_general(a, b, (((0,), (0,)), ((), ())), preferred_element_type=F32)


class _Groups:
    def __init__(self, ctx_rows, dec_batch, dec_seq):
        self.ctx_rows = ctx_rows
        self.dec_batch = dec_batch
        self.dec_seq = dec_seq
        self.rows = ctx_rows + dec_batch * dec_seq

    def of_tile(self, i, tile):
        assert self.ctx_rows % tile == 0 and self.dec_seq % tile == 0
        g = jnp.int32(0)
        for b in range(self.dec_batch):
            g = g + (i >= (self.ctx_rows + b * self.dec_seq) // tile).astype(jnp.int32)
        return g


def _mod_kernel(c_ref, w_ref, b_ref, o_ref):
    a = _silu(c_ref[...]).astype(BF16)
    o_ref[...] = _dot(a, w_ref[...].astype(BF16)) + b_ref[...]


def _modulation(cond, w_mod, b_mod):
    g8, d = cond.shape
    depth, _, n = w_mod.shape
    bn = _tile(n, MOD_BN)
    return pl.pallas_call(
        _mod_kernel,
        grid=(depth, n // bn),
        in_specs=[
            pl.BlockSpec((g8, d), lambda l, j: (0, 0)),
            pl.BlockSpec((None, d, bn), lambda l, j: (l, 0, j)),
            pl.BlockSpec((None, 1, bn), lambda l, j: (l, 0, j)),
        ],
        out_specs=pl.BlockSpec((None, g8, bn), lambda l, j: (l, 0, j)),
        out_shape=jax.ShapeDtypeStruct((depth, g8, n), F32),
        compiler_params=_params(("arbitrary", "arbitrary")),
        name="adaln_modulation",
    )(cond, w_mod, b_mod.reshape(depth, 1, n))


def _norm_kernel(*refs, modulated, x_split):
    if modulated:
        *x_refs, g_ref, sc_ref, sh_ref, o_ref = refs
    else:
        *x_refs, g_ref, o_ref = refs
    x = x_refs[0][...]
    if len(x_refs) == 2:
        x = jnp.where(pl.program_id(0) < x_split, x, x_refs[1][...])
    y = x * lax.rsqrt(jnp.mean(x * x, axis=-1, keepdims=True) + EPS) * g_ref[...]
    if modulated:
        y = y * (1.0 + sc_ref[...]) + sh_ref[...]
    o_ref[...] = y.astype(o_ref.dtype)


def _norm_mod(x, g, mod, layer, sc_chunk, sh_chunk, groups):
    rows, d = groups.rows, g.shape[0]
    tm = _tile(math.gcd(groups.ctx_rows, groups.dec_seq), ROW_TILE)

    def vec(chunk):
        return pl.BlockSpec((None, None, 1, d), lambda i: (layer, groups.of_tile(i, tm), 0, chunk))

    if isinstance(x, tuple):
        x_split = x[0].shape[0] // tm
        x_specs = [pl.BlockSpec((tm, d), lambda i: (jnp.minimum(i, x_split - 1), 0)),
                   pl.BlockSpec((tm, d), lambda i: (jnp.maximum(i - x_split, 0), 0))]
        xs = list(x)
    else:
        x_split = 0
        x_specs = [pl.BlockSpec((tm, d), lambda i: (i, 0))]
        xs = [x]
    return pl.pallas_call(
        functools.partial(_norm_kernel, modulated=True, x_split=x_split),
        grid=(rows // tm,),
        in_specs=x_specs + [
            pl.BlockSpec((1, d), lambda i: (0, 0)),
            vec(sc_chunk),
            vec(sh_chunk),
        ],
        out_specs=pl.BlockSpec((tm, d), lambda i: (i, 0)),
        out_shape=jax.ShapeDtypeStruct((rows, d), BF16),
        compiler_params=_params(("parallel",)),
        name="rmsnorm_modulate",
    )(*xs, g.reshape(1, d), mod, mod)


def _final_norm(x, g, row_off, rows):
    d = x.shape[1]
    tm = _tile(math.gcd(rows, row_off) if row_off else rows, ROW_TILE)
    off = row_off // tm
    return pl.pallas_call(
        functools.partial(_norm_kernel, modulated=False, x_split=0),
        grid=(rows // tm,),
        in_specs=[
            pl.BlockSpec((tm, d), lambda i: (i + off, 0)),
            pl.BlockSpec((1, d), lambda i: (0, 0)),
        ],
        out_specs=pl.BlockSpec((tm, d), lambda i: (i, 0)),
        out_shape=jax.ShapeDtypeStruct((rows, d), F32),
        compiler_params=_params(("parallel",)),
        name="final_rmsnorm",
    )(x, g.reshape(1, d))


def _mm_kernel(*refs, nk, epilogue, n_res, res_split, n_casts):
    x_ref, w_ref = refs[:2]
    n_in = 2 + (n_res + 1 if epilogue == "residual" else 0)
    res_refs = refs[2:2 + n_res]
    gate_ref = refs[n_in - 1]
    o_ref = refs[n_in + n_casts]
    for src_ref, dst_ref in zip(refs[n_in:n_in + n_casts], refs[n_in + n_casts + 1:]):
        dst_ref[...] = src_ref[...].astype(dst_ref.dtype)

    def finish(acc):
        if epilogue == "relu2":
            r = jnp.maximum(acc, 0.0)
            acc = r * r
        elif epilogue == "residual":
            res = res_refs[0][...]
            if len(res_refs) == 2:
                res = jnp.where(pl.program_id(0) < res_split, res, res_refs[1][...])
            acc = res + gate_ref[...] * acc
        return acc

    def product():
        return _dot(x_ref[...], w_ref[...])

    if nk == 1:
        o_ref[...] = finish(product()).astype(o_ref.dtype)
        return

    k = pl.program_id(2)

    @pl.when(k == 0)
    def _():
        o_ref[...] = product()

    @pl.when(jnp.logical_and(k > 0, k < nk - 1))
    def _():
        o_ref[...] += product()

    @pl.when(k == nk - 1)
    def _():
        o_ref[...] = finish(o_ref[...] + product())


class _Cast(NamedTuple):
    src: jax.Array
    lead: int


def _cast_kernel(s_ref, o_ref):
    o_ref[...] = s_ref[...].astype(o_ref.dtype)


def _cast_weight(c):
    _, rows, cols = c.src.shape
    tr, tc = _tile(rows, CAST_TILE_COLS), _tile(cols, 2 * CAST_TILE_COLS)
    return pl.pallas_call(
        _cast_kernel,
        grid=(rows // tr, cols // tc),
        in_specs=[pl.BlockSpec((None, tr, tc), lambda i, j: (c.lead, i, j))],
        out_specs=pl.BlockSpec((tr, tc), lambda i, j: (i, j)),
        out_shape=jax.ShapeDtypeStruct((rows, cols), BF16),
        compiler_params=_params(("parallel", "parallel")),
        name="cast_weight",
    )(c.src)


def _matmul(x, w, out_dtype, epilogue="plain", *, res=None, mod=None, layer=None, gate_chunk=None,
            groups=None, casts=()):
    m, kdim = x.shape
    n = w.shape[1]
    bm = _tile(m, MM_BM)
    n_res = 0 if epilogue != "residual" else (2 if isinstance(res, tuple) else 1)
    out_bytes = jnp.dtype(out_dtype).itemsize + 4 * n_res

    def fits(bn, bk):
        return 2 * (2 * bk * (bm + bn) + bm * bn * out_bytes) + 4 * bm * bn <= MM_VMEM_BUDGET_BYTES

    bn_full = _tile(n, MM_BN)
    candidates = [(bn_full, kdim), (max(bn_full // 2, V7X_LANES), kdim)]
    candidates += [(bn_full, kdim // s) for s in (2, 4, 8, 16, 32)]
    bn, bk = next((bn_, bk_) for bn_, bk_ in candidates if kdim % bk_ == 0 and fits(bn_, bk_))
    nk = kdim // bk
    nj = n // bn
    assert nk == 1 or out_dtype == F32
    in_specs = [
        pl.BlockSpec((bm, bk), lambda i, j, k: (i, k)),
        pl.BlockSpec((bk, bn), lambda i, j, k: (k, j)),
    ]
    args = [x, w]
    res_split = 0
    if epilogue == "residual":
        nb = n // bn
        if isinstance(res, tuple):
            res_a, res_b = res
            res_split = res_a.shape[0] // bm
            assert res_a.shape[0] % bm == 0 and res_a.shape[0] + res_b.shape[0] == m
            in_specs += [
                pl.BlockSpec((bm, bn), lambda i, j, k: (jnp.minimum(i, res_split - 1),
                                                        jnp.where(i < res_split, j, nb - 1))),
                pl.BlockSpec((bm, bn), lambda i, j, k: (jnp.maximum(i - res_split, 0),
                                                        jnp.where(i < res_split, 0, j))),
            ]
            args += [res_a, res_b]
        else:
            in_specs.append(pl.BlockSpec((bm, bn), lambda i, j, k: (i, j)))
            args.append(res)
        in_specs.append(pl.BlockSpec((None, None, 1, bn),
                                     lambda i, j, k: (layer, groups.of_tile(i, bm), 0, gate_chunk * nb + j)))
        args.append(mod)

    out_specs = [pl.BlockSpec((bm, bn), lambda i, j, k: (i, j))]
    out_shape = [jax.ShapeDtypeStruct((m, n), out_dtype)]
    n_steps = (m // bm) * nj * nk
    cast_bytes = 0
    for c in casts:
        _, rows, cols = c.src.shape
        assert cols % CAST_TILE_COLS == 0
        tiles_c = cols // CAST_TILE_COLS
        tile_rows = next(tr for tr in (128, 256, 512, 1024, 2048)
                         if rows % tr == 0 and (rows // tr) * tiles_c <= n_steps)
        n_tiles = (rows // tile_rows) * tiles_c
        cast_bytes += 2 * (4 + 2) * tile_rows * CAST_TILE_COLS
        assert cast_bytes <= CAST_VMEM_BUDGET_BYTES

        def tile_of(i, j, k, n_tiles=n_tiles, tiles_c=tiles_c):
            t = jnp.minimum((i * nj + j) * nk + k, n_tiles - 1)
            return t // tiles_c, t % tiles_c

        in_specs.append(pl.BlockSpec((None, tile_rows, CAST_TILE_COLS),
                                     lambda i, j, k, c=c, tile_of=tile_of: (c.lead, *tile_of(i, j, k))))
        args.append(c.src)
        out_specs.append(pl.BlockSpec((tile_rows, CAST_TILE_COLS), tile_of))
        out_shape.append(jax.ShapeDtypeStruct((rows, cols), BF16))

    outs = pl.pallas_call(
        functools.partial(_mm_kernel, nk=nk, epilogue=epilogue, n_res=n_res, res_split=res_split,
                          n_casts=len(casts)),
        grid=(m // bm, nj, nk),
        in_specs=in_specs,
        out_specs=out_specs,
        out_shape=out_shape,
        compiler_params=_params(("arbitrary", "arbitrary", "arbitrary")),
        name="matmul_" + epilogue,
    )(*args)
    return outs[0] if not casts else outs


class _ScanLayout(NamedTuple):
    n_ctx: int
    per_ctx: int
    per_lat: int
    total: int
    batch: int
    reverse: bool

    def pos(self, g):
        r = self.total - 1 - g if self.reverse else g
        is_lat = r >= self.n_ctx
        rl = jnp.maximum(r - self.n_ctx, 0)
        seq_ctx = jnp.minimum(r // self.per_ctx, self.batch - 1)
        seq_lat = rl // self.per_lat
        pos = jnp.where(is_lat, rl % self.per_lat, r % self.per_ctx)
        per = jnp.where(is_lat, self.per_lat, self.per_ctx)
        start, stop = (per - 1, 0) if self.reverse else (0, per - 1)
        return r, is_lat, seq_ctx, seq_lat, pos == start, pos == stop


def _level_ref(c, half, reverse):
    rows, width = c.shape
    blk = 2 * half
    idx = half if reverse else half - 1
    assert blk % V7X_SUBLANES == 0
    c3 = c.reshape(rows // blk, blk, width)
    ref = jnp.broadcast_to(c3[:, idx:idx + 1, :], c3.shape)
    return ref.reshape(rows, width)


def _exp2_neg_abs(d):
    bits = lax.bitcast_convert_type(d, jnp.uint32) | jnp.uint32(0x80000000)
    return jnp.exp2(lax.bitcast_convert_type(bits, F32))


def _level_weight(c, fg, level, reverse):
    rows = c.shape[0]
    if level > 2:
        return _exp2_neg_abs(c - _level_ref(c, 1 << (level - 1), reverse))
    pos = lax.broadcasted_iota(jnp.int32, c.shape, 0) & ((1 << level) - 1)
    prev = pltpu.roll(fg, 1, 0)
    nxt = pltpu.roll(fg, rows - 1, 0)
    if level == 1:
        return jnp.where(pos == (0 if reverse else 1), fg, 1.0)
    if reverse:
        return jnp.where(pos == 3, prev, jnp.where(pos == 2, 1.0, jnp.where(pos == 1, fg, fg * nxt)))
    return jnp.where(pos == 0, nxt, jnp.where(pos == 1, 1.0, jnp.where(pos == 2, fg, prev * fg)))


def _hg_scan_kernel(*refs, chunk, heads, dk, dv, layer, lay, fuse_out):
    if fuse_out:
        (q_ref, v_ref, f_ref, lbl_ref, s0_ref, ofwd_ref, g_ref, gn_ref,
         o_ref, sfin_ref, st_ref) = refs
    else:
        q_ref, v_ref, f_ref, lbl_ref, s0_ref, o_ref, sfin_ref, st_ref = refs
    reverse = lay.reverse
    _, is_lat, _, _, first, last = lay.pos(pl.program_id(1))
    n_levels = chunk.bit_length() - 1
    low_levels = min(n_levels, V7X_SUBLANES.bit_length() - 1)

    @pl.when(jnp.logical_and(first, is_lat))
    def _():
        for j in range(heads):
            st_ref[j] = s0_ref[0, j].T

    @pl.when(jnp.logical_and(first, jnp.logical_not(is_lat)))
    def _():
        st_ref[...] = jnp.zeros(st_ref.shape, F32)

    lbl = lbl_ref[...]
    e = jnp.exp(lbl - jnp.max(lbl, axis=0, keepdims=True))
    lb = jnp.sum(e[:layer + 1], axis=0, keepdims=True) / jnp.sum(e, axis=0, keepdims=True)

    fg_all = lb + (1.0 - lb) * _sigmoid(f_ref[...])
    k_all = 1.0 - fg_all
    log2f = jnp.log2(fg_all)

    row = lax.broadcasted_iota(jnp.int32, (chunk, chunk), 0)
    col = lax.broadcasted_iota(jnp.int32, (chunk, chunk), 1)
    before = (col > row) if reverse else (col < row)
    tri = jnp.logical_or(before, col == row).astype(BF16)
    hi = log2f.astype(BF16)
    r1 = log2f - hi.astype(F32)
    mid = r1.astype(BF16)
    lo = (r1 - mid.astype(F32)).astype(BF16)
    c_all = _dot(tri, hi) + _dot(tri, mid) + _dot(tri, lo)

    x = row ^ col
    lvl = jnp.zeros((chunk, chunk), jnp.int32)
    for b in range(n_levels):
        lvl = lvl + (x >= (1 << b)).astype(jnp.int32)
    lvl = jnp.where(jnp.logical_or(before, x == 0), lvl, -1)

    end = 0 if reverse else chunk - 1
    if fuse_out:
        gn = gn_ref[...]
    for j in range(heads):
        sl = slice(j * dk, (j + 1) * dk)
        sv = slice(j * dv, (j + 1) * dv)
        q = _silu(q_ref[:, sl])
        k = k_all[:, sl]
        c = c_all[:, sl]
        v = v_ref[:, sv].astype(BF16)
        c_end = c[end:end + 1, :]
        fg = fg_all[:, sl]

        a = jnp.where(lvl == 0, _dot_nt(q.astype(BF16), k.astype(BF16)), 0.0)
        for level in range(1, low_levels + 1):
            w = _level_weight(c, fg, level, reverse)
            a_l = _dot_nt((q * w).astype(BF16), (k * w).astype(BF16))
            a = jnp.where(lvl == level, a_l, a)
        for level in range(low_levels + 1, n_levels + 1):
            half = 1 << (level - 1)
            w = _level_weight(c, fg, level, reverse)
            q_rows, k_rows = [], []
            for b0 in range(0, chunk, 2 * half):
                early, late = (b0 + half, b0) if reverse else (b0, b0 + half)
                q_rows.append(q[late:late + half] * w[late:late + half])
                k_scaled = k[early:early + half] * w[early:early + half]
                k_rows += [k[late:late + half], k_scaled] if reverse else [k_scaled, k[late:late + half]]
            p = _dot_nt(jnp.concatenate(q_rows, axis=0).astype(BF16),
                        jnp.concatenate(k_rows, axis=0).astype(BF16))
            pieces = []
            for b, b0 in enumerate(range(0, chunk, 2 * half)):
                early, late = (b0 + half, b0) if reverse else (b0, b0 + half)
                upd = jnp.where(lvl[late:late + half] == level, p[b * half:(b + 1) * half], a[late:late + half])
                pieces += [upd, a[early:early + half]] if reverse else [a[early:early + half], upd]
            a = jnp.concatenate(pieces, axis=0)

        st = st_ref[j]
        o = _dot(a.astype(BF16), v) + _dot_nt((q * jnp.exp2(c)).astype(BF16), st.astype(BF16))
        if fuse_out:
            o = o + ofwd_ref[:, sv]
            y = o * lax.rsqrt(jnp.mean(o * o, axis=-1, keepdims=True) + EPS) * gn
            o = y * _silu(g_ref[:, sv])
        o_ref[:, sv] = o.astype(o_ref.dtype)
        k_dec = (k * jnp.exp2(c_end - c)).astype(BF16)
        st_ref[j] = st * jnp.exp2(c_end) + _dot_tn(v, k_dec)

    @pl.when(jnp.logical_and(last, jnp.logical_not(is_lat)))
    def _():
        for j in range(heads):
            sfin_ref[0, j] = st_ref[j].T


def _hg_scan(proj, lb_logits_dir, s0_lat, *, groups, batch, seq, n_heads, dk, dv, d, f_seg, reverse, layer,
             o_fwd=None, g_norm=None, g_seg=None):
    chunk = _tile(math.gcd(seq, groups.dec_seq), HG_CHUNK)
    heads = _tile(n_heads, HG_HEADS_PER_STEP)
    assert dk == dv and d == n_heads * dk
    w = heads * dk
    segs = d // w
    lay = _ScanLayout(n_ctx=groups.ctx_rows // chunk, per_ctx=seq // chunk, per_lat=groups.dec_seq // chunk,
                      total=groups.rows // chunk, batch=batch, reverse=reverse)
    fuse_out = o_fwd is not None

    def seg_spec(seg):
        return pl.BlockSpec((chunk, w), lambda h, g: (lay.pos(g)[0], seg * segs + h))

    n_lb = lb_logits_dir.shape[0]
    in_specs = [seg_spec(0), seg_spec(1), seg_spec(f_seg),
                pl.BlockSpec((n_lb, w), lambda h, g: (0, h)),
                pl.BlockSpec((1, heads, dk, dv), lambda h, g: (lay.pos(g)[3], h, 0, 0))]
    args = [proj, proj, proj, lb_logits_dir, s0_lat]
    if fuse_out:
        in_specs += [seg_spec(0), seg_spec(g_seg), pl.BlockSpec((1, dv), lambda h, g: (0, 0))]
        args += [o_fwd, proj, g_norm.reshape(1, dv)]
    o, s_fin = pl.pallas_call(
        functools.partial(_hg_scan_kernel, chunk=chunk, heads=heads, dk=dk, dv=dv,
                          layer=layer, lay=lay, fuse_out=fuse_out),
        grid=(n_heads // heads, lay.total),
        in_specs=in_specs,
        out_specs=[
            pl.BlockSpec((chunk, w), lambda h, g: (lay.pos(g)[0], h)),
            pl.BlockSpec((1, heads, dk, dv), lambda h, g: (lay.pos(g)[2], h, 0, 0)),
        ],
        out_shape=[
            jax.ShapeDtypeStruct((groups.rows, d), BF16 if fuse_out else F32),
            jax.ShapeDtypeStruct((batch, n_heads, dk, dv), F32),
        ],
        scratch_shapes=[pltpu.VMEM((heads, dv, dk), F32)],
        compiler_params=_params(("parallel", "arbitrary")),
        name="hgrn2_scan_" + ("bwd" if reverse else "fwd"),
    )(*args)
    return o, s_fin


def _rope_tables(seq, hd):
    half = hd // 2
    rows = seq // GRID_W
    r_pos = jnp.repeat(jnp.arange(rows), GRID_W).astype(F32)
    c_pos = jnp.tile(jnp.arange(GRID_W), rows).astype(F32)
    inv_freq = ROPE_BASE ** (-jnp.arange(0, half, 2, dtype=F32) / half)
    ang_r = r_pos[:, None] * inv_freq
    ang_c = c_pos[:, None] * inv_freq
    cos = jnp.concatenate([jnp.cos(ang_r)] * 2 + [jnp.cos(ang_c)] * 2, axis=-1)
    sin = jnp.concatenate([-jnp.sin(ang_r), jnp.sin(ang_r), -jnp.sin(ang_c), jnp.sin(ang_c)], axis=-1)
    return cos, sin


def _rope_tile(x, cos, sin):
    hd = x.shape[1]
    quarter = hd // 4
    lane = lax.broadcasted_iota(jnp.int32, x.shape, 1)
    first = (lane % (2 * quarter)) < quarter
    partner = jnp.where(first, pltpu.roll(x, hd - quarter, 1), pltpu.roll(x, quarter, 1))
    return x * cos + partner * sin


def _attn_kernel(*refs, hd, heads, tq, past, scale, lam_init, latent):
    if latent:
        (lam_ref, q_ref, k_ref, v_ref, kc_ref, vc_ref, cos_ref, sin_ref, g_ref, _,
         o_ref, kall_ref, vall_ref) = refs
    else:
        lam_ref, q_ref, k_ref, v_ref, g_ref, o_ref, knew_ref, vnew_ref = refs
        knew_ref[...] = k_ref[...]
        vnew_ref[...] = v_ref[...]
    w = 2 * hd
    lm = lam_ref[...]
    lam = (jnp.exp(jnp.sum(lm[0:1] * lm[1:2], axis=-1, keepdims=True))
           - jnp.exp(jnp.sum(lm[2:3] * lm[3:4], axis=-1, keepdims=True)) + lam_init)
    g = g_ref[...]

    if latent:
        i = pl.program_id(2)

        @pl.when(i == 0)
        def _():
            kall_ref[0:past, :] = kc_ref[...].astype(BF16)
            vall_ref[0:past, :] = vc_ref[...].astype(BF16)
            vall_ref[past:, :] = v_ref[...].astype(BF16)
            for half in range(2):
                hs = slice(half * hd, (half + 1) * hd)
                kall_ref[past:, hs] = _rope_tile(k_ref[:, hs], cos_ref[...], sin_ref[...]).astype(BF16)

        rows = pl.ds(pl.multiple_of(i * tq, tq), tq)
        cos_q = cos_ref[rows, :]
        sin_q = sin_ref[rows, :]

    def softmax_parts(qh, kh):
        s = _dot_nt(qh, kh)
        p = jnp.exp2(s - jnp.max(s, axis=-1, keepdims=True))
        return p.astype(BF16), 1.0 / jnp.sum(p, axis=-1, keepdims=True)

    for j in range(heads):
        parts = []
        for half in range(2):
            hs = slice(j * w + half * hd, j * w + (half + 1) * hd)
            qh = q_ref[:, hs]
            if latent:
                qh = _rope_tile(qh, cos_q, sin_q)
                kh = kall_ref[:, hs]
            else:
                kh = k_ref[:, hs].astype(BF16)
            parts.append(softmax_parts((qh * (scale * LOG2E)).astype(BF16), kh))
        (p1, r1), (p2, r2) = parts
        js = slice(j * w, (j + 1) * w)
        vj = vall_ref[...] if latent else v_ref[:, js].astype(BF16)
        o = r1 * _dot(p1, vj) - (lam * r2) * _dot(p2, vj)
        y = o * lax.rsqrt(jnp.mean(o * o, axis=-1, keepdims=True) + EPS) * g
        o_ref[:, js] = (y * (1.0 - lam_init)).astype(o_ref.dtype)


def _diff_attention(proj, lam_params, subln_g, *, batch, seq, row_off, n_heads, hd, lam_init,
                    cache_k=None, cache_v=None, prev=None):
    latent = cache_k is not None
    w = 2 * hd
    d = n_heads * w
    tq = _tile(seq, ATTN_TQ)
    nq = seq // tq
    heads = 1 if latent else _tile(n_heads, ATTN_CTX_HEADS_PER_STEP)
    assert row_off % seq == 0
    q_off, kv_off = row_off // tq, row_off // seq
    segs = n_heads // heads
    scale = hd ** -0.5
    past = cache_k.shape[0] // batch if latent else 0

    def kv_spec(rows, seg=0, off=0):
        return pl.BlockSpec((rows, heads * w), lambda b, h, i: (off + b, seg * segs + h))

    in_specs = [pl.BlockSpec(lam_params.shape, lambda b, h, i: (0, 0)),
                pl.BlockSpec((tq, heads * w), lambda b, h, i: (q_off + b * nq + i, h)),
                kv_spec(seq, 1, kv_off), kv_spec(seq, 2, kv_off)]
    args = [lam_params, proj, proj, proj]
    scratch = []
    if latent:
        cos, sin = _rope_tables(seq, hd)
        table = pl.BlockSpec((seq, hd), lambda b, h, i: (0, 0))
        in_specs += [kv_spec(past), kv_spec(past), table, table]
        args += [cache_k, cache_v, cos, sin]
        scratch = [pltpu.VMEM((past + seq, w), BF16), pltpu.VMEM((past + seq, w), BF16)]
    in_specs.append(pl.BlockSpec((1, w), lambda b, h, i: (0, 0)))
    args.append(subln_g.reshape(1, w))
    out_specs = [pl.BlockSpec((tq, heads * w), lambda b, h, i: (q_off + b * nq + i, h))]
    out_shape = [jax.ShapeDtypeStruct((proj.shape[0], d), BF16)]
    aliases = {}
    if latent:
        in_specs.append(pl.BlockSpec(memory_space=pl.ANY))
        args.append(prev)
        aliases = {len(args) - 1: 0}
    else:
        out_specs += [kv_spec(seq), kv_spec(seq)]
        out_shape += [jax.ShapeDtypeStruct((batch * seq, d), F32)] * 2
    outs = pl.pallas_call(
        functools.partial(_attn_kernel, hd=hd, heads=heads, tq=tq, past=past, scale=scale,
                          lam_init=lam_init, latent=latent),
        grid=(batch, n_heads // heads, nq),
        in_specs=in_specs,
        out_specs=out_specs,
        out_shape=out_shape,
        scratch_shapes=scratch,
        input_output_aliases=aliases,
        compiler_params=_params(("parallel", "parallel", "arbitrary")),
        name="diff_attention_" + ("latent" if latent else "context"),
    )(*args)
    return outs[0] if latent else outs


def kernel(x_prompt, x_sample, state_hgrn_fwd, state_hgrn_bwd, cache_attn_k, cache_attn_v, c, c_ctx,
           w_mod, b_mod, norm_g, w_ff1, w_ff2, hg_w_in, hg_w_out, hg_g_norm, hg_lb_logits,
           da_w_in, da_w_out, da_subln_g, da_lambda, final_norm_g):
    batch, seq, d = x_prompt.shape
    dec_batch, dec_seq, _ = x_sample.shape
    depth = w_mod.shape[0]
    _, _, hg_heads, hg_dk, hg_dv = state_hgrn_fwd.shape
    _, _, past_len, da_halves, da_hd = cache_attn_k.shape
    da_heads = da_halves // 2
    ctx_rows = batch * seq
    lat_rows = dec_batch * dec_seq
    groups = _Groups(ctx_rows, dec_batch, dec_seq)

    x = (x_prompt.reshape(ctx_rows, d), x_sample.reshape(lat_rows, d))

    n_groups = 1 + dec_batch
    g8 = -(-n_groups // V7X_SUBLANES) * V7X_SUBLANES
    cond = jnp.concatenate([c_ctx[None, :], c, jnp.zeros((g8 - n_groups, d), F32)], axis=0)
    mod = _modulation(cond, w_mod, b_mod).reshape(depth, g8, 1, 6 * d)
    SH1, SC1, GT1, SH2, SC2, GT2 = range(6)

    def mixer_weights(layer):
        j = layer // N_MIXERS
        return (_Cast(hg_w_in, j), _Cast(hg_w_out, j)) if layer % N_MIXERS == 0 else \
               (_Cast(da_w_in, j), _Cast(da_w_out, j))

    w_in = _cast_weight(mixer_weights(0)[0])
    new_sf, new_sb, new_k, new_v = [], [], [], []
    for layer in range(depth):
        h = _norm_mod(x, norm_g[layer, 0], mod, layer, SC1, SH1, groups)
        j = layer // N_MIXERS
        proj, w_out, w1 = _matmul(h, w_in, F32, casts=(mixer_weights(layer)[1], _Cast(w_ff1, layer)))
        if layer % N_MIXERS == 0:
            scan = dict(groups=groups, batch=batch, seq=seq, n_heads=hg_heads, dk=hg_dk, dv=hg_dv, d=d,
                        layer=layer)
            o_fwd, s_fwd = _hg_scan(proj, hg_lb_logits[0], state_hgrn_fwd[:, j], f_seg=2, reverse=False, **scan)
            mixed, s_bwd = _hg_scan(proj, hg_lb_logits[1], state_hgrn_bwd[:, j], f_seg=3, reverse=True,
                                    o_fwd=o_fwd, g_norm=hg_g_norm[j], g_seg=4, **scan)
            new_sf.append(s_fwd)
            new_sb.append(s_bwd)
        else:
            lam_init = 0.8 - 0.6 * math.exp(-0.3 * layer)
            att = dict(n_heads=da_heads, hd=da_hd, lam_init=lam_init)
            mixed, k_ctx, v_ctx = _diff_attention(proj, da_lambda[j], da_subln_g[j], batch=batch, seq=seq,
                                                  row_off=0, **att)
            new_k.append(k_ctx.reshape(batch, seq, da_halves, da_hd))
            new_v.append(v_ctx.reshape(batch, seq, da_heads, 2 * da_hd))
            mixed = _diff_attention(proj, da_lambda[j], da_subln_g[j], batch=dec_batch, seq=dec_seq,
                                    row_off=ctx_rows,
                                    cache_k=cache_attn_k[:, j].reshape(dec_batch * past_len, d),
                                    cache_v=cache_attn_v[:, j].reshape(dec_batch * past_len, d),
                                    prev=mixed, **att)
        x = _matmul(mixed, w_out, F32, "residual", res=x, mod=mod, layer=layer, gate_chunk=GT1, groups=groups)
        h = _norm_mod(x, norm_g[layer, 1], mod, layer, SC2, SH2, groups)
        casts = (_Cast(w_ff2, layer),) + (mixer_weights(layer + 1)[:1] if layer + 1 < depth else ())
        u, w2, *w_next = _matmul(h, w1, BF16, "relu2", casts=casts)
        w_in = w_next[0] if w_next else None
        x = _matmul(u, w2, F32, "residual", res=x, mod=mod, layer=layer, gate_chunk=GT2, groups=groups)

    y_prompt = _final_norm(x, final_norm_g, 0, ctx_rows).reshape(batch, seq, d)
    y_sample = _final_norm(x, final_norm_g, ctx_rows, lat_rows).reshape(dec_batch, dec_seq, d)
    return (y_prompt, y_sample, jnp.stack(new_sf, axis=1), jnp.stack(new_sb, axis=1),
            jnp.stack(new_k, axis=1), jnp.stack(new_v, axis=1))
```

```python
import functools
import math
from typing import NamedTuple

import jax
import jax.numpy as jnp
from jax import lax
from jax.experimental import pallas as pl
from jax.experimental.pallas import tpu as pltpu

F32 = jnp.float32
BF16 = jnp.bfloat16

GRID_W = 64
EPS = 1e-6
ROPE_BASE = 10000.0
N_MIXERS = 2

V7X_SUBLANES = 8
V7X_LANES = 128
V7X_VMEM_BYTES = 64 * 1024 * 1024
V7X_VMEM_LIMIT_BYTES = V7X_VMEM_BYTES - 4 * 1024 * 1024

MM_BM = 1024
MM_BN = 1024
MM_VMEM_BUDGET_BYTES = 44 * 1024 * 1024
CAST_VMEM_BUDGET_BYTES = 10 * 1024 * 1024
CAST_TILE_COLS = 1024
ROW_TILE = 256
MOD_BN = 512
HG_CHUNK = 128
HG_HEADS_PER_STEP = 16
ATTN_TQ = 512
ATTN_SUB_TQ = 256
ATTN_CTX_HEADS_PER_STEP = 4

LOG2E = 1.4426950408889634


def _tile(full, want):
    t = min(full, want)
    assert full % t == 0, (full, want)
    return t


def _params(sem):
    return pltpu.CompilerParams(dimension_semantics=sem, vmem_limit_bytes=V7X_VMEM_LIMIT_BYTES)


def _sigmoid(x):
    return 0.5 * jnp.tanh(0.5 * x) + 0.5


def _silu(x):
    return x * _sigmoid(x)


def _dot(a, b):
    return jnp.dot(a, b, preferred_element_type=F32)


def _dot_nt(a, b):
    return lax.dot_general(a, b, (((1,), (1,)), ((), ())), preferred_element_type=F32)


def _dot_tn(a, b):
    return lax.dot_general(a, b, (((0,), (0,)), ((), ())), preferred_element_type=F32)


class _Groups:
    def __init__(self, ctx_rows, dec_batch, dec_seq):
        self.ctx_rows = ctx_rows
        self.dec_batch = dec_batch
        self.dec_seq = dec_seq
        self.rows = ctx_rows + dec_batch * dec_seq

    def of_tile(self, i, tile):
        assert self.ctx_rows % tile == 0 and self.dec_seq % tile == 0
        g = jnp.int32(0)
        for b in range(self.dec_batch):
            g = g + (i >= (self.ctx_rows + b * self.dec_seq) // tile).astype(jnp.int32)
        return g


def _mod_kernel(c_ref, w_ref, b_ref, o_ref):
    a = _silu(c_ref[...]).astype(BF16)
    o_ref[...] = _dot(a, w_ref[...].astype(BF16)) + b_ref[...]


def _modulation(cond, w_mod, b_mod):
    g8, d = cond.shape
    depth, _, n = w_mod.shape
    bn = _tile(n, MOD_BN)
    return pl.pallas_call(
        _mod_kernel,
        grid=(depth, n // bn),
        in_specs=[
            pl.BlockSpec((g8, d), lambda l, j: (0, 0)),
            pl.BlockSpec((None, d, bn), lambda l, j: (l, 0, j)),
            pl.BlockSpec((None, 1, bn), lambda l, j: (l, 0, j)),
        ],
        out_specs=pl.BlockSpec((None, g8, bn), lambda l, j: (l, 0, j)),
        out_shape=jax.ShapeDtypeStruct((depth, g8, n), F32),
        compiler_params=_params(("arbitrary", "arbitrary")),
        name="adaln_modulation",
    )(cond, w_mod, b_mod.reshape(depth, 1, n))


def _norm_kernel(*refs, modulated, x_split):
    if modulated:
        *x_refs, g_ref, sc_ref, sh_ref, o_ref = refs
    else:
        *x_refs, g_ref, o_ref = refs
    x = x_refs[0][...]
    if len(x_refs) == 2:
        x = jnp.where(pl.program_id(0) < x_split, x, x_refs[1][...])
    y = x * lax.rsqrt(jnp.mean(x * x, axis=-1, keepdims=True) + EPS) * g_ref[...]
    if modulated:
        y = y * (1.0 + sc_ref[...]) + sh_ref[...]
    o_ref[...] = y.astype(o_ref.dtype)


def _norm_mod(x, g, mod, layer, sc_chunk, sh_chunk, groups):
    rows, d = groups.rows, g.shape[0]
    tm = _tile(math.gcd(groups.ctx_rows, groups.dec_seq), ROW_TILE)

    def vec(chunk):
        return pl.BlockSpec((None, None, 1, d), lambda i: (layer, groups.of_tile(i, tm), 0, chunk))

    if isinstance(x, tuple):
        x_split = x[0].shape[0] // tm
        x_specs = [pl.BlockSpec((tm, d), lambda i: (jnp.minimum(i, x_split - 1), 0)),
                   pl.BlockSpec((tm, d), lambda i: (jnp.maximum(i - x_split, 0), 0))]
        xs = list(x)
    else:
        x_split = 0
        x_specs = [pl.BlockSpec((tm, d), lambda i: (i, 0))]
        xs = [x]
    return pl.pallas_call(
        functools.partial(_norm_kernel, modulated=True, x_split=x_split),
        grid=(rows // tm,),
        in_specs=x_specs + [
            pl.BlockSpec((1, d), lambda i: (0, 0)),
            vec(sc_chunk),
            vec(sh_chunk),
        ],
        out_specs=pl.BlockSpec((tm, d), lambda i: (i, 0)),
        out_shape=jax.ShapeDtypeStruct((rows, d), BF16),
        compiler_params=_params(("parallel",)),
        name="rmsnorm_modulate",
    )(*xs, g.reshape(1, d), mod, mod)


def _final_norm(x, g, row_off, rows):
    d = x.shape[1]
    tm = _tile(math.gcd(rows, row_off) if row_off else rows, ROW_TILE)
    off = row_off // tm
    return pl.pallas_call(
        functools.partial(_norm_kernel, modulated=False, x_split=0),
        grid=(rows // tm,),
        in_specs=[
            pl.BlockSpec((tm, d), lambda i: (i + off, 0)),
            pl.BlockSpec((1, d), lambda i: (0, 0)),
        ],
        out_specs=pl.BlockSpec((tm, d), lambda i: (i, 0)),
        out_shape=jax.ShapeDtypeStruct((rows, d), F32),
        compiler_params=_params(("parallel",)),
        name="final_rmsnorm",
    )(x, g.reshape(1, d))


def _mm_kernel(*refs, nk, epilogue, n_res, res_split, n_casts):
    x_ref, w_ref = refs[:2]
    n_in = 2 + (n_res + 1 if epilogue == "residual" else 0)
    res_refs = refs[2:2 + n_res]
    gate_ref = refs[n_in - 1]
    o_ref = refs[n_in + n_casts]
    for src_ref, dst_ref in zip(refs[n_in:n_in + n_casts], refs[n_in + n_casts + 1:]):
        dst_ref[...] = src_ref[...].astype(dst_ref.dtype)

    def finish(acc):
        if epilogue == "relu2":
            r = jnp.maximum(acc, 0.0)
            acc = r * r
        elif epilogue == "residual":
            res = res_refs[0][...]
            if len(res_refs) == 2:
                res = jnp.where(pl.program_id(0) < res_split, res, res_refs[1][...])
            acc = res + gate_ref[...] * acc
        return acc

    def product():
        return _dot(x_ref[...], w_ref[...])

    if nk == 1:
        o_ref[...] = finish(product()).astype(o_ref.dtype)
        return

    k = pl.program_id(2)

    @pl.when(k == 0)
    def _():
        o_ref[...] = product()

    @pl.when(jnp.logical_and(k > 0, k < nk - 1))
    def _():
        o_ref[...] += product()

    @pl.when(k == nk - 1)
    def _():
        o_ref[...] = finish(o_ref[...] + product())


class _Cast(NamedTuple):
    src: jax.Array
    lead: int


def _cast_kernel(s_ref, o_ref):
    o_ref[...] = s_ref[...].astype(o_ref.dtype)


def _cast_weight(c):
    _, rows, cols = c.src.shape
    tr, tc = _tile(rows, CAST_TILE_COLS), _tile(cols, 2 * CAST_TILE_COLS)
    return pl.pallas_call(
        _cast_kernel,
        grid=(rows // tr, cols // tc),
        in_specs=[pl.BlockSpec((None, tr, tc), lambda i, j: (c.lead, i, j))],
        out_specs=pl.BlockSpec((tr, tc), lambda i, j: (i, j)),
        out_shape=jax.ShapeDtypeStruct((rows, cols), BF16),
        compiler_params=_params(("parallel", "parallel")),
        name="cast_weight",
    )(c.src)


def _matmul(x, w, out_dtype, epilogue="plain", *, res=None, mod=None, layer=None, gate_chunk=None,
            groups=None, casts=()):
    m, kdim = x.shape
    n = w.shape[1]
    bm = _tile(m, MM_BM)
    n_res = 0 if epilogue != "residual" else (2 if isinstance(res, tuple) else 1)
    out_bytes = jnp.dtype(out_dtype).itemsize + 4 * n_res

    def fits(bn, bk):
        return 2 * (2 * bk * (bm + bn) + bm * bn * out_bytes) + 4 * bm * bn <= MM_VMEM_BUDGET_BYTES

    bn_full = _tile(n, MM_BN)
    candidates = [(bn_full, kdim), (max(bn_full // 2, V7X_LANES), kdim)]
    candidates += [(bn_full, kdim // s) for s in (2, 4, 8, 16, 32)]
    bn, bk = next((bn_, bk_) for bn_, bk_ in candidates if kdim % bk_ == 0 and fits(bn_, bk_))
    nk = kdim // bk
    nj = n // bn
    assert nk == 1 or out_dtype == F32
    in_specs = [
        pl.BlockSpec((bm, bk), lambda i, j, k: (i, k)),
        pl.BlockSpec((bk, bn), lambda i, j, k: (k, j)),
    ]
    args = [x, w]
    res_split = 0
    if epilogue == "residual":
        nb = n // bn
        if isinstance(res, tuple):
            res_a, res_b = res
            res_split = res_a.shape[0] // bm
            assert res_a.shape[0] % bm == 0 and res_a.shape[0] + res_b.shape[0] == m
            in_specs += [
                pl.BlockSpec((bm, bn), lambda i, j, k: (jnp.minimum(i, res_split - 1),
                                                        jnp.where(i < res_split, j, nb - 1))),
                pl.BlockSpec((bm, bn), lambda i, j, k: (jnp.maximum(i - res_split, 0),
                                                        jnp.where(i < res_split, 0, j))),
            ]
            args += [res_a, res_b]
        else:
            in_specs.append(pl.BlockSpec((bm, bn), lambda i, j, k: (i, j)))
            args.append(res)
        in_specs.append(pl.BlockSpec((None, None, 1, bn),
                                     lambda i, j, k: (layer, groups.of_tile(i, bm), 0, gate_chunk * nb + j)))
        args.append(mod)

    out_specs = [pl.BlockSpec((bm, bn), lambda i, j, k: (i, j))]
    out_shape = [jax.ShapeDtypeStruct((m, n), out_dtype)]
    n_steps = (m // bm) * nj * nk
    cast_bytes = 0
    for c in casts:
        _, rows, cols = c.src.shape
        assert cols % CAST_TILE_COLS == 0
        tiles_c = cols // CAST_TILE_COLS
        tile_rows = next(tr for tr in (128, 256, 512, 1024, 2048)
                         if rows % tr == 0 and (rows // tr) * tiles_c <= n_steps)
        n_tiles = (rows // tile_rows) * tiles_c
        cast_bytes += 2 * (4 + 2) * tile_rows * CAST_TILE_COLS
        assert cast_bytes <= CAST_VMEM_BUDGET_BYTES

        def tile_of(i, j, k, n_tiles=n_tiles, tiles_c=tiles_c):
            t = jnp.minimum((i * nj + j) * nk + k, n_tiles - 1)
            return t // tiles_c, t % tiles_c

        in_specs.append(pl.BlockSpec((None, tile_rows, CAST_TILE_COLS),
                                     lambda i, j, k, c=c, tile_of=tile_of: (c.lead, *tile_of(i, j, k))))
        args.append(c.src)
        out_specs.append(pl.BlockSpec((tile_rows, CAST_TILE_COLS), tile_of))
        out_shape.append(jax.ShapeDtypeStruct((rows, cols), BF16))

    outs = pl.pallas_call(
        functools.partial(_mm_kernel, nk=nk, epilogue=epilogue, n_res=n_res, res_split=res_split,
                          n_casts=len(casts)),
        grid=(m // bm, nj, nk),
        in_specs=in_specs,
        out_specs=out_specs,
        out_shape=out_shape,
        compiler_params=_params(("arbitrary", "arbitrary", "arbitrary")),
        name="matmul_" + epilogue,
    )(*args)
    return outs[0] if not casts else outs


class _ScanLayout(NamedTuple):
    n_ctx: int
    per_ctx: int
    per_lat: int
    total: int
    batch: int
    reverse: bool

    def pos(self, g):
        r = self.total - 1 - g if self.reverse else g
        is_lat = r >= self.n_ctx
        rl = jnp.maximum(r - self.n_ctx, 0)
        seq_ctx = jnp.minimum(r // self.per_ctx, self.batch - 1)
        seq_lat = rl // self.per_lat
        pos = jnp.where(is_lat, rl % self.per_lat, r % self.per_ctx)
        per = jnp.where(is_lat, self.per_lat, self.per_ctx)
        start, stop = (per - 1, 0) if self.reverse else (0, per - 1)
        return r, is_lat, seq_ctx, seq_lat, pos == start, pos == stop


def _level_ref(c, half, reverse):
    rows, width = c.shape
    blk = 2 * half
    idx = half if reverse else half - 1
    assert blk % V7X_SUBLANES == 0
    c3 = c.reshape(rows // blk, blk, width)
    ref = jnp.broadcast_to(c3[:, idx:idx + 1, :], c3.shape)
    return ref.reshape(rows, width)


def _exp2_neg_abs(d):
    bits = lax.bitcast_convert_type(d, jnp.uint32) | jnp.uint32(0x80000000)
    return jnp.exp2(lax.bitcast_convert_type(bits, F32))


def _level_weight(c, fg, level, reverse):
    rows = c.shape[0]
    if level > 2:
        return _exp2_neg_abs(c - _level_ref(c, 1 << (level - 1), reverse))
    pos = lax.broadcasted_iota(jnp.int32, c.shape, 0) & ((1 << level) - 1)
    prev = pltpu.roll(fg, 1, 0)
    nxt = pltpu.roll(fg, rows - 1, 0)
    if level == 1:
        return jnp.where(pos == (0 if reverse else 1), fg, 1.0)
    if reverse:
        return jnp.where(pos == 3, prev, jnp.where(pos == 2, 1.0, jnp.where(pos == 1, fg, fg * nxt)))
    return jnp.where(pos == 0, nxt, jnp.where(pos == 1, 1.0, jnp.where(pos == 2, fg, prev * fg)))


def _hg_scan_kernel(*refs, chunk, heads, dk, dv, layer, lay, fuse_out):
    if fuse_out:
        (q_ref, v_ref, f_ref, lbl_ref, s0_ref, ofwd_ref, g_ref, gn_ref,
         o_ref, sfin_ref, st_ref) = refs
    else:
        q_ref, v_ref, f_ref, lbl_ref, s0_ref, o_ref, sfin_ref, st_ref = refs
    reverse = lay.reverse
    _, is_lat, _, _, first, last = lay.pos(pl.program_id(1))
    n_levels = chunk.bit_length() - 1
    low_levels = min(n_levels, V7X_SUBLANES.bit_length() - 1)

    @pl.when(jnp.logical_and(first, is_lat))
    def _():
        for j in range(heads):
            st_ref[j] = s0_ref[0, j].T

    @pl.when(jnp.logical_and(first, jnp.logical_not(is_lat)))
    def _():
        st_ref[...] = jnp.zeros(st_ref.shape, F32)

    lbl = lbl_ref[...]
    e = jnp.exp(lbl - jnp.max(lbl, axis=0, keepdims=True))
    lb = jnp.sum(e[:layer + 1], axis=0, keepdims=True) / jnp.sum(e, axis=0, keepdims=True)

    fg_all = lb + (1.0 - lb) * _sigmoid(f_ref[...])
    k_all = 1.0 - fg_all
    log2f = jnp.log2(fg_all)

    row = lax.broadcasted_iota(jnp.int32, (chunk, chunk), 0)
    col = lax.broadcasted_iota(jnp.int32, (chunk, chunk), 1)
    before = (col > row) if reverse else (col < row)
    tri = jnp.logical_or(before, col == row).astype(BF16)
    hi = log2f.astype(BF16)
    r1 = log2f - hi.astype(F32)
    mid = r1.astype(BF16)
    lo = (r1 - mid.astype(F32)).astype(BF16)
    c_all = _dot(tri, hi) + _dot(tri, mid) + _dot(tri, lo)

    x = row ^ col
    lvl = jnp.zeros((chunk, chunk), jnp.int32)
    for b in range(n_levels):
        lvl = lvl + (x >= (1 << b)).astype(jnp.int32)
    lvl = jnp.where(jnp.logical_or(before, x == 0), lvl, -1)

    end = 0 if reverse else chunk - 1
    if fuse_out:
        gn = gn_ref[...]

    def pair_weights(j):
        sl = slice(j * dk, (j + 1) * dk)
        q = _silu(q_ref[:, sl])
        k = k_all[:, sl]
        c = c_all[:, sl]
        fg = fg_all[:, sl]
        a = jnp.where(lvl == 0, _dot_nt(q.astype(BF16), k.astype(BF16)), 0.0)
        for level in range(1, low_levels + 1):
            w = _level_weight(c, fg, level, reverse)
            a_l = _dot_nt((q * w).astype(BF16), (k * w).astype(BF16))
            a = jnp.where(lvl == level, a_l, a)
        for level in range(low_levels + 1, n_levels + 1):
            half = 1 << (level - 1)
            w = _level_weight(c, fg, level, reverse)
            q_rows, k_rows = [], []
            for b0 in range(0, chunk, 2 * half):
                early, late = (b0 + half, b0) if reverse else (b0, b0 + half)
                q_rows.append(q[late:late + half] * w[late:late + half])
                k_scaled = k[early:early + half] * w[early:early + half]
                k_rows += [k[late:late + half], k_scaled] if reverse else [k_scaled, k[late:late + half]]
            p = _dot_nt(jnp.concatenate(q_rows, axis=0).astype(BF16),
                        jnp.concatenate(k_rows, axis=0).astype(BF16))
            pieces = []
            for b, b0 in enumerate(range(0, chunk, 2 * half)):
                early, late = (b0 + half, b0) if reverse else (b0, b0 + half)
                upd = jnp.where(lvl[late:late + half] == level, p[b * half:(b + 1) * half], a[late:late + half])
                pieces += [upd, a[early:early + half]] if reverse else [a[early:early + half], upd]
            a = jnp.concatenate(pieces, axis=0)
        return q, k, c, a

    def outputs_and_state(j, q, k, c, a):
        sv = slice(j * dv, (j + 1) * dv)
        v = v_ref[:, sv].astype(BF16)
        c_end = c[end:end + 1, :]
        st = st_ref[j]
        o = _dot(a.astype(BF16), v) + _dot_nt((q * jnp.exp2(c)).astype(BF16), st.astype(BF16))
        if fuse_out:
            o = o + ofwd_ref[:, sv]
            y = o * lax.rsqrt(jnp.mean(o * o, axis=-1, keepdims=True) + EPS) * gn
            o = y * _silu(g_ref[:, sv])
        o_ref[:, sv] = o.astype(o_ref.dtype)
        k_dec = (k * jnp.exp2(c_end - c)).astype(BF16)
        st_ref[j] = st * jnp.exp2(c_end) + _dot_tn(v, k_dec)

    current = pair_weights(0)
    for j in range(heads):
        following = pair_weights(j + 1) if j + 1 < heads else None
        outputs_and_state(j, *current)
        current = following

    @pl.when(jnp.logical_and(last, jnp.logical_not(is_lat)))
    def _():
        for j in range(heads):
            sfin_ref[0, j] = st_ref[j].T


def _hg_scan(proj, lb_logits_dir, s0_lat, *, groups, batch, seq, n_heads, dk, dv, d, f_seg, reverse, layer,
             o_fwd=None, g_norm=None, g_seg=None):
    chunk = _tile(math.gcd(seq, groups.dec_seq), HG_CHUNK)
    heads = _tile(n_heads, HG_HEADS_PER_STEP)
    assert dk == dv and d == n_heads * dk
    w = heads * dk
    segs = d // w
    lay = _ScanLayout(n_ctx=groups.ctx_rows // chunk, per_ctx=seq // chunk, per_lat=groups.dec_seq // chunk,
                      total=groups.rows // chunk, batch=batch, reverse=reverse)
    fuse_out = o_fwd is not None

    def seg_spec(seg):
        return pl.BlockSpec((chunk, w), lambda h, g: (lay.pos(g)[0], seg * segs + h))

    n_lb = lb_logits_dir.shape[0]
    in_specs = [seg_spec(0), seg_spec(1), seg_spec(f_seg),
                pl.BlockSpec((n_lb, w), lambda h, g: (0, h)),
                pl.BlockSpec((1, heads, dk, dv), lambda h, g: (lay.pos(g)[3], h, 0, 0))]
    args = [proj, proj, proj, lb_logits_dir, s0_lat]
    if fuse_out:
        in_specs += [seg_spec(0), seg_spec(g_seg), pl.BlockSpec((1, dv), lambda h, g: (0, 0))]
        args += [o_fwd, proj, g_norm.reshape(1, dv)]
    o, s_fin = pl.pallas_call(
        functools.partial(_hg_scan_kernel, chunk=chunk, heads=heads, dk=dk, dv=dv,
                          layer=layer, lay=lay, fuse_out=fuse_out),
        grid=(n_heads // heads, lay.total),
        in_specs=in_specs,
        out_specs=[
            pl.BlockSpec((chunk, w), lambda h, g: (lay.pos(g)[0], h)),
            pl.BlockSpec((1, heads, dk, dv), lambda h, g: (lay.pos(g)[2], h, 0, 0)),
        ],
        out_shape=[
            jax.ShapeDtypeStruct((groups.rows, d), BF16 if fuse_out else F32),
            jax.ShapeDtypeStruct((batch, n_heads, dk, dv), F32),
        ],
        scratch_shapes=[pltpu.VMEM((heads, dv, dk), F32)],
        compiler_params=_params(("parallel", "arbitrary")),
        name="hgrn2_scan_" + ("bwd" if reverse else "fwd"),
    )(*args)
    return o, s_fin


def _rope_tables(seq, hd):
    half = hd // 2
    rows = seq // GRID_W
    r_pos = jnp.repeat(jnp.arange(rows), GRID_W).astype(F32)
    c_pos = jnp.tile(jnp.arange(GRID_W), rows).astype(F32)
    inv_freq = ROPE_BASE ** (-jnp.arange(0, half, 2, dtype=F32) / half)
    ang_r = r_pos[:, None] * inv_freq
    ang_c = c_pos[:, None] * inv_freq
    cos = jnp.concatenate([jnp.cos(ang_r)] * 2 + [jnp.cos(ang_c)] * 2, axis=-1)
    sin = jnp.concatenate([-jnp.sin(ang_r), jnp.sin(ang_r), -jnp.sin(ang_c), jnp.sin(ang_c)], axis=-1)
    return cos, sin


def _rope_tile(x, cos, sin):
    hd = x.shape[1]
    quarter = hd // 4
    lane = lax.broadcasted_iota(jnp.int32, x.shape, 1)
    first = (lane % (2 * quarter)) < quarter
    partner = jnp.where(first, pltpu.roll(x, hd - quarter, 1), pltpu.roll(x, quarter, 1))
    return x * cos + partner * sin


def _attn_kernel(*refs, hd, heads, tq, sub, lookahead, past, scale, lam_init, latent):
    if latent:
        (lam_ref, q_ref, k_ref, v_ref, kc_ref, vc_ref, cos_ref, sin_ref, g_ref, _,
         o_ref, kall_ref, vall_ref) = refs
    else:
        lam_ref, q_ref, k_ref, v_ref, g_ref, o_ref, knew_ref, vnew_ref = refs
        knew_ref[...] = k_ref[...]
        vnew_ref[...] = v_ref[...]
    w = 2 * hd
    lm = lam_ref[...]
    lam = (jnp.exp(jnp.sum(lm[0:1] * lm[1:2], axis=-1, keepdims=True))
           - jnp.exp(jnp.sum(lm[2:3] * lm[3:4], axis=-1, keepdims=True)) + lam_init)
    g = g_ref[...]

    if latent:
        i = pl.program_id(2)

        @pl.when(i == 0)
        def _():
            kall_ref[0:past, :] = kc_ref[...].astype(BF16)
            vall_ref[0:past, :] = vc_ref[...].astype(BF16)
            vall_ref[past:, :] = v_ref[...].astype(BF16)
            for half in range(2):
                hs = slice(half * hd, (half + 1) * hd)
                kall_ref[past:, hs] = _rope_tile(k_ref[:, hs], cos_ref[...], sin_ref[...]).astype(BF16)

    def softmax_parts(qh, kh):
        s = _dot_nt(qh, kh)
        p = jnp.exp2(s - jnp.max(s, axis=-1, keepdims=True))
        return p.astype(BF16), 1.0 / jnp.sum(p, axis=-1, keepdims=True)

    units = [(j, r0) for j in range(heads) for r0 in range(0, tq, sub)]

    def probabilities(j, r0):
        parts = []
        for half in range(2):
            hs = slice(j * w + half * hd, j * w + (half + 1) * hd)
            qh = q_ref[r0:r0 + sub, hs]
            if latent:
                rows = pl.ds(pl.multiple_of(i * tq + r0, sub), sub)
                qh = _rope_tile(qh, cos_ref[rows, :], sin_ref[rows, :])
                kh = kall_ref[:, hs]
            else:
                kh = k_ref[:, hs].astype(BF16)
            parts.append(softmax_parts((qh * (scale * LOG2E)).astype(BF16), kh))
        return parts

    def output(j, r0, parts):
        (p1, r1), (p2, r2) = parts
        js = slice(j * w, (j + 1) * w)
        vj = vall_ref[...] if latent else v_ref[:, js].astype(BF16)
        o = r1 * _dot(p1, vj) - (lam * r2) * _dot(p2, vj)
        y = o * lax.rsqrt(jnp.mean(o * o, axis=-1, keepdims=True) + EPS) * g
        o_ref[r0:r0 + sub, js] = (y * (1.0 - lam_init)).astype(o_ref.dtype)

    pending = [probabilities(*u) for u in units[:lookahead]]
    for n, unit in enumerate(units):
        if n + lookahead < len(units):
            pending.append(probabilities(*units[n + lookahead]))
        output(*unit, pending.pop(0))


def _diff_attention(proj, lam_params, subln_g, *, batch, seq, row_off, n_heads, hd, lam_init,
                    cache_k=None, cache_v=None, prev=None):
    latent = cache_k is not None
    w = 2 * hd
    d = n_heads * w
    tq = _tile(seq, ATTN_TQ)
    nq = seq // tq
    heads = 1 if latent else _tile(n_heads, ATTN_CTX_HEADS_PER_STEP)
    assert row_off % seq == 0
    q_off, kv_off = row_off // tq, row_off // seq
    segs = n_heads // heads
    scale = hd ** -0.5
    past = cache_k.shape[0] // batch if latent else 0

    def kv_spec(rows, seg=0, off=0):
        return pl.BlockSpec((rows, heads * w), lambda b, h, i: (off + b, seg * segs + h))

    in_specs = [pl.BlockSpec(lam_params.shape, lambda b, h, i: (0, 0)),
                pl.BlockSpec((tq, heads * w), lambda b, h, i: (q_off + b * nq + i, h)),
                kv_spec(seq, 1, kv_off), kv_spec(seq, 2, kv_off)]
    args = [lam_params, proj, proj, proj]
    scratch = []
    if latent:
        cos, sin = _rope_tables(seq, hd)
        table = pl.BlockSpec((seq, hd), lambda b, h, i: (0, 0))
        in_specs += [kv_spec(past), kv_spec(past), table, table]
        args += [cache_k, cache_v, cos, sin]
        scratch = [pltpu.VMEM((past + seq, w), BF16), pltpu.VMEM((past + seq, w), BF16)]
    in_specs.append(pl.BlockSpec((1, w), lambda b, h, i: (0, 0)))
    args.append(subln_g.reshape(1, w))
    out_specs = [pl.BlockSpec((tq, heads * w), lambda b, h, i: (q_off + b * nq + i, h))]
    out_shape = [jax.ShapeDtypeStruct((proj.shape[0], d), BF16)]
    aliases = {}
    if latent:
        in_specs.append(pl.BlockSpec(memory_space=pl.ANY))
        args.append(prev)
        aliases = {len(args) - 1: 0}
    else:
        out_specs += [kv_spec(seq), kv_spec(seq)]
        out_shape += [jax.ShapeDtypeStruct((batch * seq, d), F32)] * 2
    outs = pl.pallas_call(
        functools.partial(_attn_kernel, hd=hd, heads=heads, tq=tq, sub=_tile(tq, ATTN_SUB_TQ),
                          lookahead=1 if latent else 0, past=past, scale=scale, lam_init=lam_init,
                          latent=latent),
        grid=(batch, n_heads // heads, nq),
        in_specs=in_specs,
        out_specs=out_specs,
        out_shape=out_shape,
        scratch_shapes=scratch,
        input_output_aliases=aliases,
        compiler_params=_params(("parallel", "parallel", "arbitrary")),
        name="diff_attention_" + ("latent" if latent else "context"),
    )(*args)
    return outs[0] if latent else outs


def kernel(x_prompt, x_sample, state_hgrn_fwd, state_hgrn_bwd, cache_attn_k, cache_attn_v, c, c_ctx,
           w_mod, b_mod, norm_g, w_ff1, w_ff2, hg_w_in, hg_w_out, hg_g_norm, hg_lb_logits,
           da_w_in, da_w_out, da_subln_g, da_lambda, final_norm_g):
    batch, seq, d = x_prompt.shape
    dec_batch, dec_seq, _ = x_sample.shape
    depth = w_mod.shape[0]
    _, _, hg_heads, hg_dk, hg_dv = state_hgrn_fwd.shape
    _, _, past_len, da_halves, da_hd = cache_attn_k.shape
    da_heads = da_halves // 2
    ctx_rows = batch * seq
    lat_rows = dec_batch * dec_seq
    groups = _Groups(ctx_rows, dec_batch, dec_seq)

    x = (x_prompt.reshape(ctx_rows, d), x_sample.reshape(lat_rows, d))

    n_groups = 1 + dec_batch
    g8 = -(-n_groups // V7X_SUBLANES) * V7X_SUBLANES
    cond = jnp.concatenate([c_ctx[None, :], c, jnp.zeros((g8 - n_groups, d), F32)], axis=0)
    mod = _modulation(cond, w_mod, b_mod).reshape(depth, g8, 1, 6 * d)
    SH1, SC1, GT1, SH2, SC2, GT2 = range(6)

    def mixer_weights(layer):
        j = layer // N_MIXERS
        return (_Cast(hg_w_in, j), _Cast(hg_w_out, j)) if layer % N_MIXERS == 0 else \
               (_Cast(da_w_in, j), _Cast(da_w_out, j))

    w_in = _cast_weight(mixer_weights(0)[0])
    new_sf, new_sb, new_k, new_v = [], [], [], []
    for layer in range(depth):
        h = _norm_mod(x, norm_g[layer, 0], mod, layer, SC1, SH1, groups)
        j = layer // N_MIXERS
        proj, w_out, w1 = _matmul(h, w_in, F32, casts=(mixer_weights(layer)[1], _Cast(w_ff1, layer)))
        if layer % N_MIXERS == 0:
            scan = dict(groups=groups, batch=batch, seq=seq, n_heads=hg_heads, dk=hg_dk, dv=hg_dv, d=d,
                        layer=layer)
            o_fwd, s_fwd = _hg_scan(proj, hg_lb_logits[0], state_hgrn_fwd[:, j], f_seg=2, reverse=False, **scan)
            mixed, s_bwd = _hg_scan(proj, hg_lb_logits[1], state_hgrn_bwd[:, j], f_seg=3, reverse=True,
                                    o_fwd=o_fwd, g_norm=hg_g_norm[j], g_seg=4, **scan)
            new_sf.append(s_fwd)
            new_sb.append(s_bwd)
        else:
            lam_init = 0.8 - 0.6 * math.exp(-0.3 * layer)
            att = dict(n_heads=da_heads, hd=da_hd, lam_init=lam_init)
            mixed, k_ctx, v_ctx = _diff_attention(proj, da_lambda[j], da_subln_g[j], batch=batch, seq=seq,
                                                  row_off=0, **att)
            new_k.append(k_ctx.reshape(batch, seq, da_halves, da_hd))
            new_v.append(v_ctx.reshape(batch, seq, da_heads, 2 * da_hd))
            mixed = _diff_attention(proj, da_lambda[j], da_subln_g[j], batch=dec_batch, seq=dec_seq,
                                    row_off=ctx_rows,
                                    cache_k=cache_attn_k[:, j].reshape(dec_batch * past_len, d),
                                    cache_v=cache_attn_v[:, j].reshape(dec_batch * past_len, d),
                                    prev=mixed, **att)
        x = _matmul(mixed, w_out, F32, "residual", res=x, mod=mod, layer=layer, gate_chunk=GT1, groups=groups)
        h = _norm_mod(x, norm_g[layer, 1], mod, layer, SC2, SH2, groups)
        casts = (_Cast(w_ff2, layer),) + (mixer_weights(layer + 1)[:1] if layer + 1 < depth else ())
        u, w2, *w_next = _matmul(h, w1, BF16, "relu2", casts=casts)
        w_in = w_next[0] if w_next else None
        x = _matmul(u, w2, F32, "residual", res=x, mod=mod, layer=layer, gate_chunk=GT2, groups=groups)

    y_prompt = _final_norm(x, final_norm_g, 0, ctx_rows).reshape(batch, seq, d)
    y_sample = _final_norm(x, final_norm_g, ctx_rows, lat_rows).reshape(dec_batch, dec_seq, d)
    return (y_prompt, y_sample, jnp.stack(new_sf, axis=1), jnp.stack(new_sb, axis=1),
            jnp.stack(new_k, axis=1), jnp.stack(new_v, axis=1))
```

```python
import functools
import math
from typing import NamedTuple

import jax
import jax.numpy as jnp
from jax import lax
from jax.experimental import pallas as pl
from jax.experimental.pallas import tpu as pltpu

F32 = jnp.float32
BF16 = jnp.bfloat16

GRID_W = 64
EPS = 1e-6
ROPE_BASE = 10000.0
N_MIXERS = 2

V7X_SUBLANES = 8
V7X_LANES = 128
V7X_VMEM_BYTES = 64 * 1024 * 1024
V7X_VMEM_LIMIT_BYTES = V7X_VMEM_BYTES - 4 * 1024 * 1024

MM_BM = 1024
MM_BN = 1024
MM_VMEM_BUDGET_BYTES = 52 * 1024 * 1024
CAST_VMEM_BUDGET_BYTES = 10 * 1024 * 1024
CAST_TILE_COLS = 1024
ROW_TILE = 512
MOD_BN = 512
HG_CHUNK = 128
HG_HEADS_PER_STEP = 16
HG_LOOKAHEAD = 1
ATTN_TQ = 1024
ATTN_SUB_TQ = 256
ATTN_CTX_HEADS_PER_STEP = 8

LOG2E = 1.4426950408889634


def _tile(full, want):
    t = min(full, want)
    assert full % t == 0, (full, want)
    return t


def _params(sem):
    return pltpu.CompilerParams(dimension_semantics=sem, vmem_limit_bytes=V7X_VMEM_LIMIT_BYTES)


def _sigmoid(x):
    return 0.5 * jnp.tanh(0.5 * x) + 0.5


def _silu(x):
    return x * _sigmoid(x)


def _dot(a, b):
    return jnp.dot(a, b, preferred_element_type=F32)


def _dot_nt(a, b):
    return lax.dot_general(a, b, (((1,), (1,)), ((), ())), preferred_element_type=F32)


def _dot_tn(a, b):
    return lax.dot_general(a, b, (((0,), (0,)), ((), ())), preferred_element_type=F32)


class _Groups:
    def __init__(self, ctx_rows, dec_batch, dec_seq):
        self.ctx_rows = ctx_rows
        self.dec_batch = dec_batch
        self.dec_seq = dec_seq
        self.rows = ctx_rows + dec_batch * dec_seq

    def of_tile(self, i, tile):
        assert self.ctx_rows % tile == 0 and self.dec_seq % tile == 0
        g = jnp.int32(0)
        for b in range(self.dec_batch):
            g = g + (i >= (self.ctx_rows + b * self.dec_seq) // tile).astype(jnp.int32)
        return g


def _mod_kernel(c_ref, w_ref, b_ref, o_ref):
    a = _silu(c_ref[...]).astype(BF16)
    o_ref[...] = _dot(a, w_ref[...].astype(BF16)) + b_ref[...]


def _modulation(cond, w_mod, b_mod):
    g8, d = cond.shape
    depth, _, n = w_mod.shape
    bn = _tile(n, MOD_BN)
    return pl.pallas_call(
        _mod_kernel,
        grid=(depth, n // bn),
        in_specs=[
            pl.BlockSpec((g8, d), lambda l, j: (0, 0)),
            pl.BlockSpec((None, d, bn), lambda l, j: (l, 0, j)),
            pl.BlockSpec((None, 1, bn), lambda l, j: (l, 0, j)),
        ],
        out_specs=pl.BlockSpec((None, g8, bn), lambda l, j: (l, 0, j)),
        out_shape=jax.ShapeDtypeStruct((depth, g8, n), F32),
        compiler_params=_params(("arbitrary", "arbitrary")),
        name="adaln_modulation",
    )(cond, w_mod, b_mod.reshape(depth, 1, n))


def _norm_kernel(*refs, modulated, x_split):
    if modulated:
        *x_refs, g_ref, sc_ref, sh_ref, o_ref = refs
    else:
        *x_refs, g_ref, o_ref = refs
    x = x_refs[0][...]
    if len(x_refs) == 2:
        x = jnp.where(pl.program_id(0) < x_split, x, x_refs[1][...])
    y = x * lax.rsqrt(jnp.mean(x * x, axis=-1, keepdims=True) + EPS) * g_ref[...]
    if modulated:
        y = y * (1.0 + sc_ref[...]) + sh_ref[...]
    o_ref[...] = y.astype(o_ref.dtype)


def _norm_mod(x, g, mod, layer, sc_chunk, sh_chunk, groups):
    rows, d = groups.rows, g.shape[0]
    tm = _tile(math.gcd(groups.ctx_rows, groups.dec_seq), ROW_TILE)

    def vec(chunk):
        return pl.BlockSpec((None, None, 1, d), lambda i: (layer, groups.of_tile(i, tm), 0, chunk))

    if isinstance(x, tuple):
        x_split = x[0].shape[0] // tm
        x_specs = [pl.BlockSpec((tm, d), lambda i: (jnp.minimum(i, x_split - 1), 0)),
                   pl.BlockSpec((tm, d), lambda i: (jnp.maximum(i - x_split, 0), 0))]
        xs = list(x)
    else:
        x_split = 0
        x_specs = [pl.BlockSpec((tm, d), lambda i: (i, 0))]
        xs = [x]
    return pl.pallas_call(
        functools.partial(_norm_kernel, modulated=True, x_split=x_split),
        grid=(rows // tm,),
        in_specs=x_specs + [
            pl.BlockSpec((1, d), lambda i: (0, 0)),
            vec(sc_chunk),
            vec(sh_chunk),
        ],
        out_specs=pl.BlockSpec((tm, d), lambda i: (i, 0)),
        out_shape=jax.ShapeDtypeStruct((rows, d), BF16),
        compiler_params=_params(("parallel",)),
        name="rmsnorm_modulate",
    )(*xs, g.reshape(1, d), mod, mod)


def _final_norm(x, g, row_off, rows):
    d = x.shape[1]
    tm = _tile(math.gcd(rows, row_off) if row_off else rows, ROW_TILE)
    off = row_off // tm
    return pl.pallas_call(
        functools.partial(_norm_kernel, modulated=False, x_split=0),
        grid=(rows // tm,),
        in_specs=[
            pl.BlockSpec((tm, d), lambda i: (i + off, 0)),
            pl.BlockSpec((1, d), lambda i: (0, 0)),
        ],
        out_specs=pl.BlockSpec((tm, d), lambda i: (i, 0)),
        out_shape=jax.ShapeDtypeStruct((rows, d), F32),
        compiler_params=_params(("parallel",)),
        name="final_rmsnorm",
    )(x, g.reshape(1, d))


def _mm_kernel(*refs, nk, epilogue, n_res, res_split, n_casts):
    x_ref, w_ref = refs[:2]
    n_in = 2 + (n_res + 1 if epilogue == "residual" else 0)
    res_refs = refs[2:2 + n_res]
    gate_ref = refs[n_in - 1]
    o_ref = refs[n_in + n_casts]
    for src_ref, dst_ref in zip(refs[n_in:n_in + n_casts], refs[n_in + n_casts + 1:]):
        dst_ref[...] = src_ref[...].astype(dst_ref.dtype)

    def finish(acc):
        if epilogue == "relu2":
            r = jnp.maximum(acc, 0.0)
            acc = r * r
        elif epilogue == "residual":
            res = res_refs[0][...]
            if len(res_refs) == 2:
                res = jnp.where(pl.program_id(0) < res_split, res, res_refs[1][...])
            acc = res + gate_ref[...] * acc
        return acc

    def product():
        return _dot(x_ref[...], w_ref[...])

    if nk == 1:
        o_ref[...] = finish(product()).astype(o_ref.dtype)
        return

    k = pl.program_id(2)

    @pl.when(k == 0)
    def _():
        o_ref[...] = product()

    @pl.when(jnp.logical_and(k > 0, k < nk - 1))
    def _():
        o_ref[...] += product()

    @pl.when(k == nk - 1)
    def _():
        o_ref[...] = finish(o_ref[...] + product())


class _Cast(NamedTuple):
    src: jax.Array
    lead: int


def _cast_kernel(s_ref, o_ref):
    o_ref[...] = s_ref[...].astype(o_ref.dtype)


def _cast_weight(c):
    _, rows, cols = c.src.shape
    tr, tc = _tile(rows, CAST_TILE_COLS), _tile(cols, 2 * CAST_TILE_COLS)
    return pl.pallas_call(
        _cast_kernel,
        grid=(rows // tr, cols // tc),
        in_specs=[pl.BlockSpec((None, tr, tc), lambda i, j: (c.lead, i, j))],
        out_specs=pl.BlockSpec((tr, tc), lambda i, j: (i, j)),
        out_shape=jax.ShapeDtypeStruct((rows, cols), BF16),
        compiler_params=_params(("parallel", "parallel")),
        name="cast_weight",
    )(c.src)


def _matmul(x, w, out_dtype, epilogue="plain", *, res=None, mod=None, layer=None, gate_chunk=None,
            groups=None, casts=()):
    m, kdim = x.shape
    n = w.shape[1]
    bm = _tile(m, MM_BM)
    n_res = 0 if epilogue != "residual" else (2 if isinstance(res, tuple) else 1)
    out_bytes = jnp.dtype(out_dtype).itemsize + 4 * n_res

    def fits(bn, bk):
        return 2 * (2 * bk * (bm + bn) + bm * bn * out_bytes) + 4 * bm * bn <= MM_VMEM_BUDGET_BYTES

    bn_full = _tile(n, MM_BN)
    candidates = [(bn_full, kdim), (max(bn_full // 2, V7X_LANES), kdim)]
    candidates += [(bn_full, kdim // s) for s in (2, 4, 8, 16, 32)]
    bn, bk = next((bn_, bk_) for bn_, bk_ in candidates if kdim % bk_ == 0 and fits(bn_, bk_))
    nk = kdim // bk
    nj = n // bn
    assert nk == 1 or out_dtype == F32
    in_specs = [
        pl.BlockSpec((bm, bk), lambda i, j, k: (i, k)),
        pl.BlockSpec((bk, bn), lambda i, j, k: (k, j)),
    ]
    args = [x, w]
    res_split = 0
    if epilogue == "residual":
        nb = n // bn
        if isinstance(res, tuple):
            res_a, res_b = res
            res_split = res_a.shape[0] // bm
            assert res_a.shape[0] % bm == 0 and res_a.shape[0] + res_b.shape[0] == m
            in_specs += [
                pl.BlockSpec((bm, bn), lambda i, j, k: (jnp.minimum(i, res_split - 1),
                                                        jnp.where(i < res_split, j, nb - 1))),
                pl.BlockSpec((bm, bn), lambda i, j, k: (jnp.maximum(i - res_split, 0),
                                                        jnp.where(i < res_split, 0, j))),
            ]
            args += [res_a, res_b]
        else:
            in_specs.append(pl.BlockSpec((bm, bn), lambda i, j, k: (i, j)))
            args.append(res)
        in_specs.append(pl.BlockSpec((None, None, 1, bn),
                                     lambda i, j, k: (layer, groups.of_tile(i, bm), 0, gate_chunk * nb + j)))
        args.append(mod)

    out_specs = [pl.BlockSpec((bm, bn), lambda i, j, k: (i, j))]
    out_shape = [jax.ShapeDtypeStruct((m, n), out_dtype)]
    n_steps = (m // bm) * nj * nk
    cast_bytes = 0
    for c in casts:
        _, rows, cols = c.src.shape
        assert cols % CAST_TILE_COLS == 0
        tiles_c = cols // CAST_TILE_COLS
        tile_rows = next(tr for tr in (128, 256, 512, 1024, 2048)
                         if rows % tr == 0 and (rows // tr) * tiles_c <= n_steps)
        n_tiles = (rows // tile_rows) * tiles_c
        cast_bytes += 2 * (4 + 2) * tile_rows * CAST_TILE_COLS
        assert cast_bytes <= CAST_VMEM_BUDGET_BYTES

        def tile_of(i, j, k, n_tiles=n_tiles, tiles_c=tiles_c):
            t = jnp.minimum((i * nj + j) * nk + k, n_tiles - 1)
            return t // tiles_c, t % tiles_c

        in_specs.append(pl.BlockSpec((None, tile_rows, CAST_TILE_COLS),
                                     lambda i, j, k, c=c, tile_of=tile_of: (c.lead, *tile_of(i, j, k))))
        args.append(c.src)
        out_specs.append(pl.BlockSpec((tile_rows, CAST_TILE_COLS), tile_of))
        out_shape.append(jax.ShapeDtypeStruct((rows, cols), BF16))

    outs = pl.pallas_call(
        functools.partial(_mm_kernel, nk=nk, epilogue=epilogue, n_res=n_res, res_split=res_split,
                          n_casts=len(casts)),
        grid=(m // bm, nj, nk),
        in_specs=in_specs,
        out_specs=out_specs,
        out_shape=out_shape,
        compiler_params=_params(("arbitrary", "arbitrary", "arbitrary")),
        name="matmul_" + epilogue,
    )(*args)
    return outs[0] if not casts else outs


class _ScanLayout(NamedTuple):
    n_ctx: int
    per_ctx: int
    per_lat: int
    total: int
    batch: int
    reverse: bool

    def pos(self, g):
        r = self.total - 1 - g if self.reverse else g
        is_lat = r >= self.n_ctx
        rl = jnp.maximum(r - self.n_ctx, 0)
        seq_ctx = jnp.minimum(r // self.per_ctx, self.batch - 1)
        seq_lat = rl // self.per_lat
        pos = jnp.where(is_lat, rl % self.per_lat, r % self.per_ctx)
        per = jnp.where(is_lat, self.per_lat, self.per_ctx)
        start, stop = (per - 1, 0) if self.reverse else (0, per - 1)
        return r, is_lat, seq_ctx, seq_lat, pos == start, pos == stop


def _level_ref(c, half, reverse):
    rows, width = c.shape
    blk = 2 * half
    idx = half if reverse else half - 1
    assert blk % V7X_SUBLANES == 0
    c3 = c.reshape(rows // blk, blk, width)
    ref = jnp.broadcast_to(c3[:, idx:idx + 1, :], c3.shape)
    return ref.reshape(rows, width)


def _exp2_neg_abs(d):
    bits = lax.bitcast_convert_type(d, jnp.uint32) | jnp.uint32(0x80000000)
    return jnp.exp2(lax.bitcast_convert_type(bits, F32))


def _level_weight(c, fg, level, reverse):
    rows = c.shape[0]
    if level > 2:
        return _exp2_neg_abs(c - _level_ref(c, 1 << (level - 1), reverse))
    pos = lax.broadcasted_iota(jnp.int32, c.shape, 0) & ((1 << level) - 1)
    prev = pltpu.roll(fg, 1, 0)
    nxt = pltpu.roll(fg, rows - 1, 0)
    if level == 1:
        return jnp.where(pos == (0 if reverse else 1), fg, 1.0)
    if reverse:
        return jnp.where(pos == 3, prev, jnp.where(pos == 2, 1.0, jnp.where(pos == 1, fg, fg * nxt)))
    return jnp.where(pos == 0, nxt, jnp.where(pos == 1, 1.0, jnp.where(pos == 2, fg, prev * fg)))


def _hg_scan_kernel(*refs, chunk, heads, dk, dv, layer, lay, fuse_out):
    if fuse_out:
        (q_ref, v_ref, f_ref, lbl_ref, s0_ref, ofwd_ref, g_ref, gn_ref,
         o_ref, sfin_ref, st_ref) = refs
    else:
        q_ref, v_ref, f_ref, lbl_ref, s0_ref, o_ref, sfin_ref, st_ref = refs
    reverse = lay.reverse
    _, is_lat, _, _, first, last = lay.pos(pl.program_id(1))
    n_levels = chunk.bit_length() - 1
    low_levels = min(n_levels, V7X_SUBLANES.bit_length() - 1)

    @pl.when(jnp.logical_and(first, is_lat))
    def _():
        for j in range(heads):
            st_ref[j] = s0_ref[0, j].T

    @pl.when(jnp.logical_and(first, jnp.logical_not(is_lat)))
    def _():
        st_ref[...] = jnp.zeros(st_ref.shape, F32)

    lbl = lbl_ref[...]
    e = jnp.exp(lbl - jnp.max(lbl, axis=0, keepdims=True))
    lb = jnp.sum(e[:layer + 1], axis=0, keepdims=True) / jnp.sum(e, axis=0, keepdims=True)

    fg_all = lb + (1.0 - lb) * _sigmoid(f_ref[...])
    k_all = 1.0 - fg_all
    log2f = jnp.log2(fg_all)

    row = lax.broadcasted_iota(jnp.int32, (chunk, chunk), 0)
    col = lax.broadcasted_iota(jnp.int32, (chunk, chunk), 1)
    before = (col > row) if reverse else (col < row)
    tri = jnp.logical_or(before, col == row).astype(BF16)
    hi = log2f.astype(BF16)
    r1 = log2f - hi.astype(F32)
    mid = r1.astype(BF16)
    lo = (r1 - mid.astype(F32)).astype(BF16)
    c_all = _dot(tri, hi) + _dot(tri, mid) + _dot(tri, lo)

    x = row ^ col
    lvl = jnp.zeros((chunk, chunk), jnp.int32)
    for b in range(n_levels):
        lvl = lvl + (x >= (1 << b)).astype(jnp.int32)
    lvl = jnp.where(jnp.logical_or(before, x == 0), lvl, -1)

    end = 0 if reverse else chunk - 1
    if fuse_out:
        gn = gn_ref[...]

    def pair_weights(j):
        sl = slice(j * dk, (j + 1) * dk)
        q = _silu(q_ref[:, sl])
        k = k_all[:, sl]
        c = c_all[:, sl]
        fg = fg_all[:, sl]
        a = jnp.where(lvl == 0, _dot_nt(q.astype(BF16), k.astype(BF16)), 0.0)
        for level in range(1, low_levels + 1):
            w = _level_weight(c, fg, level, reverse)
            a_l = _dot_nt((q * w).astype(BF16), (k * w).astype(BF16))
            a = jnp.where(lvl == level, a_l, a)
        for level in range(low_levels + 1, n_levels + 1):
            half = 1 << (level - 1)
            w = _level_weight(c, fg, level, reverse)
            q_rows, k_rows = [], []
            for b0 in range(0, chunk, 2 * half):
                early, late = (b0 + half, b0) if reverse else (b0, b0 + half)
                q_rows.append(q[late:late + half] * w[late:late + half])
                k_scaled = k[early:early + half] * w[early:early + half]
                k_rows += [k[late:late + half], k_scaled] if reverse else [k_scaled, k[late:late + half]]
            p = _dot_nt(jnp.concatenate(q_rows, axis=0).astype(BF16),
                        jnp.concatenate(k_rows, axis=0).astype(BF16))
            pieces = []
            for b, b0 in enumerate(range(0, chunk, 2 * half)):
                early, late = (b0 + half, b0) if reverse else (b0, b0 + half)
                upd = jnp.where(lvl[late:late + half] == level, p[b * half:(b + 1) * half], a[late:late + half])
                pieces += [upd, a[early:early + half]] if reverse else [a[early:early + half], upd]
            a = jnp.concatenate(pieces, axis=0)
        return q, k, c, a

    def outputs_and_state(j, q, k, c, a):
        sv = slice(j * dv, (j + 1) * dv)
        v = v_ref[:, sv].astype(BF16)
        c_end = c[end:end + 1, :]
        st = st_ref[j]
        o = _dot(a.astype(BF16), v) + _dot_nt((q * jnp.exp2(c)).astype(BF16), st.astype(BF16))
        if fuse_out:
            o = o + ofwd_ref[:, sv]
            y = o * lax.rsqrt(jnp.mean(o * o, axis=-1, keepdims=True) + EPS) * gn
            o = y * _silu(g_ref[:, sv])
        o_ref[:, sv] = o.astype(o_ref.dtype)
        k_dec = (k * jnp.exp2(c_end - c)).astype(BF16)
        st_ref[j] = st * jnp.exp2(c_end) + _dot_tn(v, k_dec)

    pending = [pair_weights(j) for j in range(min(HG_LOOKAHEAD, heads))]
    for j in range(heads):
        if j + HG_LOOKAHEAD < heads:
            pending.append(pair_weights(j + HG_LOOKAHEAD))
        outputs_and_state(j, *pending.pop(0))

    @pl.when(jnp.logical_and(last, jnp.logical_not(is_lat)))
    def _():
        for j in range(heads):
            sfin_ref[0, j] = st_ref[j].T


def _hg_scan(proj, lb_logits_dir, s0_lat, *, groups, batch, seq, n_heads, dk, dv, d, f_seg, reverse, layer,
             o_fwd=None, g_norm=None, g_seg=None):
    chunk = _tile(math.gcd(seq, groups.dec_seq), HG_CHUNK)
    heads = _tile(n_heads, HG_HEADS_PER_STEP)
    assert dk == dv and d == n_heads * dk
    w = heads * dk
    segs = d // w
    lay = _ScanLayout(n_ctx=groups.ctx_rows // chunk, per_ctx=seq // chunk, per_lat=groups.dec_seq // chunk,
                      total=groups.rows // chunk, batch=batch, reverse=reverse)
    fuse_out = o_fwd is not None

    def seg_spec(seg):
        return pl.BlockSpec((chunk, w), lambda h, g: (lay.pos(g)[0], seg * segs + h))

    n_lb = lb_logits_dir.shape[0]
    in_specs = [seg_spec(0), seg_spec(1), seg_spec(f_seg),
                pl.BlockSpec((n_lb, w), lambda h, g: (0, h)),
                pl.BlockSpec((1, heads, dk, dv), lambda h, g: (lay.pos(g)[3], h, 0, 0))]
    args = [proj, proj, proj, lb_logits_dir, s0_lat]
    if fuse_out:
        in_specs += [seg_spec(0), seg_spec(g_seg), pl.BlockSpec((1, dv), lambda h, g: (0, 0))]
        args += [o_fwd, proj, g_norm.reshape(1, dv)]
    o, s_fin = pl.pallas_call(
        functools.partial(_hg_scan_kernel, chunk=chunk, heads=heads, dk=dk, dv=dv,
                          layer=layer, lay=lay, fuse_out=fuse_out),
        grid=(n_heads // heads, lay.total),
        in_specs=in_specs,
        out_specs=[
            pl.BlockSpec((chunk, w), lambda h, g: (lay.pos(g)[0], h)),
            pl.BlockSpec((1, heads, dk, dv), lambda h, g: (lay.pos(g)[2], h, 0, 0)),
        ],
        out_shape=[
            jax.ShapeDtypeStruct((groups.rows, d), BF16 if fuse_out else F32),
            jax.ShapeDtypeStruct((batch, n_heads, dk, dv), F32),
        ],
        scratch_shapes=[pltpu.VMEM((heads, dv, dk), F32)],
        compiler_params=_params(("parallel", "arbitrary")),
        name="hgrn2_scan_" + ("bwd" if reverse else "fwd"),
    )(*args)
    return o, s_fin


def _rope_tables(seq, hd):
    half = hd // 2
    rows = seq // GRID_W
    r_pos = jnp.repeat(jnp.arange(rows), GRID_W).astype(F32)
    c_pos = jnp.tile(jnp.arange(GRID_W), rows).astype(F32)
    inv_freq = ROPE_BASE ** (-jnp.arange(0, half, 2, dtype=F32) / half)
    ang_r = r_pos[:, None] * inv_freq
    ang_c = c_pos[:, None] * inv_freq
    cos = jnp.concatenate([jnp.cos(ang_r)] * 2 + [jnp.cos(ang_c)] * 2, axis=-1)
    sin = jnp.concatenate([-jnp.sin(ang_r), jnp.sin(ang_r), -jnp.sin(ang_c), jnp.sin(ang_c)], axis=-1)
    return cos, sin


def _rope_tile(x, cos, sin):
    hd = x.shape[1]
    quarter = hd // 4
    lane = lax.broadcasted_iota(jnp.int32, x.shape, 1)
    first = (lane % (2 * quarter)) < quarter
    partner = jnp.where(first, pltpu.roll(x, hd - quarter, 1), pltpu.roll(x, quarter, 1))
    return x * cos + partner * sin


def _attn_kernel(*refs, hd, heads, tq, sub, lookahead, past, scale, lam_init, latent):
    if latent:
        (lam_ref, q_ref, k_ref, v_ref, kc_ref, vc_ref, cos_ref, sin_ref, g_ref, _,
         o_ref, kall_ref, vall_ref) = refs
    else:
        lam_ref, q_ref, k_ref, v_ref, g_ref, _, o_ref, knew_ref, vnew_ref = refs
        knew_ref[...] = k_ref[...]
        vnew_ref[...] = v_ref[...]
    w = 2 * hd
    lm = lam_ref[...]
    lam = (jnp.exp(jnp.sum(lm[0:1] * lm[1:2], axis=-1, keepdims=True))
           - jnp.exp(jnp.sum(lm[2:3] * lm[3:4], axis=-1, keepdims=True)) + lam_init)
    g = g_ref[...]

    if latent:
        i = pl.program_id(2)

        @pl.when(i == 0)
        def _():
            kall_ref[0:past, :] = kc_ref[...].astype(BF16)
            vall_ref[0:past, :] = vc_ref[...].astype(BF16)
            vall_ref[past:, :] = v_ref[...].astype(BF16)
            for half in range(2):
                hs = slice(half * hd, (half + 1) * hd)
                kall_ref[past:, hs] = _rope_tile(k_ref[:, hs], cos_ref[...], sin_ref[...]).astype(BF16)

    def softmax_parts(qh, kh):
        s = _dot_nt(qh, kh)
        p = jnp.exp2(s - jnp.max(s, axis=-1, keepdims=True))
        return p.astype(BF16), 1.0 / jnp.sum(p, axis=-1, keepdims=True)

    units = [(j, r0) for j in range(heads) for r0 in range(0, tq, sub)]

    def probabilities(j, r0):
        parts = []
        for half in range(2):
            hs = slice(j * w + half * hd, j * w + (half + 1) * hd)
            qh = q_ref[r0:r0 + sub, hs]
            if latent:
                rows = pl.ds(pl.multiple_of(i * tq + r0, sub), sub)
                qh = _rope_tile(qh, cos_ref[rows, :], sin_ref[rows, :])
                kh = kall_ref[:, hs]
            else:
                kh = k_ref[:, hs].astype(BF16)
            parts.append(softmax_parts((qh * (scale * LOG2E)).astype(BF16), kh))
        return parts

    def output(j, r0, parts):
        (p1, r1), (p2, r2) = parts
        js = slice(j * w, (j + 1) * w)
        vj = vall_ref[...] if latent else v_ref[:, js].astype(BF16)
        o = r1 * _dot(p1, vj) - (lam * r2) * _dot(p2, vj)
        y = o * lax.rsqrt(jnp.mean(o * o, axis=-1, keepdims=True) + EPS) * g
        o_ref[r0:r0 + sub, js] = (y * (1.0 - lam_init)).astype(o_ref.dtype)

    pending = [probabilities(*u) for u in units[:lookahead]]
    for n, unit in enumerate(units):
        if n + lookahead < len(units):
            pending.append(probabilities(*units[n + lookahead]))
        output(*unit, pending.pop(0))


def _diff_attention(proj, lam_params, subln_g, *, batch, seq, row_off, n_heads, hd, lam_init,
                    cache_k=None, cache_v=None, prev=None):
    latent = cache_k is not None
    w = 2 * hd
    d = n_heads * w
    tq = _tile(seq, ATTN_TQ)
    nq = seq // tq
    heads = 1 if latent else _tile(n_heads, ATTN_CTX_HEADS_PER_STEP)
    assert row_off % seq == 0
    q_off, kv_off = row_off // tq, row_off // seq
    segs = n_heads // heads
    scale = hd ** -0.5
    past = cache_k.shape[0] // batch if latent else 0

    def kv_spec(rows, seg=0, off=0):
        return pl.BlockSpec((rows, heads * w), lambda b, h, i: (off + b, seg * segs + h))

    in_specs = [pl.BlockSpec(lam_params.shape, lambda b, h, i: (0, 0)),
                pl.BlockSpec((tq, heads * w), lambda b, h, i: (q_off + b * nq + i, h)),
                kv_spec(seq, 1, kv_off), kv_spec(seq, 2, kv_off)]
    args = [lam_params, proj, proj, proj]
    scratch = []
    if latent:
        cos, sin = _rope_tables(seq, hd)
        table = pl.BlockSpec((seq, hd), lambda b, h, i: (0, 0))
        in_specs += [kv_spec(past), kv_spec(past), table, table]
        args += [cache_k, cache_v, cos, sin]
        scratch = [pltpu.VMEM((past + seq, w), BF16), pltpu.VMEM((past + seq, w), BF16)]
    in_specs.append(pl.BlockSpec((1, w), lambda b, h, i: (0, 0)))
    args.append(subln_g.reshape(1, w))
    out_specs = [pl.BlockSpec((tq, heads * w), lambda b, h, i: (q_off + b * nq + i, h))]
    out_shape = [jax.ShapeDtypeStruct((proj.shape[0], d), BF16)]
    in_specs.append(pl.BlockSpec(memory_space=pl.ANY))
    args.append(prev)
    aliases = {len(args) - 1: 0}
    if not latent:
        out_specs += [kv_spec(seq), kv_spec(seq)]
        out_shape += [jax.ShapeDtypeStruct((batch * seq, d), F32)] * 2
    outs = pl.pallas_call(
        functools.partial(_attn_kernel, hd=hd, heads=heads, tq=tq, sub=_tile(tq, ATTN_SUB_TQ),
                          lookahead=1 if latent else 0, past=past, scale=scale, lam_init=lam_init,
                          latent=latent),
        grid=(batch, n_heads // heads, nq),
        in_specs=in_specs,
        out_specs=out_specs,
        out_shape=out_shape,
        scratch_shapes=scratch,
        input_output_aliases=aliases,
        compiler_params=_params(("parallel", "parallel", "arbitrary")),
        name="diff_attention_" + ("latent" if latent else "context"),
    )(*args)
    return outs[0] if latent else outs


def kernel(x_prompt, x_sample, state_hgrn_fwd, state_hgrn_bwd, cache_attn_k, cache_attn_v, c, c_ctx,
           w_mod, b_mod, norm_g, w_ff1, w_ff2, hg_w_in, hg_w_out, hg_g_norm, hg_lb_logits,
           da_w_in, da_w_out, da_subln_g, da_lambda, final_norm_g):
    batch, seq, d = x_prompt.shape
    dec_batch, dec_seq, _ = x_sample.shape
    depth = w_mod.shape[0]
    _, _, hg_heads, hg_dk, hg_dv = state_hgrn_fwd.shape
    _, _, past_len, da_halves, da_hd = cache_attn_k.shape
    da_heads = da_halves // 2
    ctx_rows = batch * seq
    lat_rows = dec_batch * dec_seq
    groups = _Groups(ctx_rows, dec_batch, dec_seq)

    x = (x_prompt.reshape(ctx_rows, d), x_sample.reshape(lat_rows, d))

    n_groups = 1 + dec_batch
    g8 = -(-n_groups // V7X_SUBLANES) * V7X_SUBLANES
    cond = jnp.concatenate([c_ctx[None, :], c, jnp.zeros((g8 - n_groups, d), F32)], axis=0)
    mod = _modulation(cond, w_mod, b_mod).reshape(depth, g8, 1, 6 * d)
    SH1, SC1, GT1, SH2, SC2, GT2 = range(6)

    def mixer_weights(layer):
        j = layer // N_MIXERS
        return (_Cast(hg_w_in, j), _Cast(hg_w_out, j)) if layer % N_MIXERS == 0 else \
               (_Cast(da_w_in, j), _Cast(da_w_out, j))

    w_in = _cast_weight(mixer_weights(0)[0])
    new_sf, new_sb, new_k, new_v = [], [], [], []
    for layer in range(depth):
        h = _norm_mod(x, norm_g[layer, 0], mod, layer, SC1, SH1, groups)
        j = layer // N_MIXERS
        proj, w_out, w1 = _matmul(h, w_in, F32, casts=(mixer_weights(layer)[1], _Cast(w_ff1, layer)))
        if layer % N_MIXERS == 0:
            scan = dict(groups=groups, batch=batch, seq=seq, n_heads=hg_heads, dk=hg_dk, dv=hg_dv, d=d,
                        layer=layer)
            o_fwd, s_fwd = _hg_scan(proj, hg_lb_logits[0], state_hgrn_fwd[:, j], f_seg=2, reverse=False, **scan)
            mixed, s_bwd = _hg_scan(proj, hg_lb_logits[1], state_hgrn_bwd[:, j], f_seg=3, reverse=True,
                                    o_fwd=o_fwd, g_norm=hg_g_norm[j], g_seg=4, **scan)
            new_sf.append(s_fwd)
            new_sb.append(s_bwd)
        else:
            lam_init = 0.8 - 0.6 * math.exp(-0.3 * layer)
            att = dict(n_heads=da_heads, hd=da_hd, lam_init=lam_init)
            mixed, k_ctx, v_ctx = _diff_attention(proj, da_lambda[j], da_subln_g[j], batch=batch, seq=seq,
                                                  row_off=0, prev=jnp.zeros((groups.rows, d), BF16), **att)
            new_k.append(k_ctx.reshape(batch, seq, da_halves, da_hd))
            new_v.append(v_ctx.reshape(batch, seq, da_heads, 2 * da_hd))
            mixed = _diff_attention(proj, da_lambda[j], da_subln_g[j], batch=dec_batch, seq=dec_seq,
                                    row_off=ctx_rows,
                                    cache_k=cache_attn_k[:, j].reshape(dec_batch * past_len, d),
                                    cache_v=cache_attn_v[:, j].reshape(dec_batch * past_len, d),
                                    prev=mixed, **att)
        x = _matmul(mixed, w_out, F32, "residual", res=x, mod=mod, layer=layer, gate_chunk=GT1, groups=groups)
        h = _norm_mod(x, norm_g[layer, 1], mod, layer, SC2, SH2, groups)
        casts = (_Cast(w_ff2, layer),) + (mixer_weights(layer + 1)[:1] if layer + 1 < depth else ())
        u, w2, *w_next = _matmul(h, w1, BF16, "relu2", casts=casts)
        w_in = w_next[0] if w_next else None
        x = _matmul(u, w2, F32, "residual", res=x, mod=mod, layer=layer, gate_chunk=GT2, groups=groups)

    y_prompt = _final_norm(x, final_norm_g, 0, ctx_rows).reshape(batch, seq, d)
    y_sample = _final_norm(x, final_norm_g, ctx_rows, lat_rows).reshape(dec_batch, dec_seq, d)
    return (y_prompt, y_sample, jnp.stack(new_sf, axis=1), jnp.stack(new_sb, axis=1),
            jnp.stack(new_k, axis=1), jnp.stack(new_v, axis=1))
```

```python
import functools
import math
from typing import NamedTuple

import jax
import jax.numpy as jnp
from jax import lax
from jax.experimental import pallas as pl
from jax.experimental.pallas import tpu as pltpu

F32 = jnp.float32
BF16 = jnp.bfloat16

GRID_W = 64
EPS = 1e-6
ROPE_BASE = 10000.0
N_MIXERS = 2

V7X_SUBLANES = 8
V7X_LANES = 128
V7X_VMEM_BYTES = 64 * 1024 * 1024
V7X_VMEM_LIMIT_BYTES = V7X_VMEM_BYTES - 4 * 1024 * 1024

MM_BM = 1024
MM_BN = 1024
MM_VMEM_BUDGET_BYTES = 52 * 1024 * 1024
CAST_VMEM_BUDGET_BYTES = 10 * 1024 * 1024
CAST_TILE_COLS = 1024
ROW_TILE = 512
MOD_BN = 512
HG_CHUNK = 128
HG_HEADS_PER_STEP = 16
HG_LOOKAHEAD = 1
ATTN_TQ = 1024
ATTN_SUB_TQ = 256
ATTN_CTX_HEADS_PER_STEP = 8

LOG2E = 1.4426950408889634


def _tile(full, want):
    t = min(full, want)
    assert full % t == 0, (full, want)
    return t


def _params(sem):
    return pltpu.CompilerParams(dimension_semantics=sem, vmem_limit_bytes=V7X_VMEM_LIMIT_BYTES)


def _sigmoid(x):
    return 0.5 * jnp.tanh(0.5 * x) + 0.5


def _silu(x):
    return x * _sigmoid(x)


def _dot(a, b):
    return jnp.dot(a, b, preferred_element_type=F32)


def _dot_nt(a, b):
    return lax.dot_general(a, b, (((1,), (1,)), ((), ())), preferred_element_type=F32)


def _dot_tn(a, b):
    return lax.dot_general(a, b, (((0,), (0,)), ((), ())), preferred_element_type=F32)


class _Groups:
    def __init__(self, ctx_rows, dec_batch, dec_seq):
        self.ctx_rows = ctx_rows
        self.dec_batch = dec_batch
        self.dec_seq = dec_seq
        self.rows = ctx_rows + dec_batch * dec_seq

    def of_tile(self, i, tile):
        assert self.ctx_rows % tile == 0 and self.dec_seq % tile == 0
        g = jnp.int32(0)
        for b in range(self.dec_batch):
            g = g + (i >= (self.ctx_rows + b * self.dec_seq) // tile).astype(jnp.int32)
        return g


def _mod_kernel(c_ref, w_ref, b_ref, o_ref):
    a = _silu(c_ref[...]).astype(BF16)
    o_ref[...] = _dot(a, w_ref[...].astype(BF16)) + b_ref[...]


def _modulation(cond, w_mod, b_mod):
    g8, d = cond.shape
    depth, _, n = w_mod.shape
    bn = _tile(n, MOD_BN)
    return pl.pallas_call(
        _mod_kernel,
        grid=(depth, n // bn),
        in_specs=[
            pl.BlockSpec((g8, d), lambda l, j: (0, 0)),
            pl.BlockSpec((None, d, bn), lambda l, j: (l, 0, j)),
            pl.BlockSpec((None, 1, bn), lambda l, j: (l, 0, j)),
        ],
        out_specs=pl.BlockSpec((None, g8, bn), lambda l, j: (l, 0, j)),
        out_shape=jax.ShapeDtypeStruct((depth, g8, n), F32),
        compiler_params=_params(("arbitrary", "arbitrary")),
        name="adaln_modulation",
    )(cond, w_mod, b_mod.reshape(depth, 1, n))


def _norm_kernel(*refs, modulated, x_split):
    if modulated:
        *x_refs, g_ref, sc_ref, sh_ref, o_ref = refs
    else:
        *x_refs, g_ref, o_ref = refs
    x = x_refs[0][...]
    if len(x_refs) == 2:
        x = jnp.where(pl.program_id(0) < x_split, x, x_refs[1][...])
    y = x * lax.rsqrt(jnp.mean(x * x, axis=-1, keepdims=True) + EPS) * g_ref[...]
    if modulated:
        y = y * (1.0 + sc_ref[...]) + sh_ref[...]
    o_ref[...] = y.astype(o_ref.dtype)


def _norm_mod(x, g, mod, layer, sc_chunk, sh_chunk, groups):
    rows, d = groups.rows, g.shape[0]
    tm = _tile(math.gcd(groups.ctx_rows, groups.dec_seq), ROW_TILE)

    def vec(chunk):
        return pl.BlockSpec((None, None, 1, d), lambda i: (layer, groups.of_tile(i, tm), 0, chunk))

    if isinstance(x, tuple):
        x_split = x[0].shape[0] // tm
        x_specs = [pl.BlockSpec((tm, d), lambda i: (jnp.minimum(i, x_split - 1), 0)),
                   pl.BlockSpec((tm, d), lambda i: (jnp.maximum(i - x_split, 0), 0))]
        xs = list(x)
    else:
        x_split = 0
        x_specs = [pl.BlockSpec((tm, d), lambda i: (i, 0))]
        xs = [x]
    return pl.pallas_call(
        functools.partial(_norm_kernel, modulated=True, x_split=x_split),
        grid=(rows // tm,),
        in_specs=x_specs + [
            pl.BlockSpec((1, d), lambda i: (0, 0)),
            vec(sc_chunk),
            vec(sh_chunk),
        ],
        out_specs=pl.BlockSpec((tm, d), lambda i: (i, 0)),
        out_shape=jax.ShapeDtypeStruct((rows, d), BF16),
        compiler_params=_params(("parallel",)),
        name="rmsnorm_modulate",
    )(*xs, g.reshape(1, d), mod, mod)


def _final_norm(x, g, row_off, rows):
    d = x.shape[1]
    tm = _tile(math.gcd(rows, row_off) if row_off else rows, ROW_TILE)
    off = row_off // tm
    return pl.pallas_call(
        functools.partial(_norm_kernel, modulated=False, x_split=0),
        grid=(rows // tm,),
        in_specs=[
            pl.BlockSpec((tm, d), lambda i: (i + off, 0)),
            pl.BlockSpec((1, d), lambda i: (0, 0)),
        ],
        out_specs=pl.BlockSpec((tm, d), lambda i: (i, 0)),
        out_shape=jax.ShapeDtypeStruct((rows, d), F32),
        compiler_params=_params(("parallel",)),
        name="final_rmsnorm",
    )(x, g.reshape(1, d))


def _mm_kernel(*refs, nk, epilogue, n_res, res_split, n_casts):
    x_ref, w_ref = refs[:2]
    n_in = 2 + (n_res + 1 if epilogue == "residual" else 0)
    res_refs = refs[2:2 + n_res]
    gate_ref = refs[n_in - 1]
    o_ref = refs[n_in + n_casts]
    for src_ref, dst_ref in zip(refs[n_in:n_in + n_casts], refs[n_in + n_casts + 1:]):
        dst_ref[...] = src_ref[...].astype(dst_ref.dtype)

    def finish(acc):
        if epilogue == "relu2":
            r = jnp.maximum(acc, 0.0)
            acc = r * r
        elif epilogue == "residual":
            res = res_refs[0][...]
            if len(res_refs) == 2:
                res = jnp.where(pl.program_id(0) < res_split, res, res_refs[1][...])
            acc = res + gate_ref[...] * acc
        return acc

    def product():
        return _dot(x_ref[...], w_ref[...])

    if nk == 1:
        o_ref[...] = finish(product()).astype(o_ref.dtype)
        return

    k = pl.program_id(2)

    @pl.when(k == 0)
    def _():
        o_ref[...] = product()

    @pl.when(jnp.logical_and(k > 0, k < nk - 1))
    def _():
        o_ref[...] += product()

    @pl.when(k == nk - 1)
    def _():
        o_ref[...] = finish(o_ref[...] + product())


class _Cast(NamedTuple):
    src: jax.Array
    lead: int
    col0: int = 0
    ncols: int = 0

    @property
    def shape(self):
        return self.src.shape[1], self.ncols or self.src.shape[2]


def _cast_kernel(s_ref, o_ref):
    o_ref[...] = s_ref[...].astype(o_ref.dtype)


def _cast_weight(c):
    rows, cols = c.shape
    tr, tc = _tile(rows, CAST_TILE_COLS), _tile(cols, 2 * CAST_TILE_COLS)
    assert c.col0 % tc == 0
    j0 = c.col0 // tc
    return pl.pallas_call(
        _cast_kernel,
        grid=(rows // tr, cols // tc),
        in_specs=[pl.BlockSpec((None, tr, tc), lambda i, j: (c.lead, i, j0 + j))],
        out_specs=pl.BlockSpec((tr, tc), lambda i, j: (i, j)),
        out_shape=jax.ShapeDtypeStruct((rows, cols), BF16),
        compiler_params=_params(("parallel", "parallel")),
        name="cast_weight",
    )(c.src)


def _matmul(x, w, out_dtype, epilogue="plain", *, res=None, mod=None, layer=None, gate_chunk=None,
            groups=None, casts=()):
    m, kdim = x.shape
    n = w.shape[1]
    bm = _tile(m, MM_BM)
    n_res = 0 if epilogue != "residual" else (2 if isinstance(res, tuple) else 1)
    out_bytes = jnp.dtype(out_dtype).itemsize + 4 * n_res

    def fits(bn, bk):
        return 2 * (2 * bk * (bm + bn) + bm * bn * out_bytes) + 4 * bm * bn <= MM_VMEM_BUDGET_BYTES

    bn_full = _tile(n, MM_BN)
    candidates = [(bn_full, kdim), (max(bn_full // 2, V7X_LANES), kdim)]
    candidates += [(bn_full, kdim // s) for s in (2, 4, 8, 16, 32)]
    bn, bk = next((bn_, bk_) for bn_, bk_ in candidates if kdim % bk_ == 0 and fits(bn_, bk_))
    nk = kdim // bk
    nj = n // bn
    assert nk == 1 or out_dtype == F32
    in_specs = [
        pl.BlockSpec((bm, bk), lambda i, j, k: (i, k)),
        pl.BlockSpec((bk, bn), lambda i, j, k: (k, j)),
    ]
    args = [x, w]
    res_split = 0
    if epilogue == "residual":
        nb = n // bn
        if isinstance(res, tuple):
            res_a, res_b = res
            res_split = res_a.shape[0] // bm
            assert res_a.shape[0] % bm == 0 and res_a.shape[0] + res_b.shape[0] == m
            in_specs += [
                pl.BlockSpec((bm, bn), lambda i, j, k: (jnp.minimum(i, res_split - 1),
                                                        jnp.where(i < res_split, j, nb - 1))),
                pl.BlockSpec((bm, bn), lambda i, j, k: (jnp.maximum(i - res_split, 0),
                                                        jnp.where(i < res_split, 0, j))),
            ]
            args += [res_a, res_b]
        else:
            in_specs.append(pl.BlockSpec((bm, bn), lambda i, j, k: (i, j)))
            args.append(res)
        in_specs.append(pl.BlockSpec((None, None, 1, bn),
                                     lambda i, j, k: (layer, groups.of_tile(i, bm), 0, gate_chunk * nb + j)))
        args.append(mod)

    out_specs = [pl.BlockSpec((bm, bn), lambda i, j, k: (i, j))]
    out_shape = [jax.ShapeDtypeStruct((m, n), out_dtype)]
    n_steps = (m // bm) * nj * nk
    cast_bytes = 0
    for c in casts:
        rows, cols = c.shape
        assert cols % CAST_TILE_COLS == 0 and c.col0 % CAST_TILE_COLS == 0
        tiles_c = cols // CAST_TILE_COLS
        tile_rows = next(tr for tr in (128, 256, 512, 1024, 2048)
                         if rows % tr == 0 and (rows // tr) * tiles_c <= n_steps)
        n_tiles = (rows // tile_rows) * tiles_c
        cast_bytes += 2 * (4 + 2) * tile_rows * CAST_TILE_COLS
        assert cast_bytes <= CAST_VMEM_BUDGET_BYTES

        def tile_of(i, j, k, n_tiles=n_tiles, tiles_c=tiles_c):
            t = jnp.minimum((i * nj + j) * nk + k, n_tiles - 1)
            return t // tiles_c, t % tiles_c

        def src_tile_of(i, j, k, c=c, tile_of=tile_of):
            r, t = tile_of(i, j, k)
            return c.lead, r, c.col0 // CAST_TILE_COLS + t

        in_specs.append(pl.BlockSpec((None, tile_rows, CAST_TILE_COLS), src_tile_of))
        args.append(c.src)
        out_specs.append(pl.BlockSpec((tile_rows, CAST_TILE_COLS), tile_of))
        out_shape.append(jax.ShapeDtypeStruct((rows, cols), BF16))

    outs = pl.pallas_call(
        functools.partial(_mm_kernel, nk=nk, epilogue=epilogue, n_res=n_res, res_split=res_split,
                          n_casts=len(casts)),
        grid=(m // bm, nj, nk),
        in_specs=in_specs,
        out_specs=out_specs,
        out_shape=out_shape,
        compiler_params=_params(("arbitrary", "arbitrary", "arbitrary")),
        name="matmul_" + epilogue,
    )(*args)
    return outs[0] if not casts else outs


class _ScanLayout(NamedTuple):
    n_ctx: int
    per_ctx: int
    per_lat: int
    total: int
    batch: int
    reverse: bool

    def pos(self, g):
        r = self.total - 1 - g if self.reverse else g
        is_lat = r >= self.n_ctx
        rl = jnp.maximum(r - self.n_ctx, 0)
        seq_ctx = jnp.minimum(r // self.per_ctx, self.batch - 1)
        seq_lat = rl // self.per_lat
        pos = jnp.where(is_lat, rl % self.per_lat, r % self.per_ctx)
        per = jnp.where(is_lat, self.per_lat, self.per_ctx)
        start, stop = (per - 1, 0) if self.reverse else (0, per - 1)
        return r, is_lat, seq_ctx, seq_lat, pos == start, pos == stop


def _level_ref(c, half, reverse):
    rows, width = c.shape
    blk = 2 * half
    idx = half if reverse else half - 1
    assert blk % V7X_SUBLANES == 0
    c3 = c.reshape(rows // blk, blk, width)
    ref = jnp.broadcast_to(c3[:, idx:idx + 1, :], c3.shape)
    return ref.reshape(rows, width)


def _exp2_neg_abs(d):
    bits = lax.bitcast_convert_type(d, jnp.uint32) | jnp.uint32(0x80000000)
    return jnp.exp2(lax.bitcast_convert_type(bits, F32))


def _level_weights(c, fg, level, reverse):
    rows = c.shape[0]
    if level > 2:
        w = _exp2_neg_abs(c - _level_ref(c, 1 << (level - 1), reverse))
        return w, w
    if level == 1:
        return fg, None
    pos = lax.broadcasted_iota(jnp.int32, c.shape, 0) & 3
    prev = pltpu.roll(fg, 1, 0)
    nxt = pltpu.roll(fg, rows - 1, 0)
    if reverse:
        return fg * jnp.where(pos == 0, nxt, 1.0), jnp.where(pos == 3, prev, 1.0)
    return fg * jnp.where(pos == 3, prev, 1.0), jnp.where(pos == 0, nxt, 1.0)


def _hg_scan_kernel(*refs, chunk, heads, dk, dv, layer, lay, fuse_out):
    if fuse_out:
        (q_ref, v_ref, f_ref, lbl_ref, s0_ref, ofwd_ref, g_ref, gn_ref,
         o_ref, sfin_ref, st_ref) = refs
    else:
        q_ref, v_ref, f_ref, lbl_ref, s0_ref, o_ref, sfin_ref, st_ref = refs
    reverse = lay.reverse
    _, is_lat, _, _, first, last = lay.pos(pl.program_id(1))
    n_levels = chunk.bit_length() - 1
    low_levels = min(n_levels, V7X_SUBLANES.bit_length() - 1)

    @pl.when(jnp.logical_and(first, is_lat))
    def _():
        for j in range(heads):
            st_ref[j] = s0_ref[0, j].T

    @pl.when(jnp.logical_and(first, jnp.logical_not(is_lat)))
    def _():
        st_ref[...] = jnp.zeros(st_ref.shape, F32)

    lbl = lbl_ref[...]
    e = jnp.exp(lbl - jnp.max(lbl, axis=0, keepdims=True))
    lb = jnp.sum(e[:layer + 1], axis=0, keepdims=True) / jnp.sum(e, axis=0, keepdims=True)

    fg_all = lb + (1.0 - lb) * _sigmoid(f_ref[...])
    k_all = 1.0 - fg_all
    log2f = jnp.log2(fg_all)

    row = lax.broadcasted_iota(jnp.int32, (chunk, chunk), 0)
    col = lax.broadcasted_iota(jnp.int32, (chunk, chunk), 1)
    before = (col > row) if reverse else (col < row)
    tri = jnp.logical_or(before, col == row).astype(BF16)
    hi = log2f.astype(BF16)
    r1 = log2f - hi.astype(F32)
    mid = r1.astype(BF16)
    lo = (r1 - mid.astype(F32)).astype(BF16)
    c_all = _dot(tri, hi) + _dot(tri, mid) + _dot(tri, lo)

    x = row ^ col
    lvl = jnp.zeros((chunk, chunk), jnp.int32)
    for b in range(n_levels):
        lvl = lvl + (x >= (1 << b)).astype(jnp.int32)
    lvl = jnp.where(jnp.logical_or(before, x == 0), lvl, -1)

    end = 0 if reverse else chunk - 1
    if fuse_out:
        gn = gn_ref[...]

    def pair_weights(j):
        sl = slice(j * dk, (j + 1) * dk)
        q = _silu(q_ref[:, sl])
        k = k_all[:, sl]
        c = c_all[:, sl]
        fg = fg_all[:, sl]
        q_bf, k_bf = q.astype(BF16), k.astype(BF16)
        a = jnp.where(lvl == 0, _dot_nt(q_bf, k_bf), 0.0)
        for level in range(1, low_levels + 1):
            w_q, w_k = _level_weights(c, fg, level, reverse)
            a_l = _dot_nt((q * w_q).astype(BF16), k_bf if w_k is None else (k * w_k).astype(BF16))
            a = jnp.where(lvl == level, a_l, a)
        for level in range(low_levels + 1, n_levels + 1):
            half = 1 << (level - 1)
            w, _ = _level_weights(c, fg, level, reverse)
            q_rows, k_rows = [], []
            for b0 in range(0, chunk, 2 * half):
                early, late = (b0 + half, b0) if reverse else (b0, b0 + half)
                q_rows.append(q[late:late + half] * w[late:late + half])
                k_scaled = k[early:early + half] * w[early:early + half]
                k_rows += [k[late:late + half], k_scaled] if reverse else [k_scaled, k[late:late + half]]
            p = _dot_nt(jnp.concatenate(q_rows, axis=0).astype(BF16),
                        jnp.concatenate(k_rows, axis=0).astype(BF16))
            pieces = []
            for b, b0 in enumerate(range(0, chunk, 2 * half)):
                early, late = (b0 + half, b0) if reverse else (b0, b0 + half)
                upd = jnp.where(lvl[late:late + half] == level, p[b * half:(b + 1) * half], a[late:late + half])
                pieces += [upd, a[early:early + half]] if reverse else [a[early:early + half], upd]
            a = jnp.concatenate(pieces, axis=0)
        return q, k, c, a

    def outputs_and_state(j, q, k, c, a):
        sv = slice(j * dv, (j + 1) * dv)
        v = v_ref[:, sv].astype(BF16)
        c_end = c[end:end + 1, :]
        st = st_ref[j]
        o = _dot(a.astype(BF16), v) + _dot_nt((q * jnp.exp2(c)).astype(BF16), st.astype(BF16))
        if fuse_out:
            o = o + ofwd_ref[:, sv]
            y = o * lax.rsqrt(jnp.mean(o * o, axis=-1, keepdims=True) + EPS) * gn
            o = y * _silu(g_ref[:, sv])
        o_ref[:, sv] = o.astype(o_ref.dtype)
        k_dec = (k * jnp.exp2(c_end - c)).astype(BF16)
        st_ref[j] = st * jnp.exp2(c_end) + _dot_tn(v, k_dec)

    pending = [pair_weights(j) for j in range(min(HG_LOOKAHEAD, heads))]
    for j in range(heads):
        if j + HG_LOOKAHEAD < heads:
            pending.append(pair_weights(j + HG_LOOKAHEAD))
        outputs_and_state(j, *pending.pop(0))

    @pl.when(jnp.logical_and(last, jnp.logical_not(is_lat)))
    def _():
        for j in range(heads):
            sfin_ref[0, j] = st_ref[j].T


def _hg_scan(proj_qv, proj_fg, lb_logits_dir, s0_lat, *, groups, batch, seq, n_heads, dk, dv, d, reverse, layer,
             o_fwd=None, g_norm=None):
    chunk = _tile(math.gcd(seq, groups.dec_seq), HG_CHUNK)
    heads = _tile(n_heads, HG_HEADS_PER_STEP)
    assert dk == dv and d == n_heads * dk
    w = heads * dk
    segs = d // w
    lay = _ScanLayout(n_ctx=groups.ctx_rows // chunk, per_ctx=seq // chunk, per_lat=groups.dec_seq // chunk,
                      total=groups.rows // chunk, batch=batch, reverse=reverse)
    fuse_out = o_fwd is not None

    def seg_spec(seg):
        return pl.BlockSpec((chunk, w), lambda h, g: (lay.pos(g)[0], seg * segs + h))

    n_lb = lb_logits_dir.shape[0]
    in_specs = [seg_spec(0), seg_spec(1), seg_spec(1 if reverse else 0),
                pl.BlockSpec((n_lb, w), lambda h, g: (0, h)),
                pl.BlockSpec((1, heads, dk, dv), lambda h, g: (lay.pos(g)[3], h, 0, 0))]
    args = [proj_qv, proj_qv, proj_fg, lb_logits_dir, s0_lat]
    if fuse_out:
        in_specs += [seg_spec(0), seg_spec(2), pl.BlockSpec((1, dv), lambda h, g: (0, 0))]
        args += [o_fwd, proj_fg, g_norm.reshape(1, dv)]
    o, s_fin = pl.pallas_call(
        functools.partial(_hg_scan_kernel, chunk=chunk, heads=heads, dk=dk, dv=dv,
                          layer=layer, lay=lay, fuse_out=fuse_out),
        grid=(n_heads // heads, lay.total),
        in_specs=in_specs,
        out_specs=[
            pl.BlockSpec((chunk, w), lambda h, g: (lay.pos(g)[0], h)),
            pl.BlockSpec((1, heads, dk, dv), lambda h, g: (lay.pos(g)[2], h, 0, 0)),
        ],
        out_shape=[
            jax.ShapeDtypeStruct((groups.rows, d), BF16 if fuse_out else F32),
            jax.ShapeDtypeStruct((batch, n_heads, dk, dv), F32),
        ],
        scratch_shapes=[pltpu.VMEM((heads, dv, dk), F32)],
        compiler_params=_params(("parallel", "arbitrary")),
        name="hgrn2_scan_" + ("bwd" if reverse else "fwd"),
    )(*args)
    return o, s_fin


def _rope_tables(seq, hd):
    half = hd // 2
    rows = seq // GRID_W
    r_pos = jnp.repeat(jnp.arange(rows), GRID_W).astype(F32)
    c_pos = jnp.tile(jnp.arange(GRID_W), rows).astype(F32)
    inv_freq = ROPE_BASE ** (-jnp.arange(0, half, 2, dtype=F32) / half)
    ang_r = r_pos[:, None] * inv_freq
    ang_c = c_pos[:, None] * inv_freq
    cos = jnp.concatenate([jnp.cos(ang_r)] * 2 + [jnp.cos(ang_c)] * 2, axis=-1)
    sin = jnp.concatenate([-jnp.sin(ang_r), jnp.sin(ang_r), -jnp.sin(ang_c), jnp.sin(ang_c)], axis=-1)
    return cos, sin


def _rope_tile(x, cos, sin):
    hd = x.shape[1]
    quarter = hd // 4
    lane = lax.broadcasted_iota(jnp.int32, x.shape, 1)
    first = (lane % (2 * quarter)) < quarter
    partner = jnp.where(first, pltpu.roll(x, hd - quarter, 1), pltpu.roll(x, quarter, 1))
    return x * cos + partner * sin


def _attn_kernel(*refs, hd, heads, tq, sub, lookahead, past, scale, lam_init, latent):
    if latent:
        (lam_ref, q_ref, k_ref, v_ref, kc_ref, vc_ref, cos_ref, sin_ref, g_ref, _,
         o_ref, kall_ref, vall_ref) = refs
    else:
        lam_ref, q_ref, k_ref, v_ref, g_ref, _, o_ref, knew_ref, vnew_ref = refs
        knew_ref[...] = k_ref[...]
        vnew_ref[...] = v_ref[...]
    w = 2 * hd
    lm = lam_ref[...]
    lam = (jnp.exp(jnp.sum(lm[0:1] * lm[1:2], axis=-1, keepdims=True))
           - jnp.exp(jnp.sum(lm[2:3] * lm[3:4], axis=-1, keepdims=True)) + lam_init)
    g = g_ref[...]

    if latent:
        i = pl.program_id(2)

        @pl.when(i == 0)
        def _():
            kall_ref[0:past, :] = kc_ref[...].astype(BF16)
            vall_ref[0:past, :] = vc_ref[...].astype(BF16)
            vall_ref[past:, :] = v_ref[...].astype(BF16)
            for half in range(2):
                hs = slice(half * hd, (half + 1) * hd)
                kall_ref[past:, hs] = _rope_tile(k_ref[:, hs], cos_ref[...], sin_ref[...]).astype(BF16)

    def softmax_parts(qh, kh):
        s = _dot_nt(qh, kh)
        p = jnp.exp2(s - jnp.max(s, axis=-1, keepdims=True))
        return p.astype(BF16), 1.0 / jnp.sum(p, axis=-1, keepdims=True)

    units = [(j, r0) for j in range(heads) for r0 in range(0, tq, sub)]

    def probabilities(j, r0):
        parts = []
        for half in range(2):
            hs = slice(j * w + half * hd, j * w + (half + 1) * hd)
            qh = q_ref[r0:r0 + sub, hs]
            if latent:
                rows = pl.ds(pl.multiple_of(i * tq + r0, sub), sub)
                qh = _rope_tile(qh, cos_ref[rows, :], sin_ref[rows, :])
                kh = kall_ref[:, hs]
            else:
                kh = k_ref[:, hs].astype(BF16)
            parts.append(softmax_parts((qh * (scale * LOG2E)).astype(BF16), kh))
        return parts

    def output(j, r0, parts):
        (p1, r1), (p2, r2) = parts
        js = slice(j * w, (j + 1) * w)
        vj = vall_ref[...] if latent else v_ref[:, js].astype(BF16)
        o = r1 * _dot(p1, vj) - (lam * r2) * _dot(p2, vj)
        y = o * lax.rsqrt(jnp.mean(o * o, axis=-1, keepdims=True) + EPS) * g
        o_ref[r0:r0 + sub, js] = (y * (1.0 - lam_init)).astype(o_ref.dtype)

    pending = [probabilities(*u) for u in units[:lookahead]]
    for n, unit in enumerate(units):
        if n + lookahead < len(units):
            pending.append(probabilities(*units[n + lookahead]))
        output(*unit, pending.pop(0))


def _diff_attention(proj, lam_params, subln_g, *, batch, seq, row_off, n_heads, hd, lam_init,
                    cache_k=None, cache_v=None, prev=None):
    latent = cache_k is not None
    w = 2 * hd
    d = n_heads * w
    tq = _tile(seq, ATTN_TQ)
    nq = seq // tq
    heads = 1 if latent else _tile(n_heads, ATTN_CTX_HEADS_PER_STEP)
    assert row_off % seq == 0
    q_off, kv_off = row_off // tq, row_off // seq
    segs = n_heads // heads
    scale = hd ** -0.5
    past = cache_k.shape[0] // batch if latent else 0

    def kv_spec(rows, seg=0, off=0):
        return pl.BlockSpec((rows, heads * w), lambda b, h, i: (off + b, seg * segs + h))

    in_specs = [pl.BlockSpec(lam_params.shape, lambda b, h, i: (0, 0)),
                pl.BlockSpec((tq, heads * w), lambda b, h, i: (q_off + b * nq + i, h)),
                kv_spec(seq, 1, kv_off), kv_spec(seq, 2, kv_off)]
    args = [lam_params, proj, proj, proj]
    scratch = []
    if latent:
        cos, sin = _rope_tables(seq, hd)
        table = pl.BlockSpec((seq, hd), lambda b, h, i: (0, 0))
        in_specs += [kv_spec(past), kv_spec(past), table, table]
        args += [cache_k, cache_v, cos, sin]
        scratch = [pltpu.VMEM((past + seq, w), BF16), pltpu.VMEM((past + seq, w), BF16)]
    in_specs.append(pl.BlockSpec((1, w), lambda b, h, i: (0, 0)))
    args.append(subln_g.reshape(1, w))
    out_specs = [pl.BlockSpec((tq, heads * w), lambda b, h, i: (q_off + b * nq + i, h))]
    out_shape = [jax.ShapeDtypeStruct((proj.shape[0], d), BF16)]
    in_specs.append(pl.BlockSpec(memory_space=pl.ANY))
    args.append(prev)
    aliases = {len(args) - 1: 0}
    if not latent:
        out_specs += [kv_spec(seq), kv_spec(seq)]
        out_shape += [jax.ShapeDtypeStruct((batch * seq, d), F32)] * 2
    outs = pl.pallas_call(
        functools.partial(_attn_kernel, hd=hd, heads=heads, tq=tq, sub=_tile(tq, ATTN_SUB_TQ),
                          lookahead=1 if latent else 0, past=past, scale=scale, lam_init=lam_init,
                          latent=latent),
        grid=(batch, n_heads // heads, nq),
        in_specs=in_specs,
        out_specs=out_specs,
        out_shape=out_shape,
        scratch_shapes=scratch,
        input_output_aliases=aliases,
        compiler_params=_params(("parallel", "parallel", "arbitrary")),
        name="diff_attention_" + ("latent" if latent else "context"),
    )(*args)
    return outs[0] if latent else outs


def kernel(x_prompt, x_sample, state_hgrn_fwd, state_hgrn_bwd, cache_attn_k, cache_attn_v, c, c_ctx,
           w_mod, b_mod, norm_g, w_ff1, w_ff2, hg_w_in, hg_w_out, hg_g_norm, hg_lb_logits,
           da_w_in, da_w_out, da_subln_g, da_lambda, final_norm_g):
    batch, seq, d = x_prompt.shape
    dec_batch, dec_seq, _ = x_sample.shape
    depth = w_mod.shape[0]
    _, _, hg_heads, hg_dk, hg_dv = state_hgrn_fwd.shape
    _, _, past_len, da_halves, da_hd = cache_attn_k.shape
    da_heads = da_halves // 2
    ctx_rows = batch * seq
    lat_rows = dec_batch * dec_seq
    groups = _Groups(ctx_rows, dec_batch, dec_seq)

    x = (x_prompt.reshape(ctx_rows, d), x_sample.reshape(lat_rows, d))

    n_groups = 1 + dec_batch
    g8 = -(-n_groups // V7X_SUBLANES) * V7X_SUBLANES
    cond = jnp.concatenate([c_ctx[None, :], c, jnp.zeros((g8 - n_groups, d), F32)], axis=0)
    mod = _modulation(cond, w_mod, b_mod).reshape(depth, g8, 1, 6 * d)
    SH1, SC1, GT1, SH2, SC2, GT2 = range(6)

    def mixer_weights(layer):
        j = layer // N_MIXERS
        if layer % N_MIXERS == 0:
            return [_Cast(hg_w_in, j, 0, 2 * d), _Cast(hg_w_in, j, 2 * d, 3 * d)], _Cast(hg_w_out, j)
        return [_Cast(da_w_in, j)], _Cast(da_w_out, j)

    first_parts, _ = mixer_weights(0)
    w_in = [_cast_weight(first_parts[0])] + first_parts[1:]
    new_sf, new_sb, new_k, new_v = [], [], [], []
    for layer in range(depth):
        h = _norm_mod(x, norm_g[layer, 0], mod, layer, SC1, SH1, groups)
        j = layer // N_MIXERS
        out_casts = (mixer_weights(layer)[1], _Cast(w_ff1, layer))
        projs = []
        for n in range(len(w_in)):
            last = n + 1 == len(w_in)
            casts = out_casts if last else tuple(w for w in w_in[n + 1:n + 2] if isinstance(w, _Cast))
            out = _matmul(h, w_in[n], F32, casts=casts)
            if last:
                out, w_out, w1 = out
            elif casts:
                out, w_in[n + 1] = out
            projs.append(out)
        if layer % N_MIXERS == 0:
            scan = dict(groups=groups, batch=batch, seq=seq, n_heads=hg_heads, dk=hg_dk, dv=hg_dv, d=d,
                        layer=layer)
            o_fwd, s_fwd = _hg_scan(*projs, hg_lb_logits[0], state_hgrn_fwd[:, j], reverse=False, **scan)
            mixed, s_bwd = _hg_scan(*projs, hg_lb_logits[1], state_hgrn_bwd[:, j], reverse=True,
                                    o_fwd=o_fwd, g_norm=hg_g_norm[j], **scan)
            new_sf.append(s_fwd)
            new_sb.append(s_bwd)
        else:
            proj, = projs
            lam_init = 0.8 - 0.6 * math.exp(-0.3 * layer)
            att = dict(n_heads=da_heads, hd=da_hd, lam_init=lam_init)
            mixed, k_ctx, v_ctx = _diff_attention(proj, da_lambda[j], da_subln_g[j], batch=batch, seq=seq,
                                                  row_off=0, prev=jnp.zeros((groups.rows, d), BF16), **att)
            new_k.append(k_ctx.reshape(batch, seq, da_halves, da_hd))
            new_v.append(v_ctx.reshape(batch, seq, da_heads, 2 * da_hd))
            mixed = _diff_attention(proj, da_lambda[j], da_subln_g[j], batch=dec_batch, seq=dec_seq,
                                    row_off=ctx_rows,
                                    cache_k=cache_attn_k[:, j].reshape(dec_batch * past_len, d),
                                    cache_v=cache_attn_v[:, j].reshape(dec_batch * past_len, d),
                                    prev=mixed, **att)
        x = _matmul(mixed, w_out, F32, "residual", res=x, mod=mod, layer=layer, gate_chunk=GT1, groups=groups)
        h = _norm_mod(x, norm_g[layer, 1], mod, layer, SC2, SH2, groups)
        next_parts = mixer_weights(layer + 1)[0] if layer + 1 < depth else []
        u, w2, *w_in = _matmul(h, w1, BF16, "relu2", casts=(_Cast(w_ff2, layer), *next_parts))
        x = _matmul(u, w2, F32, "residual", res=x, mod=mod, layer=layer, gate_chunk=GT2, groups=groups)

    y_prompt = _final_norm(x, final_norm_g, 0, ctx_rows).reshape(batch, seq, d)
    y_sample = _final_norm(x, final_norm_g, ctx_rows, lat_rows).reshape(dec_batch, dec_seq, d)
    return (y_prompt, y_sample, jnp.stack(new_sf, axis=1), jnp.stack(new_sb, axis=1),
            jnp.stack(new_k, axis=1), jnp.stack(new_v, axis=1))
```

```python
import functools
import math
from typing import NamedTuple

import jax
import jax.numpy as jnp
from jax import lax
from jax.experimental import pallas as pl
from jax.experimental.pallas import tpu as pltpu

F32 = jnp.float32
BF16 = jnp.bfloat16

GRID_W = 64
EPS = 1e-6
ROPE_BASE = 10000.0
N_MIXERS = 2

V7X_SUBLANES = 8
V7X_LANES = 128
V7X_VMEM_BYTES = 64 * 1024 * 1024
V7X_VMEM_LIMIT_BYTES = V7X_VMEM_BYTES - 4 * 1024 * 1024

MM_BM = 1024
MM_BN = 1024
MM_VMEM_BUDGET_BYTES = 52 * 1024 * 1024
CAST_VMEM_BUDGET_BYTES = 10 * 1024 * 1024
CAST_TILE_COLS = 1024
ROW_TILE = 512
MOD_BN = 1024
HG_CHUNK = 128
HG_HEADS_PER_STEP = 32
HG_LOOKAHEAD = 1
ATTN_TQ = 1024
ATTN_SUB_TQ = 256
ATTN_CTX_HEADS_PER_STEP = 8

LOG2E = 1.4426950408889634


def _tile(full, want):
    t = min(full, want)
    assert full % t == 0, (full, want)
    return t


def _params(sem):
    return pltpu.CompilerParams(dimension_semantics=sem, vmem_limit_bytes=V7X_VMEM_LIMIT_BYTES)


def _sigmoid(x):
    return 0.5 * jnp.tanh(0.5 * x) + 0.5


def _silu(x):
    return x * _sigmoid(x)


def _dot(a, b):
    return jnp.dot(a, b, preferred_element_type=F32)


def _dot_nt(a, b):
    return lax.dot_general(a, b, (((1,), (1,)), ((), ())), preferred_element_type=F32)


def _dot_tn(a, b):
    return lax.dot_general(a, b, (((0,), (0,)), ((), ())), preferred_element_type=F32)


class _Groups:
    def __init__(self, ctx_rows, dec_batch, dec_seq):
        self.ctx_rows = ctx_rows
        self.dec_batch = dec_batch
        self.dec_seq = dec_seq
        self.rows = ctx_rows + dec_batch * dec_seq

    def of_tile(self, i, tile):
        assert self.ctx_rows % tile == 0 and self.dec_seq % tile == 0
        g = jnp.int32(0)
        for b in range(self.dec_batch):
            g = g + (i >= (self.ctx_rows + b * self.dec_seq) // tile).astype(jnp.int32)
        return g


def _mod_kernel(c_ref, w_ref, b_ref, o_ref):
    a = _silu(c_ref[...]).astype(BF16)
    o_ref[...] = _dot(a, w_ref[...].astype(BF16)) + b_ref[...]


def _modulation(cond, w_mod, b_mod):
    g8, d = cond.shape
    depth, _, n = w_mod.shape
    bn = _tile(n, MOD_BN)
    return pl.pallas_call(
        _mod_kernel,
        grid=(depth, n // bn),
        in_specs=[
            pl.BlockSpec((g8, d), lambda l, j: (0, 0)),
            pl.BlockSpec((None, d, bn), lambda l, j: (l, 0, j)),
            pl.BlockSpec((None, 1, bn), lambda l, j: (l, 0, j)),
        ],
        out_specs=pl.BlockSpec((None, g8, bn), lambda l, j: (l, 0, j)),
        out_shape=jax.ShapeDtypeStruct((depth, g8, n), F32),
        compiler_params=_params(("arbitrary", "arbitrary")),
        name="adaln_modulation",
    )(cond, w_mod, b_mod.reshape(depth, 1, n))


def _norm_kernel(*refs, modulated, x_split):
    if modulated:
        *x_refs, g_ref, sc_ref, sh_ref, o_ref = refs
    else:
        *x_refs, g_ref, o_ref = refs
    x = x_refs[0][...]
    if len(x_refs) == 2:
        x = jnp.where(pl.program_id(0) < x_split, x, x_refs[1][...])
    y = x * lax.rsqrt(jnp.mean(x * x, axis=-1, keepdims=True) + EPS) * g_ref[...]
    if modulated:
        y = y * (1.0 + sc_ref[...]) + sh_ref[...]
    o_ref[...] = y.astype(o_ref.dtype)


def _norm_mod(x, g, mod, layer, sc_chunk, sh_chunk, groups):
    rows, d = groups.rows, g.shape[0]
    tm = _tile(math.gcd(groups.ctx_rows, groups.dec_seq), ROW_TILE)

    def vec(chunk):
        return pl.BlockSpec((None, None, 1, d), lambda i: (layer, groups.of_tile(i, tm), 0, chunk))

    if isinstance(x, tuple):
        x_split = x[0].shape[0] // tm
        x_specs = [pl.BlockSpec((tm, d), lambda i: (jnp.minimum(i, x_split - 1), 0)),
                   pl.BlockSpec((tm, d), lambda i: (jnp.maximum(i - x_split, 0), 0))]
        xs = list(x)
    else:
        x_split = 0
        x_specs = [pl.BlockSpec((tm, d), lambda i: (i, 0))]
        xs = [x]
    return pl.pallas_call(
        functools.partial(_norm_kernel, modulated=True, x_split=x_split),
        grid=(rows // tm,),
        in_specs=x_specs + [
            pl.BlockSpec((1, d), lambda i: (0, 0)),
            vec(sc_chunk),
            vec(sh_chunk),
        ],
        out_specs=pl.BlockSpec((tm, d), lambda i: (i, 0)),
        out_shape=jax.ShapeDtypeStruct((rows, d), BF16),
        compiler_params=_params(("parallel",)),
        name="rmsnorm_modulate",
    )(*xs, g.reshape(1, d), mod, mod)


def _final_norm(x, g, row_off, rows):
    d = x.shape[1]
    tm = _tile(math.gcd(rows, row_off) if row_off else rows, ROW_TILE)
    off = row_off // tm
    return pl.pallas_call(
        functools.partial(_norm_kernel, modulated=False, x_split=0),
        grid=(rows // tm,),
        in_specs=[
            pl.BlockSpec((tm, d), lambda i: (i + off, 0)),
            pl.BlockSpec((1, d), lambda i: (0, 0)),
        ],
        out_specs=pl.BlockSpec((tm, d), lambda i: (i, 0)),
        out_shape=jax.ShapeDtypeStruct((rows, d), F32),
        compiler_params=_params(("parallel",)),
        name="final_rmsnorm",
    )(x, g.reshape(1, d))


def _mm_kernel(*refs, nk, epilogue, n_res, res_split, n_casts):
    x_ref, w_ref = refs[:2]
    n_in = 2 + (n_res + 1 if epilogue == "residual" else 0)
    res_refs = refs[2:2 + n_res]
    gate_ref = refs[n_in - 1]
    o_ref = refs[n_in + n_casts]
    for src_ref, dst_ref in zip(refs[n_in:n_in + n_casts], refs[n_in + n_casts + 1:]):
        dst_ref[...] = src_ref[...].astype(dst_ref.dtype)

    def finish(acc):
        if epilogue == "relu2":
            r = jnp.maximum(acc, 0.0)
            acc = r * r
        elif epilogue == "residual":
            res = res_refs[0][...]
            if len(res_refs) == 2:
                res = jnp.where(pl.program_id(0) < res_split, res, res_refs[1][...])
            acc = res + gate_ref[...] * acc
        return acc

    def product():
        return _dot(x_ref[...], w_ref[...])

    if nk == 1:
        o_ref[...] = finish(product()).astype(o_ref.dtype)
        return

    k = pl.program_id(2)

    @pl.when(k == 0)
    def _():
        o_ref[...] = product()

    @pl.when(jnp.logical_and(k > 0, k < nk - 1))
    def _():
        o_ref[...] += product()

    @pl.when(k == nk - 1)
    def _():
        o_ref[...] = finish(o_ref[...] + product())


class _Cast(NamedTuple):
    src: jax.Array
    lead: int
    col0: int = 0
    ncols: int = 0

    @property
    def shape(self):
        return self.src.shape[1], self.ncols or self.src.shape[2]


def _cast_kernel(s_ref, o_ref):
    o_ref[...] = s_ref[...].astype(o_ref.dtype)


def _cast_weight(c):
    rows, cols = c.shape
    tr, tc = _tile(rows, CAST_TILE_COLS), _tile(cols, 2 * CAST_TILE_COLS)
    assert c.col0 % tc == 0
    j0 = c.col0 // tc
    return pl.pallas_call(
        _cast_kernel,
        grid=(rows // tr, cols // tc),
        in_specs=[pl.BlockSpec((None, tr, tc), lambda i, j: (c.lead, i, j0 + j))],
        out_specs=pl.BlockSpec((tr, tc), lambda i, j: (i, j)),
        out_shape=jax.ShapeDtypeStruct((rows, cols), BF16),
        compiler_params=_params(("parallel", "parallel")),
        name="cast_weight",
    )(c.src)


def _matmul(x, w, out_dtype, epilogue="plain", *, res=None, mod=None, layer=None, gate_chunk=None,
            groups=None, casts=()):
    m, kdim = x.shape
    n = w.shape[1]
    bm = _tile(m, MM_BM)
    n_res = 0 if epilogue != "residual" else (2 if isinstance(res, tuple) else 1)
    out_bytes = jnp.dtype(out_dtype).itemsize + 4 * n_res

    def fits(bn, bk):
        return 2 * (2 * bk * (bm + bn) + bm * bn * out_bytes) + 4 * bm * bn <= MM_VMEM_BUDGET_BYTES

    bn_full = _tile(n, MM_BN)
    candidates = [(bn_full, kdim), (max(bn_full // 2, V7X_LANES), kdim)]
    candidates += [(bn_full, kdim // s) for s in (2, 4, 8, 16, 32)]
    bn, bk = next((bn_, bk_) for bn_, bk_ in candidates if kdim % bk_ == 0 and fits(bn_, bk_))
    nk = kdim // bk
    nj = n // bn
    assert nk == 1 or out_dtype == F32
    in_specs = [
        pl.BlockSpec((bm, bk), lambda i, j, k: (i, k)),
        pl.BlockSpec((bk, bn), lambda i, j, k: (k, j)),
    ]
    args = [x, w]
    res_split = 0
    if epilogue == "residual":
        nb = n // bn
        if isinstance(res, tuple):
            res_a, res_b = res
            res_split = res_a.shape[0] // bm
            assert res_a.shape[0] % bm == 0 and res_a.shape[0] + res_b.shape[0] == m
            in_specs += [
                pl.BlockSpec((bm, bn), lambda i, j, k: (jnp.minimum(i, res_split - 1),
                                                        jnp.where(i < res_split, j, nb - 1))),
                pl.BlockSpec((bm, bn), lambda i, j, k: (jnp.maximum(i - res_split, 0),
                                                        jnp.where(i < res_split, 0, j))),
            ]
            args += [res_a, res_b]
        else:
            in_specs.append(pl.BlockSpec((bm, bn), lambda i, j, k: (i, j)))
            args.append(res)
        in_specs.append(pl.BlockSpec((None, None, 1, bn),
                                     lambda i, j, k: (layer, groups.of_tile(i, bm), 0, gate_chunk * nb + j)))
        args.append(mod)

    out_specs = [pl.BlockSpec((bm, bn), lambda i, j, k: (i, j))]
    out_shape = [jax.ShapeDtypeStruct((m, n), out_dtype)]
    n_steps = (m // bm) * nj * nk
    cast_bytes = 0
    for c in casts:
        rows, cols = c.shape
        assert cols % CAST_TILE_COLS == 0 and c.col0 % CAST_TILE_COLS == 0
        tiles_c = cols // CAST_TILE_COLS
        tile_rows = next(tr for tr in (128, 256, 512, 1024, 2048)
                         if rows % tr == 0 and (rows // tr) * tiles_c <= n_steps)
        n_tiles = (rows // tile_rows) * tiles_c
        cast_bytes += 2 * (4 + 2) * tile_rows * CAST_TILE_COLS
        assert cast_bytes <= CAST_VMEM_BUDGET_BYTES

        def tile_of(i, j, k, n_tiles=n_tiles, tiles_c=tiles_c):
            t = jnp.minimum((i * nj + j) * nk + k, n_tiles - 1)
            return t // tiles_c, t % tiles_c

        def src_tile_of(i, j, k, c=c, tile_of=tile_of):
            r, t = tile_of(i, j, k)
            return c.lead, r, c.col0 // CAST_TILE_COLS + t

        in_specs.append(pl.BlockSpec((None, tile_rows, CAST_TILE_COLS), src_tile_of))
        args.append(c.src)
        out_specs.append(pl.BlockSpec((tile_rows, CAST_TILE_COLS), tile_of))
        out_shape.append(jax.ShapeDtypeStruct((rows, cols), BF16))

    outs = pl.pallas_call(
        functools.partial(_mm_kernel, nk=nk, epilogue=epilogue, n_res=n_res, res_split=res_split,
                          n_casts=len(casts)),
        grid=(m // bm, nj, nk),
        in_specs=in_specs,
        out_specs=out_specs,
        out_shape=out_shape,
        compiler_params=_params(("arbitrary", "arbitrary", "arbitrary")),
        name="matmul_" + epilogue,
    )(*args)
    return outs[0] if not casts else outs


class _ScanLayout(NamedTuple):
    n_ctx: int
    per_ctx: int
    per_lat: int
    total: int
    batch: int
    reverse: bool

    def pos(self, g):
        r = self.total - 1 - g if self.reverse else g
        is_lat = r >= self.n_ctx
        rl = jnp.maximum(r - self.n_ctx, 0)
        seq_ctx = jnp.minimum(r // self.per_ctx, self.batch - 1)
        seq_lat = rl // self.per_lat
        pos = jnp.where(is_lat, rl % self.per_lat, r % self.per_ctx)
        per = jnp.where(is_lat, self.per_lat, self.per_ctx)
        start, stop = (per - 1, 0) if self.reverse else (0, per - 1)
        return r, is_lat, seq_ctx, seq_lat, pos == start, pos == stop


def _level_ref(c, half, reverse):
    rows, width = c.shape
    blk = 2 * half
    idx = half if reverse else half - 1
    assert blk % V7X_SUBLANES == 0
    c3 = c.reshape(rows // blk, blk, width)
    ref = jnp.broadcast_to(c3[:, idx:idx + 1, :], c3.shape)
    return ref.reshape(rows, width)


def _exp2_neg_abs(d):
    bits = lax.bitcast_convert_type(d, jnp.uint32) | jnp.uint32(0x80000000)
    return jnp.exp2(lax.bitcast_convert_type(bits, F32))


def _level_weights(c, fg, level, reverse):
    rows = c.shape[0]
    if level > 2:
        w = _exp2_neg_abs(c - _level_ref(c, 1 << (level - 1), reverse))
        return w, w
    if level == 1:
        return fg, None
    pos = lax.broadcasted_iota(jnp.int32, c.shape, 0) & 3
    prev = pltpu.roll(fg, 1, 0)
    nxt = pltpu.roll(fg, rows - 1, 0)
    if reverse:
        return fg * jnp.where(pos == 0, nxt, 1.0), jnp.where(pos == 3, prev, 1.0)
    return fg * jnp.where(pos == 3, prev, 1.0), jnp.where(pos == 0, nxt, 1.0)


def _hg_scan_kernel(*refs, chunk, heads, dk, dv, layer, lay, fuse_out):
    if fuse_out:
        (q_ref, v_ref, f_ref, lbl_ref, s0_ref, ofwd_ref, g_ref, gn_ref,
         o_ref, sfin_ref, st_ref) = refs
    else:
        q_ref, v_ref, f_ref, lbl_ref, s0_ref, o_ref, sfin_ref, st_ref = refs
    reverse = lay.reverse
    _, is_lat, _, _, first, last = lay.pos(pl.program_id(1))
    n_levels = chunk.bit_length() - 1
    low_levels = min(n_levels, V7X_SUBLANES.bit_length() - 1)

    @pl.when(jnp.logical_and(first, is_lat))
    def _():
        for j in range(heads):
            st_ref[j] = s0_ref[0, j].T

    @pl.when(jnp.logical_and(first, jnp.logical_not(is_lat)))
    def _():
        st_ref[...] = jnp.zeros(st_ref.shape, F32)

    lbl = lbl_ref[...]
    e = jnp.exp(lbl - jnp.max(lbl, axis=0, keepdims=True))
    lb = jnp.sum(e[:layer + 1], axis=0, keepdims=True) / jnp.sum(e, axis=0, keepdims=True)

    fg_all = lb + (1.0 - lb) * _sigmoid(f_ref[...])
    k_all = 1.0 - fg_all
    log2f = jnp.log2(fg_all)

    row = lax.broadcasted_iota(jnp.int32, (chunk, chunk), 0)
    col = lax.broadcasted_iota(jnp.int32, (chunk, chunk), 1)
    before = (col > row) if reverse else (col < row)
    tri = jnp.logical_or(before, col == row).astype(BF16)
    hi = log2f.astype(BF16)
    r1 = log2f - hi.astype(F32)
    mid = r1.astype(BF16)
    lo = (r1 - mid.astype(F32)).astype(BF16)
    c_all = _dot(tri, hi) + _dot(tri, mid) + _dot(tri, lo)

    x = row ^ col
    lvl = jnp.zeros((chunk, chunk), jnp.int32)
    for b in range(n_levels):
        lvl = lvl + (x >= (1 << b)).astype(jnp.int32)
    lvl = jnp.where(jnp.logical_or(before, x == 0), lvl, -1)

    end = 0 if reverse else chunk - 1
    if fuse_out:
        gn = gn_ref[...]

    def pair_weights(j):
        sl = slice(j * dk, (j + 1) * dk)
        q = _silu(q_ref[:, sl])
        k = k_all[:, sl]
        c = c_all[:, sl]
        fg = fg_all[:, sl]
        q_bf, k_bf = q.astype(BF16), k.astype(BF16)
        a = jnp.where(lvl == 0, _dot_nt(q_bf, k_bf), 0.0)
        for level in range(1, low_levels + 1):
            w_q, w_k = _level_weights(c, fg, level, reverse)
            a_l = _dot_nt((q * w_q).astype(BF16), k_bf if w_k is None else (k * w_k).astype(BF16))
            a = jnp.where(lvl == level, a_l, a)
        for level in range(low_levels + 1, n_levels + 1):
            half = 1 << (level - 1)
            w, _ = _level_weights(c, fg, level, reverse)
            q_rows, k_rows = [], []
            for b0 in range(0, chunk, 2 * half):
                early, late = (b0 + half, b0) if reverse else (b0, b0 + half)
                q_rows.append(q[late:late + half] * w[late:late + half])
                k_scaled = k[early:early + half] * w[early:early + half]
                k_rows += [k[late:late + half], k_scaled] if reverse else [k_scaled, k[late:late + half]]
            p = _dot_nt(jnp.concatenate(q_rows, axis=0).astype(BF16),
                        jnp.concatenate(k_rows, axis=0).astype(BF16))
            pieces = []
            for b, b0 in enumerate(range(0, chunk, 2 * half)):
                early, late = (b0 + half, b0) if reverse else (b0, b0 + half)
                upd = jnp.where(lvl[late:late + half] == level, p[b * half:(b + 1) * half], a[late:late + half])
                pieces += [upd, a[early:early + half]] if reverse else [a[early:early + half], upd]
            a = jnp.concatenate(pieces, axis=0)
        return q, k, c, a

    def outputs_and_state(j, q, k, c, a):
        sv = slice(j * dv, (j + 1) * dv)
        v = v_ref[:, sv].astype(BF16)
        c_end = c[end:end + 1, :]
        st = st_ref[j]
        o = _dot(a.astype(BF16), v) + _dot_nt((q * jnp.exp2(c)).astype(BF16), st.astype(BF16))
        if fuse_out:
            o = o + ofwd_ref[:, sv]
            y = o * lax.rsqrt(jnp.mean(o * o, axis=-1, keepdims=True) + EPS) * gn
            o = y * _silu(g_ref[:, sv])
        o_ref[:, sv] = o.astype(o_ref.dtype)
        k_dec = (k * jnp.exp2(c_end - c)).astype(BF16)
        st_ref[j] = st * jnp.exp2(c_end) + _dot_tn(v, k_dec)

    pending = [pair_weights(j) for j in range(min(HG_LOOKAHEAD, heads))]
    for j in range(heads):
        if j + HG_LOOKAHEAD < heads:
            pending.append(pair_weights(j + HG_LOOKAHEAD))
        outputs_and_state(j, *pending.pop(0))

    @pl.when(jnp.logical_and(last, jnp.logical_not(is_lat)))
    def _():
        for j in range(heads):
            sfin_ref[0, j] = st_ref[j].T


def _hg_scan(proj_qv, proj_fg, lb_logits_dir, s0_lat, *, groups, batch, seq, n_heads, dk, dv, d, reverse, layer,
             o_fwd=None, g_norm=None):
    chunk = _tile(math.gcd(seq, groups.dec_seq), HG_CHUNK)
    heads = _tile(n_heads, HG_HEADS_PER_STEP)
    assert dk == dv and d == n_heads * dk
    w = heads * dk
    segs = d // w
    lay = _ScanLayout(n_ctx=groups.ctx_rows // chunk, per_ctx=seq // chunk, per_lat=groups.dec_seq // chunk,
                      total=groups.rows // chunk, batch=batch, reverse=reverse)
    fuse_out = o_fwd is not None

    def seg_spec(seg):
        return pl.BlockSpec((chunk, w), lambda h, g: (lay.pos(g)[0], seg * segs + h))

    n_lb = lb_logits_dir.shape[0]
    in_specs = [seg_spec(0), seg_spec(1), seg_spec(1 if reverse else 0),
                pl.BlockSpec((n_lb, w), lambda h, g: (0, h)),
                pl.BlockSpec((1, heads, dk, dv), lambda h, g: (lay.pos(g)[3], h, 0, 0))]
    args = [proj_qv, proj_qv, proj_fg, lb_logits_dir, s0_lat]
    if fuse_out:
        in_specs += [seg_spec(0), seg_spec(2), pl.BlockSpec((1, dv), lambda h, g: (0, 0))]
        args += [o_fwd, proj_fg, g_norm.reshape(1, dv)]
    o, s_fin = pl.pallas_call(
        functools.partial(_hg_scan_kernel, chunk=chunk, heads=heads, dk=dk, dv=dv,
                          layer=layer, lay=lay, fuse_out=fuse_out),
        grid=(n_heads // heads, lay.total),
        in_specs=in_specs,
        out_specs=[
            pl.BlockSpec((chunk, w), lambda h, g: (lay.pos(g)[0], h)),
            pl.BlockSpec((1, heads, dk, dv), lambda h, g: (lay.pos(g)[2], h, 0, 0)),
        ],
        out_shape=[
            jax.ShapeDtypeStruct((groups.rows, d), BF16 if fuse_out else F32),
            jax.ShapeDtypeStruct((batch, n_heads, dk, dv), F32),
        ],
        scratch_shapes=[pltpu.VMEM((heads, dv, dk), F32)],
        compiler_params=_params(("parallel", "arbitrary")),
        name="hgrn2_scan_" + ("bwd" if reverse else "fwd"),
    )(*args)
    return o, s_fin


def _rope_tables(seq, hd):
    half = hd // 2
    rows = seq // GRID_W
    r_pos = jnp.repeat(jnp.arange(rows), GRID_W).astype(F32)
    c_pos = jnp.tile(jnp.arange(GRID_W), rows).astype(F32)
    inv_freq = ROPE_BASE ** (-jnp.arange(0, half, 2, dtype=F32) / half)
    ang_r = r_pos[:, None] * inv_freq
    ang_c = c_pos[:, None] * inv_freq
    cos = jnp.concatenate([jnp.cos(ang_r)] * 2 + [jnp.cos(ang_c)] * 2, axis=-1)
    sin = jnp.concatenate([-jnp.sin(ang_r), jnp.sin(ang_r), -jnp.sin(ang_c), jnp.sin(ang_c)], axis=-1)
    return cos, sin


def _rope_tile(x, cos, sin):
    hd = x.shape[1]
    quarter = hd // 4
    lane = lax.broadcasted_iota(jnp.int32, x.shape, 1)
    first = (lane % (2 * quarter)) < quarter
    partner = jnp.where(first, pltpu.roll(x, hd - quarter, 1), pltpu.roll(x, quarter, 1))
    return x * cos + partner * sin


def _attn_kernel(*refs, hd, heads, tq, sub, lookahead, past, scale, lam_init, latent):
    if latent:
        (lam_ref, q_ref, k_ref, v_ref, kc_ref, vc_ref, cos_ref, sin_ref, g_ref, _,
         o_ref, kall_ref, vall_ref) = refs
    else:
        lam_ref, q_ref, k_ref, v_ref, g_ref, _, o_ref, knew_ref, vnew_ref = refs
        knew_ref[...] = k_ref[...]
        vnew_ref[...] = v_ref[...]
    w = 2 * hd
    lm = lam_ref[...]
    lam = (jnp.exp(jnp.sum(lm[0:1] * lm[1:2], axis=-1, keepdims=True))
           - jnp.exp(jnp.sum(lm[2:3] * lm[3:4], axis=-1, keepdims=True)) + lam_init)
    g = g_ref[...]

    if latent:
        i = pl.program_id(2)

        @pl.when(i == 0)
        def _():
            kall_ref[0:past, :] = kc_ref[...].astype(BF16)
            vall_ref[0:past, :] = vc_ref[...].astype(BF16)
            vall_ref[past:, :] = v_ref[...].astype(BF16)
            for half in range(2):
                hs = slice(half * hd, (half + 1) * hd)
                kall_ref[past:, hs] = _rope_tile(k_ref[:, hs], cos_ref[...], sin_ref[...]).astype(BF16)

    def softmax_parts(qh, kh):
        s = _dot_nt(qh, kh)
        p = jnp.exp2(s - jnp.max(s, axis=-1, keepdims=True))
        return p.astype(BF16), 1.0 / jnp.sum(p, axis=-1, keepdims=True)

    units = [(j, r0) for j in range(heads) for r0 in range(0, tq, sub)]

    def probabilities(j, r0):
        parts = []
        for half in range(2):
            hs = slice(j * w + half * hd, j * w + (half + 1) * hd)
            qh = q_ref[r0:r0 + sub, hs]
            if latent:
                rows = pl.ds(pl.multiple_of(i * tq + r0, sub), sub)
                qh = _rope_tile(qh, cos_ref[rows, :], sin_ref[rows, :])
                kh = kall_ref[:, hs]
            else:
                kh = k_ref[:, hs].astype(BF16)
            parts.append(softmax_parts((qh * (scale * LOG2E)).astype(BF16), kh))
        return parts

    def output(j, r0, parts):
        (p1, r1), (p2, r2) = parts
        js = slice(j * w, (j + 1) * w)
        vj = vall_ref[...] if latent else v_ref[:, js].astype(BF16)
        o = r1 * _dot(p1, vj) - (lam * r2) * _dot(p2, vj)
        y = o * lax.rsqrt(jnp.mean(o * o, axis=-1, keepdims=True) + EPS) * g
        o_ref[r0:r0 + sub, js] = (y * (1.0 - lam_init)).astype(o_ref.dtype)

    pending = [probabilities(*u) for u in units[:lookahead]]
    for n, unit in enumerate(units):
        if n + lookahead < len(units):
            pending.append(probabilities(*units[n + lookahead]))
        output(*unit, pending.pop(0))


def _diff_attention(proj, lam_params, subln_g, *, batch, seq, row_off, n_heads, hd, lam_init,
                    cache_k=None, cache_v=None, prev=None):
    latent = cache_k is not None
    w = 2 * hd
    d = n_heads * w
    tq = _tile(seq, ATTN_TQ)
    nq = seq // tq
    heads = 1 if latent else _tile(n_heads, ATTN_CTX_HEADS_PER_STEP)
    assert row_off % seq == 0
    q_off, kv_off = row_off // tq, row_off // seq
    segs = n_heads // heads
    scale = hd ** -0.5
    past = cache_k.shape[0] // batch if latent else 0

    def kv_spec(rows, seg=0, off=0):
        return pl.BlockSpec((rows, heads * w), lambda b, h, i: (off + b, seg * segs + h))

    in_specs = [pl.BlockSpec(lam_params.shape, lambda b, h, i: (0, 0)),
                pl.BlockSpec((tq, heads * w), lambda b, h, i: (q_off + b * nq + i, h)),
                kv_spec(seq, 1, kv_off), kv_spec(seq, 2, kv_off)]
    args = [lam_params, proj, proj, proj]
    scratch = []
    if latent:
        cos, sin = _rope_tables(seq, hd)
        table = pl.BlockSpec((seq, hd), lambda b, h, i: (0, 0))
        in_specs += [kv_spec(past), kv_spec(past), table, table]
        args += [cache_k, cache_v, cos, sin]
        scratch = [pltpu.VMEM((past + seq, w), BF16), pltpu.VMEM((past + seq, w), BF16)]
    in_specs.append(pl.BlockSpec((1, w), lambda b, h, i: (0, 0)))
    args.append(subln_g.reshape(1, w))
    out_specs = [pl.BlockSpec((tq, heads * w), lambda b, h, i: (q_off + b * nq + i, h))]
    out_shape = [jax.ShapeDtypeStruct((proj.shape[0], d), BF16)]
    in_specs.append(pl.BlockSpec(memory_space=pl.ANY))
    args.append(prev)
    aliases = {len(args) - 1: 0}
    if not latent:
        out_specs += [kv_spec(seq), kv_spec(seq)]
        out_shape += [jax.ShapeDtypeStruct((batch * seq, d), F32)] * 2
    outs = pl.pallas_call(
        functools.partial(_attn_kernel, hd=hd, heads=heads, tq=tq, sub=_tile(tq, ATTN_SUB_TQ),
                          lookahead=1 if latent else 0, past=past, scale=scale, lam_init=lam_init,
                          latent=latent),
        grid=(batch, n_heads // heads, nq),
        in_specs=in_specs,
        out_specs=out_specs,
        out_shape=out_shape,
        scratch_shapes=scratch,
        input_output_aliases=aliases,
        compiler_params=_params(("parallel", "parallel", "arbitrary")),
        name="diff_attention_" + ("latent" if latent else "context"),
    )(*args)
    return outs[0] if latent else outs


def kernel(x_prompt, x_sample, state_hgrn_fwd, state_hgrn_bwd, cache_attn_k, cache_attn_v, c, c_ctx,
           w_mod, b_mod, norm_g, w_ff1, w_ff2, hg_w_in, hg_w_out, hg_g_norm, hg_lb_logits,
           da_w_in, da_w_out, da_subln_g, da_lambda, final_norm_g):
    batch, seq, d = x_prompt.shape
    dec_batch, dec_seq, _ = x_sample.shape
    depth = w_mod.shape[0]
    _, _, hg_heads, hg_dk, hg_dv = state_hgrn_fwd.shape
    _, _, past_len, da_halves, da_hd = cache_attn_k.shape
    da_heads = da_halves // 2
    ctx_rows = batch * seq
    lat_rows = dec_batch * dec_seq
    groups = _Groups(ctx_rows, dec_batch, dec_seq)

    x = (x_prompt.reshape(ctx_rows, d), x_sample.reshape(lat_rows, d))

    n_groups = 1 + dec_batch
    g8 = -(-n_groups // V7X_SUBLANES) * V7X_SUBLANES
    cond = jnp.concatenate([c_ctx[None, :], c, jnp.zeros((g8 - n_groups, d), F32)], axis=0)
    mod = _modulation(cond, w_mod, b_mod).reshape(depth, g8, 1, 6 * d)
    SH1, SC1, GT1, SH2, SC2, GT2 = range(6)

    def mixer_weights(layer):
        j = layer // N_MIXERS
        if layer % N_MIXERS == 0:
            return [_Cast(hg_w_in, j, 0, 2 * d), _Cast(hg_w_in, j, 2 * d, 3 * d)], _Cast(hg_w_out, j)
        return [_Cast(da_w_in, j)], _Cast(da_w_out, j)

    first_parts, _ = mixer_weights(0)
    w_in = [_cast_weight(first_parts[0])] + first_parts[1:]
    new_sf, new_sb, new_k, new_v = [], [], [], []
    for layer in range(depth):
        h = _norm_mod(x, norm_g[layer, 0], mod, layer, SC1, SH1, groups)
        j = layer // N_MIXERS
        out_casts = (mixer_weights(layer)[1], _Cast(w_ff1, layer))
        projs = []
        for n in range(len(w_in)):
            last = n + 1 == len(w_in)
            casts = out_casts if last else tuple(w for w in w_in[n + 1:n + 2] if isinstance(w, _Cast))
            out = _matmul(h, w_in[n], F32, casts=casts)
            if last:
                out, w_out, w1 = out
            elif casts:
                out, w_in[n + 1] = out
            projs.append(out)
        if layer % N_MIXERS == 0:
            scan = dict(groups=groups, batch=batch, seq=seq, n_heads=hg_heads, dk=hg_dk, dv=hg_dv, d=d,
                        layer=layer)
            o_fwd, s_fwd = _hg_scan(*projs, hg_lb_logits[0], state_hgrn_fwd[:, j], reverse=False, **scan)
            mixed, s_bwd = _hg_scan(*projs, hg_lb_logits[1], state_hgrn_bwd[:, j], reverse=True,
                                    o_fwd=o_fwd, g_norm=hg_g_norm[j], **scan)
            new_sf.append(s_fwd)
            new_sb.append(s_bwd)
        else:
            proj, = projs
            lam_init = 0.8 - 0.6 * math.exp(-0.3 * layer)
            att = dict(n_heads=da_heads, hd=da_hd, lam_init=lam_init)
            mixed, k_ctx, v_ctx = _diff_attention(proj, da_lambda[j], da_subln_g[j], batch=batch, seq=seq,
                                                  row_off=0, prev=h, **att)
            new_k.append(k_ctx.reshape(batch, seq, da_halves, da_hd))
            new_v.append(v_ctx.reshape(batch, seq, da_heads, 2 * da_hd))
            mixed = _diff_attention(proj, da_lambda[j], da_subln_g[j], batch=dec_batch, seq=dec_seq,
                                    row_off=ctx_rows,
                                    cache_k=cache_attn_k[:, j].reshape(dec_batch * past_len, d),
                                    cache_v=cache_attn_v[:, j].reshape(dec_batch * past_len, d),
                                    prev=mixed, **att)
        x = _matmul(mixed, w_out, F32, "residual", res=x, mod=mod, layer=layer, gate_chunk=GT1, groups=groups)
        h = _norm_mod(x, norm_g[layer, 1], mod, layer, SC2, SH2, groups)
        next_parts = mixer_weights(layer + 1)[0] if layer + 1 < depth else []
        u, w2, *w_in = _matmul(h, w1, BF16, "relu2", casts=(_Cast(w_ff2, layer), *next_parts))
        x = _matmul(u, w2, F32, "residual", res=x, mod=mod, layer=layer, gate_chunk=GT2, groups=groups)

    y_prompt = _final_norm(x, final_norm_g, 0, ctx_rows).reshape(batch, seq, d)
    y_sample = _final_norm(x, final_norm_g, ctx_rows, lat_rows).reshape(dec_batch, dec_seq, d)
    return (y_prompt, y_sample, jnp.stack(new_sf, axis=1), jnp.stack(new_sb, axis=1),
            jnp.stack(new_k, axis=1), jnp.stack(new_v, axis=1))
```

```python
import functools
import math
from typing import NamedTuple

import jax
import jax.numpy as jnp
from jax import lax
from jax.experimental import pallas as pl
from jax.experimental.pallas import tpu as pltpu

F32 = jnp.float32
BF16 = jnp.bfloat16

GRID_W = 64
EPS = 1e-6
ROPE_BASE = 10000.0
N_MIXERS = 2

V7X_SUBLANES = 8
V7X_LANES = 128
V7X_VMEM_BYTES = 64 * 1024 * 1024
COMPILER_RESERVED_VMEM_BYTES = 4 * 1024 * 1024
V7X_VMEM_LIMIT_BYTES = V7X_VMEM_BYTES - COMPILER_RESERVED_VMEM_BYTES

MM_BM = 1024
MM_BN = 1024
MM_VMEM_BUDGET_BYTES = 52 * 1024 * 1024
CAST_VMEM_BUDGET_BYTES = 10 * 1024 * 1024
CAST_TILE_COLS = 1024
ROW_TILE = 512
MOD_BN = 1024
HG_CHUNK = 128
HG_HEADS_PER_STEP = 32
HG_LOOKAHEAD = 1
ATTN_TQ = 1024
ATTN_SUB_TQ = 256
ATTN_CTX_HEADS_PER_STEP = 8

LOG2E = 1.4426950408889634


def _tile(full, want):
    t = min(full, want)
    assert full % t == 0, (full, want)
    return t


def _params(sem):
    return pltpu.CompilerParams(dimension_semantics=sem, vmem_limit_bytes=V7X_VMEM_LIMIT_BYTES)


def _silu(x):
    half = 0.5 * x
    return half + half * jnp.tanh(half)


def _dot(a, b):
    return jnp.dot(a, b, preferred_element_type=F32)


def _dot_nt(a, b):
    return lax.dot_general(a, b, (((1,), (1,)), ((), ())), preferred_element_type=F32)


def _dot_tn(a, b):
    return lax.dot_general(a, b, (((0,), (0,)), ((), ())), preferred_element_type=F32)


class _Groups:
    def __init__(self, ctx_rows, dec_batch, dec_seq):
        self.ctx_rows = ctx_rows
        self.dec_batch = dec_batch
        self.dec_seq = dec_seq
        self.rows = ctx_rows + dec_batch * dec_seq

    def of_tile(self, i, tile):
        assert self.ctx_rows % tile == 0 and self.dec_seq % tile == 0
        g = jnp.int32(0)
        for b in range(self.dec_batch):
            g = g + (i >= (self.ctx_rows + b * self.dec_seq) // tile).astype(jnp.int32)
        return g


def _mod_kernel(c_ref, w_ref, b_ref, o_ref):
    a = _silu(c_ref[...]).astype(BF16)
    o_ref[...] = _dot(a, w_ref[...].astype(BF16)) + b_ref[...]


def _modulation(cond, w_mod, b_mod):
    g8, d = cond.shape
    depth, _, n = w_mod.shape
    bn = _tile(n, MOD_BN)
    return pl.pallas_call(
        _mod_kernel,
        grid=(depth, n // bn),
        in_specs=[
            pl.BlockSpec((g8, d), lambda l, j: (0, 0)),
            pl.BlockSpec((None, d, bn), lambda l, j: (l, 0, j)),
            pl.BlockSpec((None, 1, bn), lambda l, j: (l, 0, j)),
        ],
        out_specs=pl.BlockSpec((None, g8, bn), lambda l, j: (l, 0, j)),
        out_shape=jax.ShapeDtypeStruct((depth, g8, n), F32),
        compiler_params=_params(("arbitrary", "arbitrary")),
        name="adaln_modulation",
    )(cond, w_mod, b_mod.reshape(depth, 1, n))


def _norm_kernel(*refs, modulated, x_split):
    if modulated:
        *x_refs, g_ref, sc_ref, sh_ref, o_ref = refs
    else:
        *x_refs, g_ref, o_ref = refs
    x = x_refs[0][...]
    if len(x_refs) == 2:
        x = jnp.where(pl.program_id(0) < x_split, x, x_refs[1][...])
    y = x * lax.rsqrt(jnp.mean(x * x, axis=-1, keepdims=True) + EPS) * g_ref[...]
    if modulated:
        y = y * (1.0 + sc_ref[...]) + sh_ref[...]
    o_ref[...] = y.astype(o_ref.dtype)


def _norm_mod(x, g, mod, layer, sc_chunk, sh_chunk, groups):
    rows, d = groups.rows, g.shape[0]
    tm = _tile(math.gcd(groups.ctx_rows, groups.dec_seq), ROW_TILE)

    def vec(chunk):
        return pl.BlockSpec((None, None, 1, d), lambda i: (layer, groups.of_tile(i, tm), 0, chunk))

    if isinstance(x, tuple):
        x_split = x[0].shape[0] // tm
        x_specs = [pl.BlockSpec((tm, d), lambda i: (jnp.minimum(i, x_split - 1), 0)),
                   pl.BlockSpec((tm, d), lambda i: (jnp.maximum(i - x_split, 0), 0))]
        xs = list(x)
    else:
        x_split = 0
        x_specs = [pl.BlockSpec((tm, d), lambda i: (i, 0))]
        xs = [x]
    return pl.pallas_call(
        functools.partial(_norm_kernel, modulated=True, x_split=x_split),
        grid=(rows // tm,),
        in_specs=x_specs + [
            pl.BlockSpec((1, d), lambda i: (0, 0)),
            vec(sc_chunk),
            vec(sh_chunk),
        ],
        out_specs=pl.BlockSpec((tm, d), lambda i: (i, 0)),
        out_shape=jax.ShapeDtypeStruct((rows, d), BF16),
        compiler_params=_params(("parallel",)),
        name="rmsnorm_modulate",
    )(*xs, g.reshape(1, d), mod, mod)


def _final_norm(x, g, row_off, rows):
    d = x.shape[1]
    tm = _tile(math.gcd(rows, row_off) if row_off else rows, ROW_TILE)
    off = row_off // tm
    return pl.pallas_call(
        functools.partial(_norm_kernel, modulated=False, x_split=0),
        grid=(rows // tm,),
        in_specs=[
            pl.BlockSpec((tm, d), lambda i: (i + off, 0)),
            pl.BlockSpec((1, d), lambda i: (0, 0)),
        ],
        out_specs=pl.BlockSpec((tm, d), lambda i: (i, 0)),
        out_shape=jax.ShapeDtypeStruct((rows, d), F32),
        compiler_params=_params(("parallel",)),
        name="final_rmsnorm",
    )(x, g.reshape(1, d))


def _mm_kernel(*refs, nk, epilogue, n_res, res_split, n_casts):
    x_ref, w_ref = refs[:2]
    n_in = 2 + (n_res + 1 if epilogue == "residual" else 0)
    res_refs = refs[2:2 + n_res]
    gate_ref = refs[n_in - 1]
    o_ref = refs[n_in + n_casts]
    for src_ref, dst_ref in zip(refs[n_in:n_in + n_casts], refs[n_in + n_casts + 1:]):
        dst_ref[...] = src_ref[...].astype(dst_ref.dtype)

    def finish(acc):
        if epilogue == "relu2":
            r = jnp.maximum(acc, 0.0)
            acc = r * r
        elif epilogue == "residual":
            res = res_refs[0][...]
            if len(res_refs) == 2:
                res = jnp.where(pl.program_id(0) < res_split, res, res_refs[1][...])
            acc = res + gate_ref[...] * acc
        return acc

    def product():
        return _dot(x_ref[...], w_ref[...])

    if nk == 1:
        o_ref[...] = finish(product()).astype(o_ref.dtype)
        return

    k = pl.program_id(2)

    @pl.when(k == 0)
    def _():
        o_ref[...] = product()

    @pl.when(jnp.logical_and(k > 0, k < nk - 1))
    def _():
        o_ref[...] += product()

    @pl.when(k == nk - 1)
    def _():
        o_ref[...] = finish(o_ref[...] + product())


class _Cast(NamedTuple):
    src: jax.Array
    lead: int
    col0: int = 0
    ncols: int = 0

    @property
    def shape(self):
        return self.src.shape[1], self.ncols or self.src.shape[2]


def _cast_kernel(s_ref, o_ref):
    o_ref[...] = s_ref[...].astype(o_ref.dtype)


def _cast_weight(c):
    rows, cols = c.shape
    tr, tc = _tile(rows, CAST_TILE_COLS), _tile(cols, 2 * CAST_TILE_COLS)
    assert c.col0 % tc == 0
    j0 = c.col0 // tc
    return pl.pallas_call(
        _cast_kernel,
        grid=(rows // tr, cols // tc),
        in_specs=[pl.BlockSpec((None, tr, tc), lambda i, j: (c.lead, i, j0 + j))],
        out_specs=pl.BlockSpec((tr, tc), lambda i, j: (i, j)),
        out_shape=jax.ShapeDtypeStruct((rows, cols), BF16),
        compiler_params=_params(("parallel", "parallel")),
        name="cast_weight",
    )(c.src)


def _matmul(x, w, out_dtype, epilogue="plain", *, res=None, mod=None, layer=None, gate_chunk=None,
            groups=None, casts=()):
    m, kdim = x.shape
    n = w.shape[1]
    bm = _tile(m, MM_BM)
    n_res = 0 if epilogue != "residual" else (2 if isinstance(res, tuple) else 1)
    out_bytes = jnp.dtype(out_dtype).itemsize + 4 * n_res

    def fits(bn, bk):
        return 2 * (2 * bk * (bm + bn) + bm * bn * out_bytes) + 4 * bm * bn <= MM_VMEM_BUDGET_BYTES

    bn_full = _tile(n, MM_BN)
    candidates = [(bn_full, kdim), (max(bn_full // 2, V7X_LANES), kdim)]
    candidates += [(bn_full, kdim // s) for s in (2, 4, 8, 16, 32)]
    bn, bk = next((bn_, bk_) for bn_, bk_ in candidates if kdim % bk_ == 0 and fits(bn_, bk_))
    nk = kdim // bk
    nj = n // bn
    assert nk == 1 or out_dtype == F32
    in_specs = [
        pl.BlockSpec((bm, bk), lambda i, j, k: (i, k)),
        pl.BlockSpec((bk, bn), lambda i, j, k: (k, j)),
    ]
    args = [x, w]
    res_split = 0
    if epilogue == "residual":
        nb = n // bn
        if isinstance(res, tuple):
            res_a, res_b = res
            res_split = res_a.shape[0] // bm
            assert res_a.shape[0] % bm == 0 and res_a.shape[0] + res_b.shape[0] == m
            in_specs += [
                pl.BlockSpec((bm, bn), lambda i, j, k: (jnp.minimum(i, res_split - 1),
                                                        jnp.where(i < res_split, j, nb - 1))),
                pl.BlockSpec((bm, bn), lambda i, j, k: (jnp.maximum(i - res_split, 0),
                                                        jnp.where(i < res_split, 0, j))),
            ]
            args += [res_a, res_b]
        else:
            in_specs.append(pl.BlockSpec((bm, bn), lambda i, j, k: (i, j)))
            args.append(res)
        in_specs.append(pl.BlockSpec((None, None, 1, bn),
                                     lambda i, j, k: (layer, groups.of_tile(i, bm), 0, gate_chunk * nb + j)))
        args.append(mod)

    out_specs = [pl.BlockSpec((bm, bn), lambda i, j, k: (i, j))]
    out_shape = [jax.ShapeDtypeStruct((m, n), out_dtype)]
    n_steps = (m // bm) * nj * nk
    cast_bytes = 0
    for c in casts:
        rows, cols = c.shape
        assert cols % CAST_TILE_COLS == 0 and c.col0 % CAST_TILE_COLS == 0
        tiles_c = cols // CAST_TILE_COLS
        tile_rows = next(tr for tr in (128, 256, 512, 1024, 2048)
                         if rows % tr == 0 and (rows // tr) * tiles_c <= n_steps)
        n_tiles = (rows // tile_rows) * tiles_c
        cast_bytes += 2 * (4 + 2) * tile_rows * CAST_TILE_COLS
        assert cast_bytes <= CAST_VMEM_BUDGET_BYTES

        def tile_of(i, j, k, n_tiles=n_tiles, tiles_c=tiles_c):
            t = jnp.minimum((i * nj + j) * nk + k, n_tiles - 1)
            return t // tiles_c, t % tiles_c

        def src_tile_of(i, j, k, c=c, tile_of=tile_of):
            r, t = tile_of(i, j, k)
            return c.lead, r, c.col0 // CAST_TILE_COLS + t

        in_specs.append(pl.BlockSpec((None, tile_rows, CAST_TILE_COLS), src_tile_of))
        args.append(c.src)
        out_specs.append(pl.BlockSpec((tile_rows, CAST_TILE_COLS), tile_of))
        out_shape.append(jax.ShapeDtypeStruct((rows, cols), BF16))

    outs = pl.pallas_call(
        functools.partial(_mm_kernel, nk=nk, epilogue=epilogue, n_res=n_res, res_split=res_split,
                          n_casts=len(casts)),
        grid=(m // bm, nj, nk),
        in_specs=in_specs,
        out_specs=out_specs,
        out_shape=out_shape,
        compiler_params=_params(("arbitrary", "arbitrary", "arbitrary")),
        name="matmul_" + epilogue,
    )(*args)
    return outs[0] if not casts else outs


class _ScanLayout(NamedTuple):
    n_ctx: int
    per_ctx: int
    per_lat: int
    total: int
    batch: int
    reverse: bool

    def pos(self, g):
        r = self.total - 1 - g if self.reverse else g
        is_lat = r >= self.n_ctx
        rl = jnp.maximum(r - self.n_ctx, 0)
        seq_ctx = jnp.minimum(r // self.per_ctx, self.batch - 1)
        seq_lat = rl // self.per_lat
        pos = jnp.where(is_lat, rl % self.per_lat, r % self.per_ctx)
        per = jnp.where(is_lat, self.per_lat, self.per_ctx)
        start, stop = (per - 1, 0) if self.reverse else (0, per - 1)
        return r, is_lat, seq_ctx, seq_lat, pos == start, pos == stop


def _level_ref(c, half, reverse):
    rows, width = c.shape
    blk = 2 * half
    idx = half if reverse else half - 1
    assert blk % V7X_SUBLANES == 0
    c3 = c.reshape(rows // blk, blk, width)
    ref = jnp.broadcast_to(c3[:, idx:idx + 1, :], c3.shape)
    return ref.reshape(rows, width)


def _exp2_neg_abs(d):
    bits = lax.bitcast_convert_type(d, jnp.uint32) | jnp.uint32(0x80000000)
    return jnp.exp2(lax.bitcast_convert_type(bits, F32))


def _level_weights(c, fg, level, reverse):
    rows = c.shape[0]
    if level > 2:
        w = _exp2_neg_abs(c - _level_ref(c, 1 << (level - 1), reverse))
        return w, w
    if level == 1:
        return fg, None
    pos = lax.broadcasted_iota(jnp.int32, c.shape, 0) & 3
    prev = pltpu.roll(fg, 1, 0)
    nxt = pltpu.roll(fg, rows - 1, 0)
    if reverse:
        return fg * jnp.where(pos == 0, nxt, 1.0), jnp.where(pos == 3, prev, 1.0)
    return fg * jnp.where(pos == 3, prev, 1.0), jnp.where(pos == 0, nxt, 1.0)


def _hg_scan_kernel(*refs, chunk, heads, dk, dv, layer, lay, fuse_out):
    if fuse_out:
        (q_ref, v_ref, f_ref, lbl_ref, s0_ref, ofwd_ref, g_ref, gn_ref,
         o_ref, sfin_ref, st_ref) = refs
    else:
        q_ref, v_ref, f_ref, lbl_ref, s0_ref, o_ref, sfin_ref, st_ref = refs
    reverse = lay.reverse
    _, is_lat, _, _, first, last = lay.pos(pl.program_id(1))
    n_levels = chunk.bit_length() - 1
    low_levels = min(n_levels, V7X_SUBLANES.bit_length() - 1)

    @pl.when(jnp.logical_and(first, is_lat))
    def _():
        for j in range(heads):
            st_ref[j] = s0_ref[0, j].T

    @pl.when(jnp.logical_and(first, jnp.logical_not(is_lat)))
    def _():
        st_ref[...] = jnp.zeros(st_ref.shape, F32)

    lbl = lbl_ref[...]
    e = jnp.exp(lbl - jnp.max(lbl, axis=0, keepdims=True))
    lb = jnp.sum(e[:layer + 1], axis=0, keepdims=True) / jnp.sum(e, axis=0, keepdims=True)

    f = f_ref[...]
    e_f = _exp2_neg_abs(f * LOG2E)
    r = 1.0 / (1.0 + e_f)
    fg_all = lb + (1.0 - lb) * jnp.where(f >= 0.0, r, e_f * r)
    k_all = 1.0 - fg_all
    log2f = jnp.log2(fg_all)

    row = lax.broadcasted_iota(jnp.int32, (chunk, chunk), 0)
    col = lax.broadcasted_iota(jnp.int32, (chunk, chunk), 1)
    before = (col > row) if reverse else (col < row)
    tri = jnp.logical_or(before, col == row).astype(BF16)
    hi = log2f.astype(BF16)
    r1 = log2f - hi.astype(F32)
    mid = r1.astype(BF16)
    lo = (r1 - mid.astype(F32)).astype(BF16)
    c_all = _dot(tri, hi) + _dot(tri, mid) + _dot(tri, lo)

    x = row ^ col
    lvl = jnp.zeros((chunk, chunk), jnp.int32)
    for b in range(n_levels):
        lvl = lvl + (x >= (1 << b)).astype(jnp.int32)
    lvl = jnp.where(jnp.logical_or(before, x == 0), lvl, -1)

    end = 0 if reverse else chunk - 1
    if fuse_out:
        gn = gn_ref[...]

    def pair_weights(j):
        sl = slice(j * dk, (j + 1) * dk)
        q = _silu(q_ref[:, sl])
        k = k_all[:, sl]
        c = c_all[:, sl]
        fg = fg_all[:, sl]
        q_bf, k_bf = q.astype(BF16), k.astype(BF16)
        a = jnp.where(lvl == 0, _dot_nt(q_bf, k_bf), 0.0)
        for level in range(1, low_levels + 1):
            w_q, w_k = _level_weights(c, fg, level, reverse)
            a_l = _dot_nt((q * w_q).astype(BF16), k_bf if w_k is None else (k * w_k).astype(BF16))
            a = jnp.where(lvl == level, a_l, a)
        for level in range(low_levels + 1, n_levels + 1):
            half = 1 << (level - 1)
            w, _ = _level_weights(c, fg, level, reverse)
            q_rows, k_rows = [], []
            for b0 in range(0, chunk, 2 * half):
                early, late = (b0 + half, b0) if reverse else (b0, b0 + half)
                q_rows.append(q[late:late + half] * w[late:late + half])
                k_scaled = k[early:early + half] * w[early:early + half]
                k_rows += [k[late:late + half], k_scaled] if reverse else [k_scaled, k[late:late + half]]
            p = _dot_nt(jnp.concatenate(q_rows, axis=0).astype(BF16),
                        jnp.concatenate(k_rows, axis=0).astype(BF16))
            pieces = []
            for b, b0 in enumerate(range(0, chunk, 2 * half)):
                early, late = (b0 + half, b0) if reverse else (b0, b0 + half)
                upd = jnp.where(lvl[late:late + half] == level, p[b * half:(b + 1) * half], a[late:late + half])
                pieces += [upd, a[early:early + half]] if reverse else [a[early:early + half], upd]
            a = jnp.concatenate(pieces, axis=0)
        return q, k, c, a

    def outputs_and_state(j, q, k, c, a):
        sv = slice(j * dv, (j + 1) * dv)
        v = v_ref[:, sv].astype(BF16)
        c_end = c[end:end + 1, :]
        st = st_ref[j]
        o = _dot(a.astype(BF16), v) + _dot_nt((q * jnp.exp2(c)).astype(BF16), st.astype(BF16))
        if fuse_out:
            o = o + ofwd_ref[:, sv]
            y = o * lax.rsqrt(jnp.mean(o * o, axis=-1, keepdims=True) + EPS) * gn
            o = y * _silu(g_ref[:, sv])
        o_ref[:, sv] = o.astype(o_ref.dtype)
        k_dec = (k * jnp.exp2(c_end - c)).astype(BF16)
        st_ref[j] = st * jnp.exp2(c_end) + _dot_tn(v, k_dec)

    pending = [pair_weights(j) for j in range(min(HG_LOOKAHEAD, heads))]
    for j in range(heads):
        if j + HG_LOOKAHEAD < heads:
            pending.append(pair_weights(j + HG_LOOKAHEAD))
        outputs_and_state(j, *pending.pop(0))

    @pl.when(jnp.logical_and(last, jnp.logical_not(is_lat)))
    def _():
        for j in range(heads):
            sfin_ref[0, j] = st_ref[j].T


def _hg_scan(proj_qv, proj_fg, lb_logits_dir, s0_lat, *, groups, batch, seq, n_heads, dk, dv, d, reverse, layer,
             o_fwd=None, g_norm=None):
    chunk = _tile(math.gcd(seq, groups.dec_seq), HG_CHUNK)
    heads = _tile(n_heads, HG_HEADS_PER_STEP)
    assert dk == dv and d == n_heads * dk
    w = heads * dk
    segs = d // w
    lay = _ScanLayout(n_ctx=groups.ctx_rows // chunk, per_ctx=seq // chunk, per_lat=groups.dec_seq // chunk,
                      total=groups.rows // chunk, batch=batch, reverse=reverse)
    fuse_out = o_fwd is not None

    def seg_spec(seg):
        return pl.BlockSpec((chunk, w), lambda h, g: (lay.pos(g)[0], seg * segs + h))

    n_lb = lb_logits_dir.shape[0]
    in_specs = [seg_spec(0), seg_spec(1), seg_spec(1 if reverse else 0),
                pl.BlockSpec((n_lb, w), lambda h, g: (0, h)),
                pl.BlockSpec((1, heads, dk, dv), lambda h, g: (lay.pos(g)[3], h, 0, 0))]
    args = [proj_qv, proj_qv, proj_fg, lb_logits_dir, s0_lat]
    if fuse_out:
        in_specs += [seg_spec(0), seg_spec(2), pl.BlockSpec((1, dv), lambda h, g: (0, 0))]
        args += [o_fwd, proj_fg, g_norm.reshape(1, dv)]
    o, s_fin = pl.pallas_call(
        functools.partial(_hg_scan_kernel, chunk=chunk, heads=heads, dk=dk, dv=dv,
                          layer=layer, lay=lay, fuse_out=fuse_out),
        grid=(n_heads // heads, lay.total),
        in_specs=in_specs,
        out_specs=[
            pl.BlockSpec((chunk, w), lambda h, g: (lay.pos(g)[0], h)),
            pl.BlockSpec((1, heads, dk, dv), lambda h, g: (lay.pos(g)[2], h, 0, 0)),
        ],
        out_shape=[
            jax.ShapeDtypeStruct((groups.rows, d), BF16 if fuse_out else F32),
            jax.ShapeDtypeStruct((batch, n_heads, dk, dv), F32),
        ],
        scratch_shapes=[pltpu.VMEM((heads, dv, dk), F32)],
        compiler_params=_params(("parallel", "arbitrary")),
        name="hgrn2_scan_" + ("bwd" if reverse else "fwd"),
    )(*args)
    return o, s_fin


def _rope_tables(seq, hd):
    half = hd // 2
    rows = seq // GRID_W
    r_pos = jnp.repeat(jnp.arange(rows), GRID_W).astype(F32)
    c_pos = jnp.tile(jnp.arange(GRID_W), rows).astype(F32)
    inv_freq = ROPE_BASE ** (-jnp.arange(0, half, 2, dtype=F32) / half)
    ang_r = r_pos[:, None] * inv_freq
    ang_c = c_pos[:, None] * inv_freq
    cos = jnp.concatenate([jnp.cos(ang_r)] * 2 + [jnp.cos(ang_c)] * 2, axis=-1)
    sin = jnp.concatenate([-jnp.sin(ang_r), jnp.sin(ang_r), -jnp.sin(ang_c), jnp.sin(ang_c)], axis=-1)
    return cos, sin


def _rope_tile(x, cos, sin):
    hd = x.shape[1]
    quarter = hd // 4
    lane = lax.broadcasted_iota(jnp.int32, x.shape, 1)
    first = (lane % (2 * quarter)) < quarter
    partner = jnp.where(first, pltpu.roll(x, hd - quarter, 1), pltpu.roll(x, quarter, 1))
    return x * cos + partner * sin


def _attn_kernel(*refs, hd, heads, tq, sub, lookahead, past, scale, lam_init, latent):
    if latent:
        (lam_ref, q_ref, k_ref, v_ref, kc_ref, vc_ref, cos_ref, sin_ref, g_ref, _,
         o_ref, kall_ref, vall_ref) = refs
    else:
        lam_ref, q_ref, k_ref, v_ref, g_ref, _, o_ref, knew_ref, vnew_ref = refs
        knew_ref[...] = k_ref[...]
        vnew_ref[...] = v_ref[...]
    w = 2 * hd
    lm = lam_ref[...]
    lam = (jnp.exp(jnp.sum(lm[0:1] * lm[1:2], axis=-1, keepdims=True))
           - jnp.exp(jnp.sum(lm[2:3] * lm[3:4], axis=-1, keepdims=True)) + lam_init)
    g = g_ref[...]

    if latent:
        i = pl.program_id(2)

        @pl.when(i == 0)
        def _():
            kall_ref[0:past, :] = kc_ref[...].astype(BF16)
            vall_ref[0:past, :] = vc_ref[...].astype(BF16)
            vall_ref[past:, :] = v_ref[...].astype(BF16)
            for half in range(2):
                hs = slice(half * hd, (half + 1) * hd)
                kall_ref[past:, hs] = _rope_tile(k_ref[:, hs], cos_ref[...], sin_ref[...]).astype(BF16)

    def softmax_parts(qh, kh):
        s = _dot_nt(qh, kh)
        p = jnp.exp2(s - jnp.max(s, axis=-1, keepdims=True))
        return p.astype(BF16), 1.0 / jnp.sum(p, axis=-1, keepdims=True)

    units = [(j, r0) for j in range(heads) for r0 in range(0, tq, sub)]

    def probabilities(j, r0):
        parts = []
        for half in range(2):
            hs = slice(j * w + half * hd, j * w + (half + 1) * hd)
            qh = q_ref[r0:r0 + sub, hs]
            if latent:
                rows = pl.ds(pl.multiple_of(i * tq + r0, sub), sub)
                qh = _rope_tile(qh, cos_ref[rows, :], sin_ref[rows, :])
                kh = kall_ref[:, hs]
            else:
                kh = k_ref[:, hs].astype(BF16)
            parts.append(softmax_parts((qh * (scale * LOG2E)).astype(BF16), kh))
        return parts

    def output(j, r0, parts):
        (p1, r1), (p2, r2) = parts
        js = slice(j * w, (j + 1) * w)
        vj = vall_ref[...] if latent else v_ref[:, js].astype(BF16)
        o = r1 * _dot(p1, vj) - (lam * r2) * _dot(p2, vj)
        y = o * lax.rsqrt(jnp.mean(o * o, axis=-1, keepdims=True) + EPS) * g
        o_ref[r0:r0 + sub, js] = (y * (1.0 - lam_init)).astype(o_ref.dtype)

    pending = [probabilities(*u) for u in units[:lookahead]]
    for n, unit in enumerate(units):
        if n + lookahead < len(units):
            pending.append(probabilities(*units[n + lookahead]))
        output(*unit, pending.pop(0))


def _diff_attention(proj, lam_params, subln_g, *, batch, seq, row_off, n_heads, hd, lam_init,
                    cache_k=None, cache_v=None, prev=None):
    latent = cache_k is not None
    w = 2 * hd
    d = n_heads * w
    tq = _tile(seq, ATTN_TQ)
    nq = seq // tq
    heads = 1 if latent else _tile(n_heads, ATTN_CTX_HEADS_PER_STEP)
    assert row_off % seq == 0
    q_off, kv_off = row_off // tq, row_off // seq
    segs = n_heads // heads
    scale = hd ** -0.5
    past = cache_k.shape[0] // batch if latent else 0

    def kv_spec(rows, seg=0, off=0):
        return pl.BlockSpec((rows, heads * w), lambda b, h, i: (off + b, seg * segs + h))

    in_specs = [pl.BlockSpec(lam_params.shape, lambda b, h, i: (0, 0)),
                pl.BlockSpec((tq, heads * w), lambda b, h, i: (q_off + b * nq + i, h)),
                kv_spec(seq, 1, kv_off), kv_spec(seq, 2, kv_off)]
    args = [lam_params, proj, proj, proj]
    scratch = []
    if latent:
        cos, sin = _rope_tables(seq, hd)
        table = pl.BlockSpec((seq, hd), lambda b, h, i: (0, 0))
        in_specs += [kv_spec(past), kv_spec(past), table, table]
        args += [cache_k, cache_v, cos, sin]
        scratch = [pltpu.VMEM((past + seq, w), BF16), pltpu.VMEM((past + seq, w), BF16)]
    in_specs.append(pl.BlockSpec((1, w), lambda b, h, i: (0, 0)))
    args.append(subln_g.reshape(1, w))
    out_specs = [pl.BlockSpec((tq, heads * w), lambda b, h, i: (q_off + b * nq + i, h))]
    out_shape = [jax.ShapeDtypeStruct((proj.shape[0], d), BF16)]
    in_specs.append(pl.BlockSpec(memory_space=pl.ANY))
    args.append(prev)
    aliases = {len(args) - 1: 0}
    if not latent:
        out_specs += [kv_spec(seq), kv_spec(seq)]
        out_shape += [jax.ShapeDtypeStruct((batch * seq, d), F32)] * 2
    outs = pl.pallas_call(
        functools.partial(_attn_kernel, hd=hd, heads=heads, tq=tq, sub=_tile(tq, ATTN_SUB_TQ),
                          lookahead=1 if latent else 0, past=past, scale=scale, lam_init=lam_init,
                          latent=latent),
        grid=(batch, n_heads // heads, nq),
        in_specs=in_specs,
        out_specs=out_specs,
        out_shape=out_shape,
        scratch_shapes=scratch,
        input_output_aliases=aliases,
        compiler_params=_params(("parallel", "parallel", "arbitrary")),
        name="diff_attention_" + ("latent" if latent else "context"),
    )(*args)
    return outs[0] if latent else outs


def kernel(x_prompt, x_sample, state_hgrn_fwd, state_hgrn_bwd, cache_attn_k, cache_attn_v, c, c_ctx,
           w_mod, b_mod, norm_g, w_ff1, w_ff2, hg_w_in, hg_w_out, hg_g_norm, hg_lb_logits,
           da_w_in, da_w_out, da_subln_g, da_lambda, final_norm_g):
    batch, seq, d = x_prompt.shape
    dec_batch, dec_seq, _ = x_sample.shape
    depth = w_mod.shape[0]
    _, _, hg_heads, hg_dk, hg_dv = state_hgrn_fwd.shape
    _, _, past_len, da_halves, da_hd = cache_attn_k.shape
    da_heads = da_halves // 2
    ctx_rows = batch * seq
    lat_rows = dec_batch * dec_seq
    groups = _Groups(ctx_rows, dec_batch, dec_seq)

    x = (x_prompt.reshape(ctx_rows, d), x_sample.reshape(lat_rows, d))

    n_groups = 1 + dec_batch
    g8 = -(-n_groups // V7X_SUBLANES) * V7X_SUBLANES
    cond = jnp.concatenate([c_ctx[None, :], c, jnp.zeros((g8 - n_groups, d), F32)], axis=0)
    mod = _modulation(cond, w_mod, b_mod).reshape(depth, g8, 1, 6 * d)
    SH1, SC1, GT1, SH2, SC2, GT2 = range(6)

    def mixer_weights(layer):
        j = layer // N_MIXERS
        if layer % N_MIXERS == 0:
            return [_Cast(hg_w_in, j, 0, 2 * d), _Cast(hg_w_in, j, 2 * d, 3 * d)], _Cast(hg_w_out, j)
        return [_Cast(da_w_in, j)], _Cast(da_w_out, j)

    first_parts, _ = mixer_weights(0)
    w_in = [_cast_weight(first_parts[0])] + first_parts[1:]
    new_sf, new_sb, new_k, new_v = [], [], [], []
    for layer in range(depth):
        h = _norm_mod(x, norm_g[layer, 0], mod, layer, SC1, SH1, groups)
        j = layer // N_MIXERS
        out_casts = (mixer_weights(layer)[1], _Cast(w_ff1, layer))
        projs = []
        for n in range(len(w_in)):
            last = n + 1 == len(w_in)
            casts = out_casts if last else tuple(w for w in w_in[n + 1:n + 2] if isinstance(w, _Cast))
            out = _matmul(h, w_in[n], F32, casts=casts)
            if last:
                out, w_out, w1 = out
            elif casts:
                out, w_in[n + 1] = out
            projs.append(out)
        if layer % N_MIXERS == 0:
            scan = dict(groups=groups, batch=batch, seq=seq, n_heads=hg_heads, dk=hg_dk, dv=hg_dv, d=d,
                        layer=layer)
            o_fwd, s_fwd = _hg_scan(*projs, hg_lb_logits[0], state_hgrn_fwd[:, j], reverse=False, **scan)
            mixed, s_bwd = _hg_scan(*projs, hg_lb_logits[1], state_hgrn_bwd[:, j], reverse=True,
                                    o_fwd=o_fwd, g_norm=hg_g_norm[j], **scan)
            new_sf.append(s_fwd)
            new_sb.append(s_bwd)
        else:
            proj, = projs
            lam_init = 0.8 - 0.6 * math.exp(-0.3 * layer)
            att = dict(n_heads=da_heads, hd=da_hd, lam_init=lam_init)
            mixed, k_ctx, v_ctx = _diff_attention(proj, da_lambda[j], da_subln_g[j], batch=batch, seq=seq,
                                                  row_off=0, prev=h, **att)
            new_k.append(k_ctx.reshape(batch, seq, da_halves, da_hd))
            new_v.append(v_ctx.reshape(batch, seq, da_heads, 2 * da_hd))
            mixed = _diff_attention(proj, da_lambda[j], da_subln_g[j], batch=dec_batch, seq=dec_seq,
                                    row_off=ctx_rows,
                                    cache_k=cache_attn_k[:, j].reshape(dec_batch * past_len, d),
                                    cache_v=cache_attn_v[:, j].reshape(dec_batch * past_len, d),
                                    prev=mixed, **att)
        x = _matmul(mixed, w_out, F32, "residual", res=x, mod=mod, layer=layer, gate_chunk=GT1, groups=groups)
        h = _norm_mod(x, norm_g[layer, 1], mod, layer, SC2, SH2, groups)
        next_parts = mixer_weights(layer + 1)[0] if layer + 1 < depth else []
        u, w2, *w_in = _matmul(h, w1, BF16, "relu2", casts=(_Cast(w_ff2, layer), *next_parts))
        x = _matmul(u, w2, F32, "residual", res=x, mod=mod, layer=layer, gate_chunk=GT2, groups=groups)

    y_prompt = _final_norm(x, final_norm_g, 0, ctx_rows).reshape(batch, seq, d)
    y_sample = _final_norm(x, final_norm_g, ctx_rows, lat_rows).reshape(dec_batch, dec_seq, d)
    return (y_prompt, y_sample, jnp.stack(new_sf, axis=1), jnp.stack(new_sb, axis=1),
            jnp.stack(new_k, axis=1), jnp.stack(new_v, axis=1))
```

```python
import functools
import math
from typing import NamedTuple

import jax
import jax.numpy as jnp
from jax import lax
from jax.experimental import pallas as pl
from jax.experimental.pallas import tpu as pltpu

F32 = jnp.float32
BF16 = jnp.bfloat16

GRID_W = 64
EPS = 1e-6
ROPE_BASE = 10000.0
N_MIXERS = 2

V7X_SUBLANES = 8
V7X_LANES = 128
V7X_VMEM_BYTES = 64 * 1024 * 1024
COMPILER_RESERVED_VMEM_BYTES = 4 * 1024 * 1024
V7X_VMEM_LIMIT_BYTES = V7X_VMEM_BYTES - COMPILER_RESERVED_VMEM_BYTES

MM_BM = 1024
MM_BN = 1024
MM_VMEM_BUDGET_BYTES = 52 * 1024 * 1024
CAST_VMEM_BUDGET_BYTES = 10 * 1024 * 1024
CAST_TILE_COLS = 1024
ROW_TILE = 512
MOD_BN = 1024
HG_CHUNK = 128
HG_HEADS_PER_STEP = 32
HG_LOOKAHEAD = 1
ATTN_TQ = 1024
ATTN_SUB_TQ = 256
ATTN_CTX_HEADS_PER_STEP = 8

LOG2E = 1.4426950408889634


def _tile(full, want):
    t = min(full, want)
    assert full % t == 0, (full, want)
    return t


def _params(sem):
    return pltpu.CompilerParams(dimension_semantics=sem, vmem_limit_bytes=V7X_VMEM_LIMIT_BYTES)


def _silu(x):
    half = 0.5 * x
    return half + half * jnp.tanh(half)


def _dot(a, b):
    return jnp.dot(a, b, preferred_element_type=F32)


def _dot_nt(a, b):
    return lax.dot_general(a, b, (((1,), (1,)), ((), ())), preferred_element_type=F32)


def _dot_tn(a, b):
    return lax.dot_general(a, b, (((0,), (0,)), ((), ())), preferred_element_type=F32)


class _Groups:
    def __init__(self, ctx_rows, dec_batch, dec_seq):
        self.ctx_rows = ctx_rows
        self.dec_batch = dec_batch
        self.dec_seq = dec_seq
        self.rows = ctx_rows + dec_batch * dec_seq

    def of_tile(self, i, tile):
        assert self.ctx_rows % tile == 0 and self.dec_seq % tile == 0
        g = jnp.int32(0)
        for b in range(self.dec_batch):
            g = g + (i >= (self.ctx_rows + b * self.dec_seq) // tile).astype(jnp.int32)
        return g


def _mod_kernel(c_ref, w_ref, b_ref, o_ref):
    a = _silu(c_ref[...]).astype(BF16)
    o_ref[...] = _dot(a, w_ref[...].astype(BF16)) + b_ref[...]


def _modulation(cond, w_mod, b_mod):
    g8, d = cond.shape
    depth, _, n = w_mod.shape
    bn = _tile(n, MOD_BN)
    return pl.pallas_call(
        _mod_kernel,
        grid=(depth, n // bn),
        in_specs=[
            pl.BlockSpec((g8, d), lambda l, j: (0, 0)),
            pl.BlockSpec((None, d, bn), lambda l, j: (l, 0, j)),
            pl.BlockSpec((None, 1, bn), lambda l, j: (l, 0, j)),
        ],
        out_specs=pl.BlockSpec((None, g8, bn), lambda l, j: (l, 0, j)),
        out_shape=jax.ShapeDtypeStruct((depth, g8, n), F32),
        compiler_params=_params(("arbitrary", "arbitrary")),
        name="adaln_modulation",
    )(cond, w_mod, b_mod.reshape(depth, 1, n))


def _norm_kernel(*refs, modulated, x_split):
    if modulated:
        *x_refs, g_ref, sc_ref, sh_ref, o_ref = refs
    else:
        *x_refs, g_ref, o_ref = refs
    x = x_refs[0][...]
    if len(x_refs) == 2:
        x = jnp.where(pl.program_id(0) < x_split, x, x_refs[1][...])
    y = x * lax.rsqrt(jnp.mean(x * x, axis=-1, keepdims=True) + EPS) * g_ref[...]
    if modulated:
        y = y * (1.0 + sc_ref[...]) + sh_ref[...]
    o_ref[...] = y.astype(o_ref.dtype)


def _norm_mod(x, g, mod, layer, sc_chunk, sh_chunk, groups):
    rows, d = groups.rows, g.shape[0]
    tm = _tile(math.gcd(groups.ctx_rows, groups.dec_seq), ROW_TILE)

    def vec(chunk):
        return pl.BlockSpec((None, None, 1, d), lambda i: (layer, groups.of_tile(i, tm), 0, chunk))

    if isinstance(x, tuple):
        x_split = x[0].shape[0] // tm
        x_specs = [pl.BlockSpec((tm, d), lambda i: (jnp.minimum(i, x_split - 1), 0)),
                   pl.BlockSpec((tm, d), lambda i: (jnp.maximum(i - x_split, 0), 0))]
        xs = list(x)
    else:
        x_split = 0
        x_specs = [pl.BlockSpec((tm, d), lambda i: (i, 0))]
        xs = [x]
    return pl.pallas_call(
        functools.partial(_norm_kernel, modulated=True, x_split=x_split),
        grid=(rows // tm,),
        in_specs=x_specs + [
            pl.BlockSpec((1, d), lambda i: (0, 0)),
            vec(sc_chunk),
            vec(sh_chunk),
        ],
        out_specs=pl.BlockSpec((tm, d), lambda i: (i, 0)),
        out_shape=jax.ShapeDtypeStruct((rows, d), BF16),
        compiler_params=_params(("parallel",)),
        name="rmsnorm_modulate",
    )(*xs, g.reshape(1, d), mod, mod)


def _final_norm(x, g, row_off, rows):
    d = x.shape[1]
    tm = _tile(math.gcd(rows, row_off) if row_off else rows, ROW_TILE)
    off = row_off // tm
    return pl.pallas_call(
        functools.partial(_norm_kernel, modulated=False, x_split=0),
        grid=(rows // tm,),
        in_specs=[
            pl.BlockSpec((tm, d), lambda i: (i + off, 0)),
            pl.BlockSpec((1, d), lambda i: (0, 0)),
        ],
        out_specs=pl.BlockSpec((tm, d), lambda i: (i, 0)),
        out_shape=jax.ShapeDtypeStruct((rows, d), F32),
        compiler_params=_params(("parallel",)),
        name="final_rmsnorm",
    )(x, g.reshape(1, d))


def _mm_kernel(*refs, nk, epilogue, n_res, res_split, n_casts):
    x_ref, w_ref = refs[:2]
    n_in = 2 + (n_res + 1 if epilogue == "residual" else 0)
    res_refs = refs[2:2 + n_res]
    gate_ref = refs[n_in - 1]
    o_ref = refs[n_in + n_casts]
    for src_ref, dst_ref in zip(refs[n_in:n_in + n_casts], refs[n_in + n_casts + 1:]):
        dst_ref[...] = src_ref[...].astype(dst_ref.dtype)

    def finish(acc):
        if epilogue == "relu2":
            r = jnp.maximum(acc, 0.0)
            acc = r * r
        elif epilogue == "residual":
            res = res_refs[0][...]
            if len(res_refs) == 2:
                res = jnp.where(pl.program_id(0) < res_split, res, res_refs[1][...])
            acc = res + gate_ref[...] * acc
        return acc

    def product():
        return _dot(x_ref[...], w_ref[...])

    if nk == 1:
        o_ref[...] = finish(product()).astype(o_ref.dtype)
        return

    k = pl.program_id(2)

    @pl.when(k == 0)
    def _():
        o_ref[...] = product()

    @pl.when(jnp.logical_and(k > 0, k < nk - 1))
    def _():
        o_ref[...] += product()

    @pl.when(k == nk - 1)
    def _():
        o_ref[...] = finish(o_ref[...] + product())


class _Cast(NamedTuple):
    src: jax.Array
    lead: int
    col0: int = 0
    ncols: int = 0

    @property
    def shape(self):
        return self.src.shape[1], self.ncols or self.src.shape[2]


def _cast_kernel(s_ref, o_ref):
    o_ref[...] = s_ref[...].astype(o_ref.dtype)


def _cast_weight(c):
    rows, cols = c.shape
    tr, tc = _tile(rows, CAST_TILE_COLS), _tile(cols, 2 * CAST_TILE_COLS)
    assert c.col0 % tc == 0
    j0 = c.col0 // tc
    return pl.pallas_call(
        _cast_kernel,
        grid=(rows // tr, cols // tc),
        in_specs=[pl.BlockSpec((None, tr, tc), lambda i, j: (c.lead, i, j0 + j))],
        out_specs=pl.BlockSpec((tr, tc), lambda i, j: (i, j)),
        out_shape=jax.ShapeDtypeStruct((rows, cols), BF16),
        compiler_params=_params(("parallel", "parallel")),
        name="cast_weight",
    )(c.src)


def _matmul(x, w, out_dtype, epilogue="plain", *, res=None, mod=None, layer=None, gate_chunk=None,
            groups=None, casts=()):
    m, kdim = x.shape
    n = w.shape[1]
    bm = _tile(m, MM_BM)
    n_res = 0 if epilogue != "residual" else (2 if isinstance(res, tuple) else 1)
    out_bytes = jnp.dtype(out_dtype).itemsize + 4 * n_res

    def fits(bn, bk):
        return 2 * (2 * bk * (bm + bn) + bm * bn * out_bytes) + 4 * bm * bn <= MM_VMEM_BUDGET_BYTES

    bn_full = _tile(n, MM_BN)
    candidates = [(bn_full, kdim), (max(bn_full // 2, V7X_LANES), kdim)]
    candidates += [(bn_full, kdim // s) for s in (2, 4, 8, 16, 32)]
    bn, bk = next((bn_, bk_) for bn_, bk_ in candidates if kdim % bk_ == 0 and fits(bn_, bk_))
    nk = kdim // bk
    nj = n // bn
    assert nk == 1 or out_dtype == F32
    in_specs = [
        pl.BlockSpec((bm, bk), lambda i, j, k: (i, k)),
        pl.BlockSpec((bk, bn), lambda i, j, k: (k, j)),
    ]
    args = [x, w]
    res_split = 0
    if epilogue == "residual":
        nb = n // bn
        if isinstance(res, tuple):
            res_a, res_b = res
            res_split = res_a.shape[0] // bm
            assert res_a.shape[0] % bm == 0 and res_a.shape[0] + res_b.shape[0] == m
            in_specs += [
                pl.BlockSpec((bm, bn), lambda i, j, k: (jnp.minimum(i, res_split - 1),
                                                        jnp.where(i < res_split, j, nb - 1))),
                pl.BlockSpec((bm, bn), lambda i, j, k: (jnp.maximum(i - res_split, 0),
                                                        jnp.where(i < res_split, 0, j))),
            ]
            args += [res_a, res_b]
        else:
            in_specs.append(pl.BlockSpec((bm, bn), lambda i, j, k: (i, j)))
            args.append(res)
        in_specs.append(pl.BlockSpec((None, None, 1, bn),
                                     lambda i, j, k: (layer, groups.of_tile(i, bm), 0, gate_chunk * nb + j)))
        args.append(mod)

    out_specs = [pl.BlockSpec((bm, bn), lambda i, j, k: (i, j))]
    out_shape = [jax.ShapeDtypeStruct((m, n), out_dtype)]
    n_steps = (m // bm) * nj * nk
    cast_bytes = 0
    for c in casts:
        rows, cols = c.shape
        assert cols % CAST_TILE_COLS == 0 and c.col0 % CAST_TILE_COLS == 0
        tiles_c = cols // CAST_TILE_COLS
        tile_rows = next(tr for tr in (128, 256, 512, 1024, 2048)
                         if rows % tr == 0 and (rows // tr) * tiles_c <= n_steps)
        n_tiles = (rows // tile_rows) * tiles_c
        cast_bytes += 2 * (4 + 2) * tile_rows * CAST_TILE_COLS
        assert cast_bytes <= CAST_VMEM_BUDGET_BYTES

        def tile_of(i, j, k, n_tiles=n_tiles, tiles_c=tiles_c):
            t = jnp.minimum((i * nj + j) * nk + k, n_tiles - 1)
            return t // tiles_c, t % tiles_c

        def src_tile_of(i, j, k, c=c, tile_of=tile_of):
            r, t = tile_of(i, j, k)
            return c.lead, r, c.col0 // CAST_TILE_COLS + t

        in_specs.append(pl.BlockSpec((None, tile_rows, CAST_TILE_COLS), src_tile_of))
        args.append(c.src)
        out_specs.append(pl.BlockSpec((tile_rows, CAST_TILE_COLS), tile_of))
        out_shape.append(jax.ShapeDtypeStruct((rows, cols), BF16))

    outs = pl.pallas_call(
        functools.partial(_mm_kernel, nk=nk, epilogue=epilogue, n_res=n_res, res_split=res_split,
                          n_casts=len(casts)),
        grid=(m // bm, nj, nk),
        in_specs=in_specs,
        out_specs=out_specs,
        out_shape=out_shape,
        compiler_params=_params(("arbitrary", "arbitrary", "arbitrary")),
        name="matmul_" + epilogue,
    )(*args)
    return outs[0] if not casts else outs


class _ScanLayout(NamedTuple):
    n_ctx: int
    per_ctx: int
    per_lat: int
    total: int
    batch: int
    reverse: bool

    def pos(self, g):
        r = self.total - 1 - g if self.reverse else g
        is_lat = r >= self.n_ctx
        rl = jnp.maximum(r - self.n_ctx, 0)
        seq_ctx = jnp.minimum(r // self.per_ctx, self.batch - 1)
        seq_lat = rl // self.per_lat
        pos = jnp.where(is_lat, rl % self.per_lat, r % self.per_ctx)
        per = jnp.where(is_lat, self.per_lat, self.per_ctx)
        start, stop = (per - 1, 0) if self.reverse else (0, per - 1)
        return r, is_lat, seq_ctx, seq_lat, pos == start, pos == stop


def _level_ref(c, half, reverse):
    rows, width = c.shape
    blk = 2 * half
    idx = half if reverse else half - 1
    assert blk % V7X_SUBLANES == 0
    c3 = c.reshape(rows // blk, blk, width)
    ref = jnp.broadcast_to(c3[:, idx:idx + 1, :], c3.shape)
    return ref.reshape(rows, width)


def _exp2_neg_abs(d):
    bits = lax.bitcast_convert_type(d, jnp.uint32) | jnp.uint32(0x80000000)
    return jnp.exp2(lax.bitcast_convert_type(bits, F32))


def _level_weights(c, fg, level, reverse):
    rows = c.shape[0]
    if level > 2:
        w = _exp2_neg_abs(c - _level_ref(c, 1 << (level - 1), reverse))
        return w, w
    if level == 1:
        return fg, None
    pos = lax.broadcasted_iota(jnp.int32, c.shape, 0) & 3
    prev = pltpu.roll(fg, 1, 0)
    nxt = pltpu.roll(fg, rows - 1, 0)
    if reverse:
        return fg * jnp.where(pos == 0, nxt, 1.0), jnp.where(pos == 3, prev, 1.0)
    return fg * jnp.where(pos == 3, prev, 1.0), jnp.where(pos == 0, nxt, 1.0)


def _hg_scan_kernel(*refs, chunk, heads, dk, dv, layer, lay, fuse_out):
    if fuse_out:
        (q_ref, v_ref, f_ref, lbl_ref, s0_ref, ofwd_ref, g_ref, gn_ref,
         o_ref, sfin_ref, st_ref) = refs
    else:
        q_ref, v_ref, f_ref, lbl_ref, s0_ref, o_ref, sfin_ref, st_ref = refs
    reverse = lay.reverse
    _, is_lat, _, _, first, last = lay.pos(pl.program_id(1))
    n_levels = chunk.bit_length() - 1
    low_levels = min(n_levels, V7X_SUBLANES.bit_length() - 1)

    @pl.when(jnp.logical_and(first, is_lat))
    def _():
        for j in range(heads):
            st_ref[j] = s0_ref[0, j].T

    @pl.when(jnp.logical_and(first, jnp.logical_not(is_lat)))
    def _():
        st_ref[...] = jnp.zeros(st_ref.shape, F32)

    lbl = lbl_ref[...]
    e = jnp.exp(lbl - jnp.max(lbl, axis=0, keepdims=True))
    lb = jnp.sum(e[:layer + 1], axis=0, keepdims=True) / jnp.sum(e, axis=0, keepdims=True)

    f = f_ref[...]
    e_f = _exp2_neg_abs(f * LOG2E)
    r = 1.0 / (1.0 + e_f)
    fg_all = lb + (1.0 - lb) * jnp.where(f >= 0.0, r, e_f * r)
    k_all = 1.0 - fg_all
    log2f = jnp.log2(fg_all)

    row = lax.broadcasted_iota(jnp.int32, (chunk, chunk), 0)
    col = lax.broadcasted_iota(jnp.int32, (chunk, chunk), 1)
    before = (col > row) if reverse else (col < row)
    tri = jnp.logical_or(before, col == row).astype(BF16)
    hi = log2f.astype(BF16)
    r1 = log2f - hi.astype(F32)
    mid = r1.astype(BF16)
    lo = (r1 - mid.astype(F32)).astype(BF16)
    c_all = _dot(tri, hi) + _dot(tri, mid) + _dot(tri, lo)

    x = row ^ col
    lvl = jnp.zeros((chunk, chunk), jnp.int32)
    for b in range(n_levels):
        lvl = lvl + (x >= (1 << b)).astype(jnp.int32)
    lvl = jnp.where(jnp.logical_or(before, x == 0), lvl, -1)

    end = 0 if reverse else chunk - 1
    if fuse_out:
        gn = gn_ref[...]

    def pair_weights(j):
        sl = slice(j * dk, (j + 1) * dk)
        q = _silu(q_ref[:, sl])
        k = k_all[:, sl]
        c = c_all[:, sl]
        fg = fg_all[:, sl]
        q_bf, k_bf = q.astype(BF16), k.astype(BF16)
        a = jnp.where(lvl == 0, _dot_nt(q_bf, k_bf), 0.0)
        for level in range(1, low_levels + 1):
            w_q, w_k = _level_weights(c, fg, level, reverse)
            a_l = _dot_nt((q * w_q).astype(BF16), k_bf if w_k is None else (k * w_k).astype(BF16))
            a = jnp.where(lvl == level, a_l, a)
        for level in range(low_levels + 1, n_levels + 1):
            half = 1 << (level - 1)
            w, _ = _level_weights(c, fg, level, reverse)
            q_rows, k_rows = [], []
            for b0 in range(0, chunk, 2 * half):
                early, late = (b0 + half, b0) if reverse else (b0, b0 + half)
                q_rows.append(q[late:late + half] * w[late:late + half])
                k_scaled = k[early:early + half] * w[early:early + half]
                k_rows += [k[late:late + half], k_scaled] if reverse else [k_scaled, k[late:late + half]]
            p = _dot_nt(jnp.concatenate(q_rows, axis=0).astype(BF16),
                        jnp.concatenate(k_rows, axis=0).astype(BF16))
            pieces = []
            for b, b0 in enumerate(range(0, chunk, 2 * half)):
                early, late = (b0 + half, b0) if reverse else (b0, b0 + half)
                upd = jnp.where(lvl[late:late + half] == level, p[b * half:(b + 1) * half], a[late:late + half])
                pieces += [upd, a[early:early + half]] if reverse else [a[early:early + half], upd]
            a = jnp.concatenate(pieces, axis=0)
        return q, k, c, a

    def outputs_and_state(j, q, k, c, a):
        sv = slice(j * dv, (j + 1) * dv)
        v = v_ref[:, sv].astype(BF16)
        c_end = c[end:end + 1, :]
        st = st_ref[j]
        o = _dot(a.astype(BF16), v) + _dot_nt((q * jnp.exp2(c)).astype(BF16), st.astype(BF16))
        if fuse_out:
            o = o + ofwd_ref[:, sv]
            y = o * lax.rsqrt(jnp.mean(o * o, axis=-1, keepdims=True) + EPS) * gn
            o = y * _silu(g_ref[:, sv])
        o_ref[:, sv] = o.astype(o_ref.dtype)
        k_dec = (k * jnp.exp2(c_end - c)).astype(BF16)
        st_ref[j] = st * jnp.exp2(c_end) + _dot_tn(v, k_dec)

    pending = [pair_weights(j) for j in range(min(HG_LOOKAHEAD, heads))]
    for j in range(heads):
        if j + HG_LOOKAHEAD < heads:
            pending.append(pair_weights(j + HG_LOOKAHEAD))
        outputs_and_state(j, *pending.pop(0))

    @pl.when(jnp.logical_and(last, jnp.logical_not(is_lat)))
    def _():
        for j in range(heads):
            sfin_ref[0, j] = st_ref[j].T


def _hg_scan(proj_qv, proj_fg, lb_logits_dir, s0_lat, *, groups, batch, seq, n_heads, dk, dv, d, reverse, layer,
             o_fwd=None, g_norm=None):
    chunk = _tile(math.gcd(seq, groups.dec_seq), HG_CHUNK)
    heads = _tile(n_heads, HG_HEADS_PER_STEP)
    assert dk == dv and d == n_heads * dk
    w = heads * dk
    segs = d // w
    lay = _ScanLayout(n_ctx=groups.ctx_rows // chunk, per_ctx=seq // chunk, per_lat=groups.dec_seq // chunk,
                      total=groups.rows // chunk, batch=batch, reverse=reverse)
    fuse_out = o_fwd is not None

    def seg_spec(seg):
        return pl.BlockSpec((chunk, w), lambda h, g: (lay.pos(g)[0], seg * segs + h))

    n_lb = lb_logits_dir.shape[0]
    in_specs = [seg_spec(0), seg_spec(1), seg_spec(1 if reverse else 0),
                pl.BlockSpec((n_lb, w), lambda h, g: (0, h)),
                pl.BlockSpec((1, heads, dk, dv), lambda h, g: (lay.pos(g)[3], h, 0, 0))]
    args = [proj_qv, proj_qv, proj_fg, lb_logits_dir, s0_lat]
    if fuse_out:
        in_specs += [seg_spec(0), seg_spec(2), pl.BlockSpec((1, dv), lambda h, g: (0, 0))]
        args += [o_fwd, proj_fg, g_norm.reshape(1, dv)]
    o, s_fin = pl.pallas_call(
        functools.partial(_hg_scan_kernel, chunk=chunk, heads=heads, dk=dk, dv=dv,
                          layer=layer, lay=lay, fuse_out=fuse_out),
        grid=(n_heads // heads, lay.total),
        in_specs=in_specs,
        out_specs=[
            pl.BlockSpec((chunk, w), lambda h, g: (lay.pos(g)[0], h)),
            pl.BlockSpec((1, heads, dk, dv), lambda h, g: (lay.pos(g)[2], h, 0, 0)),
        ],
        out_shape=[
            jax.ShapeDtypeStruct((groups.rows, d), BF16 if fuse_out else F32),
            jax.ShapeDtypeStruct((batch, n_heads, dk, dv), F32),
        ],
        scratch_shapes=[pltpu.VMEM((heads, dv, dk), F32)],
        compiler_params=_params(("parallel", "arbitrary")),
        name="hgrn2_scan_" + ("bwd" if reverse else "fwd"),
    )(*args)
    return o, s_fin


def _rope_tables(seq, hd):
    half = hd // 2
    rows = seq // GRID_W
    r_pos = jnp.repeat(jnp.arange(rows), GRID_W).astype(F32)
    c_pos = jnp.tile(jnp.arange(GRID_W), rows).astype(F32)
    inv_freq = ROPE_BASE ** (-jnp.arange(0, half, 2, dtype=F32) / half)
    ang_r = r_pos[:, None] * inv_freq
    ang_c = c_pos[:, None] * inv_freq
    cos = jnp.concatenate([jnp.cos(ang_r)] * 2 + [jnp.cos(ang_c)] * 2, axis=-1)
    sin = jnp.concatenate([-jnp.sin(ang_r), jnp.sin(ang_r), -jnp.sin(ang_c), jnp.sin(ang_c)], axis=-1)
    return cos, sin


def _rope_tile(x, cos, sin):
    hd = x.shape[1]
    quarter = hd // 4
    lane = lax.broadcasted_iota(jnp.int32, x.shape, 1)
    first = (lane % (2 * quarter)) < quarter
    partner = jnp.where(first, pltpu.roll(x, hd - quarter, 1), pltpu.roll(x, quarter, 1))
    return x * cos + partner * sin


def _attn_kernel(*refs, hd, heads, tq, sub, lookahead, past, scale, lam_init, latent):
    if latent:
        (lam_ref, q_ref, k_ref, v_ref, kc_ref, vc_ref, cos_ref, sin_ref, g_ref, _, ksrc_ref, vsrc_ref,
         o_ref, kdst_ref, vdst_ref, kall_ref, vall_ref) = refs
        kdst_ref[...] = ksrc_ref[...]
        vdst_ref[...] = vsrc_ref[...]
    else:
        lam_ref, q_ref, k_ref, v_ref, g_ref, _, o_ref = refs
    w = 2 * hd
    lm = lam_ref[...]
    lam = (jnp.exp(jnp.sum(lm[0:1] * lm[1:2], axis=-1, keepdims=True))
           - jnp.exp(jnp.sum(lm[2:3] * lm[3:4], axis=-1, keepdims=True)) + lam_init)
    g = g_ref[...]

    if latent:
        i = pl.program_id(2)

        @pl.when(i == 0)
        def _():
            kall_ref[0:past, :] = kc_ref[...].astype(BF16)
            vall_ref[0:past, :] = vc_ref[...].astype(BF16)
            vall_ref[past:, :] = v_ref[...].astype(BF16)
            for half in range(2):
                hs = slice(half * hd, (half + 1) * hd)
                kall_ref[past:, hs] = _rope_tile(k_ref[:, hs], cos_ref[...], sin_ref[...]).astype(BF16)

    def softmax_parts(qh, kh):
        s = _dot_nt(qh, kh)
        p = jnp.exp2(s - jnp.max(s, axis=-1, keepdims=True))
        return p.astype(BF16), 1.0 / jnp.sum(p, axis=-1, keepdims=True)

    units = [(j, r0) for j in range(heads) for r0 in range(0, tq, sub)]

    def probabilities(j, r0):
        parts = []
        for half in range(2):
            hs = slice(j * w + half * hd, j * w + (half + 1) * hd)
            qh = q_ref[r0:r0 + sub, hs]
            if latent:
                rows = pl.ds(pl.multiple_of(i * tq + r0, sub), sub)
                qh = _rope_tile(qh, cos_ref[rows, :], sin_ref[rows, :])
                kh = kall_ref[:, hs]
            else:
                kh = k_ref[:, hs].astype(BF16)
            parts.append(softmax_parts((qh * (scale * LOG2E)).astype(BF16), kh))
        return parts

    def output(j, r0, parts):
        (p1, r1), (p2, r2) = parts
        js = slice(j * w, (j + 1) * w)
        vj = vall_ref[...] if latent else v_ref[:, js].astype(BF16)
        o = r1 * _dot(p1, vj) - (lam * r2) * _dot(p2, vj)
        y = o * lax.rsqrt(jnp.mean(o * o, axis=-1, keepdims=True) + EPS) * g
        o_ref[r0:r0 + sub, js] = (y * (1.0 - lam_init)).astype(o_ref.dtype)

    pending = [probabilities(*u) for u in units[:lookahead]]
    for n, unit in enumerate(units):
        if n + lookahead < len(units):
            pending.append(probabilities(*units[n + lookahead]))
        output(*unit, pending.pop(0))


def _diff_attention(proj, lam_params, subln_g, *, batch, seq, row_off, n_heads, hd, lam_init,
                    cache_k=None, cache_v=None, prev=None):
    latent = cache_k is not None
    w = 2 * hd
    d = n_heads * w
    tq = _tile(seq, ATTN_TQ)
    nq = seq // tq
    heads = 1 if latent else _tile(n_heads, ATTN_CTX_HEADS_PER_STEP)
    assert row_off % seq == 0
    q_off, kv_off = row_off // tq, row_off // seq
    segs = n_heads // heads
    scale = hd ** -0.5
    past = cache_k.shape[0] // batch if latent else 0

    def kv_spec(rows, seg=0, off=0):
        return pl.BlockSpec((rows, heads * w), lambda b, h, i: (off + b, seg * segs + h))

    in_specs = [pl.BlockSpec(lam_params.shape, lambda b, h, i: (0, 0)),
                pl.BlockSpec((tq, heads * w), lambda b, h, i: (q_off + b * nq + i, h)),
                kv_spec(seq, 1, kv_off), kv_spec(seq, 2, kv_off)]
    args = [lam_params, proj, proj, proj]
    scratch = []
    if latent:
        cos, sin = _rope_tables(seq, hd)
        table = pl.BlockSpec((seq, hd), lambda b, h, i: (0, 0))
        in_specs += [kv_spec(past), kv_spec(past), table, table]
        args += [cache_k, cache_v, cos, sin]
        scratch = [pltpu.VMEM((past + seq, w), BF16), pltpu.VMEM((past + seq, w), BF16)]
    in_specs.append(pl.BlockSpec((1, w), lambda b, h, i: (0, 0)))
    args.append(subln_g.reshape(1, w))
    out_specs = [pl.BlockSpec((tq, heads * w), lambda b, h, i: (q_off + b * nq + i, h))]
    out_shape = [jax.ShapeDtypeStruct((proj.shape[0], d), BF16)]
    in_specs.append(pl.BlockSpec(memory_space=pl.ANY))
    args.append(prev)
    aliases = {len(args) - 1: 0}
    semantics = ("parallel", "parallel", "arbitrary")
    if latent:
        n_steps = batch * (n_heads // heads) * nq
        slab = row_off // n_steps
        assert row_off % n_steps == 0 and slab % V7X_SUBLANES == 0

        def slab_of(b, h, i):
            return (b * (n_heads // heads) + h) * nq + i

        in_specs += [pl.BlockSpec((slab, d), lambda b, h, i: (slab_of(b, h, i), 1)),
                     pl.BlockSpec((slab, d), lambda b, h, i: (slab_of(b, h, i), 2))]
        args += [proj, proj]
        out_specs += [pl.BlockSpec((slab, d), lambda b, h, i: (slab_of(b, h, i), 0))] * 2
        out_shape += [jax.ShapeDtypeStruct((row_off, d), F32)] * 2
        semantics = ("arbitrary", "arbitrary", "arbitrary")
    outs = pl.pallas_call(
        functools.partial(_attn_kernel, hd=hd, heads=heads, tq=tq, sub=_tile(tq, ATTN_SUB_TQ),
                          lookahead=1 if latent else 0, past=past, scale=scale, lam_init=lam_init,
                          latent=latent),
        grid=(batch, n_heads // heads, nq),
        in_specs=in_specs,
        out_specs=out_specs,
        out_shape=out_shape,
        scratch_shapes=scratch,
        input_output_aliases=aliases,
        compiler_params=_params(semantics),
        name="diff_attention_" + ("latent" if latent else "context"),
    )(*args)
    return outs if latent else outs[0]


def kernel(x_prompt, x_sample, state_hgrn_fwd, state_hgrn_bwd, cache_attn_k, cache_attn_v, c, c_ctx,
           w_mod, b_mod, norm_g, w_ff1, w_ff2, hg_w_in, hg_w_out, hg_g_norm, hg_lb_logits,
           da_w_in, da_w_out, da_subln_g, da_lambda, final_norm_g):
    batch, seq, d = x_prompt.shape
    dec_batch, dec_seq, _ = x_sample.shape
    depth = w_mod.shape[0]
    _, _, hg_heads, hg_dk, hg_dv = state_hgrn_fwd.shape
    _, _, past_len, da_halves, da_hd = cache_attn_k.shape
    da_heads = da_halves // 2
    ctx_rows = batch * seq
    lat_rows = dec_batch * dec_seq
    groups = _Groups(ctx_rows, dec_batch, dec_seq)

    x = (x_prompt.reshape(ctx_rows, d), x_sample.reshape(lat_rows, d))

    n_groups = 1 + dec_batch
    g8 = -(-n_groups // V7X_SUBLANES) * V7X_SUBLANES
    cond = jnp.concatenate([c_ctx[None, :], c, jnp.zeros((g8 - n_groups, d), F32)], axis=0)
    mod = _modulation(cond, w_mod, b_mod).reshape(depth, g8, 1, 6 * d)
    SH1, SC1, GT1, SH2, SC2, GT2 = range(6)

    def mixer_weights(layer):
        j = layer // N_MIXERS
        if layer % N_MIXERS == 0:
            return [_Cast(hg_w_in, j, 0, 2 * d), _Cast(hg_w_in, j, 2 * d, 3 * d)], _Cast(hg_w_out, j)
        return [_Cast(da_w_in, j)], _Cast(da_w_out, j)

    first_parts, _ = mixer_weights(0)
    w_in = [_cast_weight(first_parts[0])] + first_parts[1:]
    new_sf, new_sb, new_k, new_v = [], [], [], []
    for layer in range(depth):
        h = _norm_mod(x, norm_g[layer, 0], mod, layer, SC1, SH1, groups)
        j = layer // N_MIXERS
        out_casts = (mixer_weights(layer)[1], _Cast(w_ff1, layer))
        projs = []
        for n in range(len(w_in)):
            last = n + 1 == len(w_in)
            casts = out_casts if last else tuple(w for w in w_in[n + 1:n + 2] if isinstance(w, _Cast))
            out = _matmul(h, w_in[n], F32, casts=casts)
            if last:
                out, w_out, w1 = out
            elif casts:
                out, w_in[n + 1] = out
            projs.append(out)
        if layer % N_MIXERS == 0:
            scan = dict(groups=groups, batch=batch, seq=seq, n_heads=hg_heads, dk=hg_dk, dv=hg_dv, d=d,
                        layer=layer)
            o_fwd, s_fwd = _hg_scan(*projs, hg_lb_logits[0], state_hgrn_fwd[:, j], reverse=False, **scan)
            mixed, s_bwd = _hg_scan(*projs, hg_lb_logits[1], state_hgrn_bwd[:, j], reverse=True,
                                    o_fwd=o_fwd, g_norm=hg_g_norm[j], **scan)
            new_sf.append(s_fwd)
            new_sb.append(s_bwd)
        else:
            proj, = projs
            lam_init = 0.8 - 0.6 * math.exp(-0.3 * layer)
            att = dict(n_heads=da_heads, hd=da_hd, lam_init=lam_init)
            mixed = _diff_attention(proj, da_lambda[j], da_subln_g[j], batch=batch, seq=seq, row_off=0,
                                    prev=h, **att)
            mixed, k_ctx, v_ctx = _diff_attention(
                proj, da_lambda[j], da_subln_g[j], batch=dec_batch, seq=dec_seq, row_off=ctx_rows,
                cache_k=cache_attn_k[:, j].reshape(dec_batch * past_len, d),
                cache_v=cache_attn_v[:, j].reshape(dec_batch * past_len, d), prev=mixed, **att)
            new_k.append(k_ctx.reshape(batch, seq, da_halves, da_hd))
            new_v.append(v_ctx.reshape(batch, seq, da_heads, 2 * da_hd))
        x = _matmul(mixed, w_out, F32, "residual", res=x, mod=mod, layer=layer, gate_chunk=GT1, groups=groups)
        h = _norm_mod(x, norm_g[layer, 1], mod, layer, SC2, SH2, groups)
        next_parts = mixer_weights(layer + 1)[0] if layer + 1 < depth else []
        u, w2, *w_in = _matmul(h, w1, BF16, "relu2", casts=(_Cast(w_ff2, layer), *next_parts))
        x = _matmul(u, w2, F32, "residual", res=x, mod=mod, layer=layer, gate_chunk=GT2, groups=groups)

    y_prompt = _final_norm(x, final_norm_g, 0, ctx_rows).reshape(batch, seq, d)
    y_sample = _final_norm(x, final_norm_g, ctx_rows, lat_rows).reshape(dec_batch, dec_seq, d)
    return (y_prompt, y_sample, jnp.stack(new_sf, axis=1), jnp.stack(new_sb, axis=1),
            jnp.stack(new_k, axis=1), jnp.stack(new_v, axis=1))
```

```python
import functools
import math
from typing import NamedTuple

import jax
import jax.numpy as jnp
from jax import lax
from jax.experimental import pallas as pl
from jax.experimental.pallas import tpu as pltpu

F32 = jnp.float32
BF16 = jnp.bfloat16

GRID_W = 64
EPS = 1e-6
ROPE_BASE = 10000.0
N_MIXERS = 2

V7X_SUBLANES = 8
V7X_LANES = 128
V7X_VMEM_BYTES = 64 * 1024 * 1024
COMPILER_RESERVED_VMEM_BYTES = 4 * 1024 * 1024
V7X_VMEM_LIMIT_BYTES = V7X_VMEM_BYTES - COMPILER_RESERVED_VMEM_BYTES

MM_BM = 1024
MM_BN = 1024
MM_VMEM_BUDGET_BYTES = 52 * 1024 * 1024
CAST_VMEM_BUDGET_BYTES = 10 * 1024 * 1024
CAST_TILE_COLS = 1024
ROW_TILE = 512
MOD_BN = 1024
HG_CHUNK = 128
HG_HEADS_PER_STEP = 32
HG_LOOKAHEAD = 1
ATTN_TQ = 1024
ATTN_SUB_TQ = 256
ATTN_CTX_HEADS_PER_STEP = 8

LOG2E = 1.4426950408889634


def _tile(full, want):
    t = min(full, want)
    assert full % t == 0, (full, want)
    return t


def _params(sem):
    return pltpu.CompilerParams(dimension_semantics=sem, vmem_limit_bytes=V7X_VMEM_LIMIT_BYTES)


def _silu(x):
    half = 0.5 * x
    return half + half * jnp.tanh(half)


def _dot(a, b):
    return jnp.dot(a, b, preferred_element_type=F32)


def _dot_nt(a, b):
    return lax.dot_general(a, b, (((1,), (1,)), ((), ())), preferred_element_type=F32)


def _dot_tn(a, b):
    return lax.dot_general(a, b, (((0,), (0,)), ((), ())), preferred_element_type=F32)


class _Groups:
    def __init__(self, ctx_rows, dec_batch, dec_seq):
        self.ctx_rows = ctx_rows
        self.dec_batch = dec_batch
        self.dec_seq = dec_seq
        self.rows = ctx_rows + dec_batch * dec_seq

    def of_tile(self, i, tile):
        assert self.ctx_rows % tile == 0 and self.dec_seq % tile == 0
        g = jnp.int32(0)
        for b in range(self.dec_batch):
            g = g + (i >= (self.ctx_rows + b * self.dec_seq) // tile).astype(jnp.int32)
        return g


def _mod_kernel(c_ref, w_ref, b_ref, o_ref):
    a = _silu(c_ref[...]).astype(BF16)
    o_ref[...] = _dot(a, w_ref[...].astype(BF16)) + b_ref[...]


def _modulation(cond, w_mod, b_mod, layer):
    g8, d = cond.shape
    n = w_mod.shape[2]
    bn = _tile(n, MOD_BN)
    return pl.pallas_call(
        _mod_kernel,
        grid=(n // bn,),
        in_specs=[
            pl.BlockSpec((g8, d), lambda j: (0, 0)),
            pl.BlockSpec((None, d, bn), lambda j: (layer, 0, j)),
            pl.BlockSpec((None, 1, bn), lambda j: (layer, 0, j)),
        ],
        out_specs=pl.BlockSpec((g8, bn), lambda j: (0, j)),
        out_shape=jax.ShapeDtypeStruct((g8, n), F32),
        compiler_params=_params(("arbitrary",)),
        name="adaln_modulation",
    )(cond, w_mod, b_mod)


def _norm_kernel(*refs, modulated, x_split):
    if modulated:
        *x_refs, g_ref, sc_ref, sh_ref, o_ref = refs
    else:
        *x_refs, g_ref, o_ref = refs
    x = x_refs[0][...]
    if len(x_refs) == 2:
        x = jnp.where(pl.program_id(0) < x_split, x, x_refs[1][...])
    y = x * lax.rsqrt(jnp.mean(x * x, axis=-1, keepdims=True) + EPS) * g_ref[...]
    if modulated:
        y = y * (1.0 + sc_ref[...]) + sh_ref[...]
    o_ref[...] = y.astype(o_ref.dtype)


def _norm_mod(x, g, mod, layer, sc_chunk, sh_chunk, groups):
    rows, d = groups.rows, g.shape[0]
    tm = _tile(math.gcd(groups.ctx_rows, groups.dec_seq), ROW_TILE)

    def vec(chunk):
        return pl.BlockSpec((None, None, 1, d), lambda i: (layer, groups.of_tile(i, tm), 0, chunk))

    if isinstance(x, tuple):
        x_split = x[0].shape[0] // tm
        x_specs = [pl.BlockSpec((tm, d), lambda i: (jnp.minimum(i, x_split - 1), 0)),
                   pl.BlockSpec((tm, d), lambda i: (jnp.maximum(i - x_split, 0), 0))]
        xs = list(x)
    else:
        x_split = 0
        x_specs = [pl.BlockSpec((tm, d), lambda i: (i, 0))]
        xs = [x]
    return pl.pallas_call(
        functools.partial(_norm_kernel, modulated=True, x_split=x_split),
        grid=(rows // tm,),
        in_specs=x_specs + [
            pl.BlockSpec((1, d), lambda i: (0, 0)),
            vec(sc_chunk),
            vec(sh_chunk),
        ],
        out_specs=pl.BlockSpec((tm, d), lambda i: (i, 0)),
        out_shape=jax.ShapeDtypeStruct((rows, d), BF16),
        compiler_params=_params(("parallel",)),
        name="rmsnorm_modulate",
    )(*xs, g.reshape(1, d), mod, mod)


def _final_norm(x, g, row_off, rows):
    d = x.shape[1]
    tm = _tile(math.gcd(rows, row_off) if row_off else rows, ROW_TILE)
    off = row_off // tm
    return pl.pallas_call(
        functools.partial(_norm_kernel, modulated=False, x_split=0),
        grid=(rows // tm,),
        in_specs=[
            pl.BlockSpec((tm, d), lambda i: (i + off, 0)),
            pl.BlockSpec((1, d), lambda i: (0, 0)),
        ],
        out_specs=pl.BlockSpec((tm, d), lambda i: (i, 0)),
        out_shape=jax.ShapeDtypeStruct((rows, d), F32),
        compiler_params=_params(("parallel",)),
        name="final_rmsnorm",
    )(x, g.reshape(1, d))


def _mm_kernel(*refs, nk, epilogue, n_res, res_split, n_casts):
    x_ref, w_ref = refs[:2]
    n_in = 2 + (n_res + 1 if epilogue == "residual" else 0)
    res_refs = refs[2:2 + n_res]
    gate_ref = refs[n_in - 1]
    o_ref = refs[n_in + n_casts]
    for src_ref, dst_ref in zip(refs[n_in:n_in + n_casts], refs[n_in + n_casts + 1:]):
        dst_ref[...] = src_ref[...].astype(dst_ref.dtype)

    def finish(acc):
        if epilogue == "relu2":
            r = jnp.maximum(acc, 0.0)
            acc = r * r
        elif epilogue == "residual":
            res = res_refs[0][...]
            if len(res_refs) == 2:
                res = jnp.where(pl.program_id(0) < res_split, res, res_refs[1][...])
            acc = res + gate_ref[...] * acc
        return acc

    def product():
        return _dot(x_ref[...], w_ref[...])

    if nk == 1:
        o_ref[...] = finish(product()).astype(o_ref.dtype)
        return

    k = pl.program_id(2)

    @pl.when(k == 0)
    def _():
        o_ref[...] = product()

    @pl.when(jnp.logical_and(k > 0, k < nk - 1))
    def _():
        o_ref[...] += product()

    @pl.when(k == nk - 1)
    def _():
        o_ref[...] = finish(o_ref[...] + product())


class _Cast(NamedTuple):
    src: jax.Array
    lead: int
    col0: int = 0
    ncols: int = 0

    @property
    def shape(self):
        return self.src.shape[1], self.ncols or self.src.shape[2]


def _cast_kernel(s_ref, o_ref):
    o_ref[...] = s_ref[...].astype(o_ref.dtype)


def _cast_weight(c):
    rows, cols = c.shape
    tr, tc = _tile(rows, CAST_TILE_COLS), _tile(cols, 2 * CAST_TILE_COLS)
    assert c.col0 % tc == 0
    j0 = c.col0 // tc
    return pl.pallas_call(
        _cast_kernel,
        grid=(rows // tr, cols // tc),
        in_specs=[pl.BlockSpec((None, tr, tc), lambda i, j: (c.lead, i, j0 + j))],
        out_specs=pl.BlockSpec((tr, tc), lambda i, j: (i, j)),
        out_shape=jax.ShapeDtypeStruct((rows, cols), BF16),
        compiler_params=_params(("parallel", "parallel")),
        name="cast_weight",
    )(c.src)


def _matmul(x, w, out_dtype, epilogue="plain", *, res=None, mod=None, layer=None, gate_chunk=None,
            groups=None, casts=()):
    m, kdim = x.shape
    n = w.shape[1]
    bm = _tile(m, MM_BM)
    n_res = 0 if epilogue != "residual" else (2 if isinstance(res, tuple) else 1)
    out_bytes = jnp.dtype(out_dtype).itemsize + 4 * n_res

    def fits(bn, bk):
        return 2 * (2 * bk * (bm + bn) + bm * bn * out_bytes) + 4 * bm * bn <= MM_VMEM_BUDGET_BYTES

    bn_full = _tile(n, MM_BN)
    candidates = [(bn_full, kdim), (max(bn_full // 2, V7X_LANES), kdim)]
    candidates += [(bn_full, kdim // s) for s in (2, 4, 8, 16, 32)]
    bn, bk = next((bn_, bk_) for bn_, bk_ in candidates if kdim % bk_ == 0 and fits(bn_, bk_))
    nk = kdim // bk
    nj = n // bn
    assert nk == 1 or out_dtype == F32
    in_specs = [
        pl.BlockSpec((bm, bk), lambda i, j, k: (i, k)),
        pl.BlockSpec((bk, bn), lambda i, j, k: (k, j)),
    ]
    args = [x, w]
    res_split = 0
    if epilogue == "residual":
        nb = n // bn
        if isinstance(res, tuple):
            res_a, res_b = res
            res_split = res_a.shape[0] // bm
            assert res_a.shape[0] % bm == 0 and res_a.shape[0] + res_b.shape[0] == m
            in_specs += [
                pl.BlockSpec((bm, bn), lambda i, j, k: (jnp.minimum(i, res_split - 1),
                                                        jnp.where(i < res_split, j, nb - 1))),
                pl.BlockSpec((bm, bn), lambda i, j, k: (jnp.maximum(i - res_split, 0),
                                                        jnp.where(i < res_split, 0, j))),
            ]
            args += [res_a, res_b]
        else:
            in_specs.append(pl.BlockSpec((bm, bn), lambda i, j, k: (i, j)))
            args.append(res)
        in_specs.append(pl.BlockSpec((None, None, 1, bn),
                                     lambda i, j, k: (layer, groups.of_tile(i, bm), 0, gate_chunk * nb + j)))
        args.append(mod)

    out_specs = [pl.BlockSpec((bm, bn), lambda i, j, k: (i, j))]
    out_shape = [jax.ShapeDtypeStruct((m, n), out_dtype)]
    n_steps = (m // bm) * nj * nk
    cast_bytes = 0
    for c in casts:
        rows, cols = c.shape
        assert cols % CAST_TILE_COLS == 0 and c.col0 % CAST_TILE_COLS == 0
        tiles_c = cols // CAST_TILE_COLS
        tile_rows = next(tr for tr in (128, 256, 512, 1024, 2048)
                         if rows % tr == 0 and (rows // tr) * tiles_c <= n_steps)
        n_tiles = (rows // tile_rows) * tiles_c
        cast_bytes += 2 * (4 + 2) * tile_rows * CAST_TILE_COLS
        assert cast_bytes <= CAST_VMEM_BUDGET_BYTES

        def tile_of(i, j, k, n_tiles=n_tiles, tiles_c=tiles_c):
            t = jnp.minimum((i * nj + j) * nk + k, n_tiles - 1)
            return t // tiles_c, t % tiles_c

        def src_tile_of(i, j, k, c=c, tile_of=tile_of):
            r, t = tile_of(i, j, k)
            return c.lead, r, c.col0 // CAST_TILE_COLS + t

        in_specs.append(pl.BlockSpec((None, tile_rows, CAST_TILE_COLS), src_tile_of))
        args.append(c.src)
        out_specs.append(pl.BlockSpec((tile_rows, CAST_TILE_COLS), tile_of))
        out_shape.append(jax.ShapeDtypeStruct((rows, cols), BF16))

    outs = pl.pallas_call(
        functools.partial(_mm_kernel, nk=nk, epilogue=epilogue, n_res=n_res, res_split=res_split,
                          n_casts=len(casts)),
        grid=(m // bm, nj, nk),
        in_specs=in_specs,
        out_specs=out_specs,
        out_shape=out_shape,
        compiler_params=_params(("arbitrary", "arbitrary", "arbitrary")),
        name="matmul_" + epilogue,
    )(*args)
    return outs[0] if not casts else outs


class _ScanLayout(NamedTuple):
    n_ctx: int
    per_ctx: int
    per_lat: int
    total: int
    batch: int
    reverse: bool

    def pos(self, g):
        r = self.total - 1 - g if self.reverse else g
        is_lat = r >= self.n_ctx
        rl = jnp.maximum(r - self.n_ctx, 0)
        seq_ctx = jnp.minimum(r // self.per_ctx, self.batch - 1)
        seq_lat = rl // self.per_lat
        pos = jnp.where(is_lat, rl % self.per_lat, r % self.per_ctx)
        per = jnp.where(is_lat, self.per_lat, self.per_ctx)
        start, stop = (per - 1, 0) if self.reverse else (0, per - 1)
        return r, is_lat, seq_ctx, seq_lat, pos == start, pos == stop


def _level_ref(c, half, reverse):
    rows, width = c.shape
    blk = 2 * half
    idx = half if reverse else half - 1
    assert blk % V7X_SUBLANES == 0
    c3 = c.reshape(rows // blk, blk, width)
    ref = jnp.broadcast_to(c3[:, idx:idx + 1, :], c3.shape)
    return ref.reshape(rows, width)


def _exp2_neg_abs(d):
    bits = lax.bitcast_convert_type(d, jnp.uint32) | jnp.uint32(0x80000000)
    return jnp.exp2(lax.bitcast_convert_type(bits, F32))


def _level_weights(c, fg, level, reverse):
    rows = c.shape[0]
    if level > 2:
        w = _exp2_neg_abs(c - _level_ref(c, 1 << (level - 1), reverse))
        return w, w
    if level == 1:
        return fg, None
    pos = lax.broadcasted_iota(jnp.int32, c.shape, 0) & 3
    prev = pltpu.roll(fg, 1, 0)
    nxt = pltpu.roll(fg, rows - 1, 0)
    if reverse:
        return fg * jnp.where(pos == 0, nxt, 1.0), jnp.where(pos == 3, prev, 1.0)
    return fg * jnp.where(pos == 3, prev, 1.0), jnp.where(pos == 0, nxt, 1.0)


def _hg_scan_kernel(*refs, chunk, heads, dk, dv, layer, lay, fuse_out, mod_rider):
    refs = list(refs)
    q_ref, v_ref, f_ref, lbl_ref, s0_ref = refs[:5]
    del refs[:5]
    if fuse_out:
        ofwd_ref, g_ref, gn_ref = refs[:3]
        del refs[:3]
    if mod_rider:
        cond_ref, wmod_ref, bmod_ref = refs[:3]
        del refs[:3]
        o_ref, sfin_ref, modr_ref, st_ref = refs
        modr_ref[...] = _dot(_silu(cond_ref[...]).astype(BF16), wmod_ref[...].astype(BF16)) + bmod_ref[...]
    else:
        o_ref, sfin_ref, st_ref = refs
    reverse = lay.reverse
    _, is_lat, _, _, first, last = lay.pos(pl.program_id(1))
    n_levels = chunk.bit_length() - 1
    low_levels = min(n_levels, V7X_SUBLANES.bit_length() - 1)

    @pl.when(jnp.logical_and(first, is_lat))
    def _():
        for j in range(heads):
            st_ref[j] = s0_ref[0, j].T

    @pl.when(jnp.logical_and(first, jnp.logical_not(is_lat)))
    def _():
        st_ref[...] = jnp.zeros(st_ref.shape, F32)

    lbl = lbl_ref[...]
    e = jnp.exp(lbl - jnp.max(lbl, axis=0, keepdims=True))
    lb = jnp.sum(e[:layer + 1], axis=0, keepdims=True) / jnp.sum(e, axis=0, keepdims=True)

    f = f_ref[...]
    e_f = _exp2_neg_abs(f * LOG2E)
    r = 1.0 / (1.0 + e_f)
    fg_all = lb + (1.0 - lb) * jnp.where(f >= 0.0, r, e_f * r)
    k_all = 1.0 - fg_all
    log2f = jnp.log2(fg_all)

    row = lax.broadcasted_iota(jnp.int32, (chunk, chunk), 0)
    col = lax.broadcasted_iota(jnp.int32, (chunk, chunk), 1)
    before = (col > row) if reverse else (col < row)
    tri = jnp.logical_or(before, col == row).astype(BF16)
    hi = log2f.astype(BF16)
    r1 = log2f - hi.astype(F32)
    mid = r1.astype(BF16)
    lo = (r1 - mid.astype(F32)).astype(BF16)
    c_all = _dot(tri, hi) + _dot(tri, mid) + _dot(tri, lo)

    x = row ^ col
    lvl = jnp.zeros((chunk, chunk), jnp.int32)
    for b in range(n_levels):
        lvl = lvl + (x >= (1 << b)).astype(jnp.int32)
    lvl = jnp.where(jnp.logical_or(before, x == 0), lvl, -1)

    end = 0 if reverse else chunk - 1
    if fuse_out:
        gn = gn_ref[...]

    def pair_weights(j):
        sl = slice(j * dk, (j + 1) * dk)
        q = _silu(q_ref[:, sl])
        k = k_all[:, sl]
        c = c_all[:, sl]
        fg = fg_all[:, sl]
        q_bf, k_bf = q.astype(BF16), k.astype(BF16)
        a = jnp.where(lvl == 0, _dot_nt(q_bf, k_bf), 0.0)
        for level in range(1, low_levels + 1):
            w_q, w_k = _level_weights(c, fg, level, reverse)
            a_l = _dot_nt((q * w_q).astype(BF16), k_bf if w_k is None else (k * w_k).astype(BF16))
            a = jnp.where(lvl == level, a_l, a)
        for level in range(low_levels + 1, n_levels + 1):
            half = 1 << (level - 1)
            w, _ = _level_weights(c, fg, level, reverse)
            q_rows, k_rows = [], []
            for b0 in range(0, chunk, 2 * half):
                early, late = (b0 + half, b0) if reverse else (b0, b0 + half)
                q_rows.append(q[late:late + half] * w[late:late + half])
                k_scaled = k[early:early + half] * w[early:early + half]
                k_rows += [k[late:late + half], k_scaled] if reverse else [k_scaled, k[late:late + half]]
            p = _dot_nt(jnp.concatenate(q_rows, axis=0).astype(BF16),
                        jnp.concatenate(k_rows, axis=0).astype(BF16))
            pieces = []
            for b, b0 in enumerate(range(0, chunk, 2 * half)):
                early, late = (b0 + half, b0) if reverse else (b0, b0 + half)
                upd = jnp.where(lvl[late:late + half] == level, p[b * half:(b + 1) * half], a[late:late + half])
                pieces += [upd, a[early:early + half]] if reverse else [a[early:early + half], upd]
            a = jnp.concatenate(pieces, axis=0)
        return q, k, c, a

    def outputs_and_state(j, q, k, c, a):
        sv = slice(j * dv, (j + 1) * dv)
        v = v_ref[:, sv].astype(BF16)
        c_end = c[end:end + 1, :]
        st = st_ref[j]
        o = _dot(a.astype(BF16), v) + _dot_nt((q * jnp.exp2(c)).astype(BF16), st.astype(BF16))
        if fuse_out:
            o = o + ofwd_ref[:, sv]
            y = o * lax.rsqrt(jnp.mean(o * o, axis=-1, keepdims=True) + EPS) * gn
            o = y * _silu(g_ref[:, sv])
        o_ref[:, sv] = o.astype(o_ref.dtype)
        k_dec = (k * jnp.exp2(c_end - c)).astype(BF16)
        st_ref[j] = st * jnp.exp2(c_end) + _dot_tn(v, k_dec)

    pending = [pair_weights(j) for j in range(min(HG_LOOKAHEAD, heads))]
    for j in range(heads):
        if j + HG_LOOKAHEAD < heads:
            pending.append(pair_weights(j + HG_LOOKAHEAD))
        outputs_and_state(j, *pending.pop(0))

    @pl.when(jnp.logical_and(last, jnp.logical_not(is_lat)))
    def _():
        for j in range(heads):
            sfin_ref[0, j] = st_ref[j].T


def _hg_scan(proj_qv, proj_fg, lb_logits_dir, s0_lat, *, groups, batch, seq, n_heads, dk, dv, d, reverse, layer,
             o_fwd=None, g_norm=None, mod_rider=None):
    chunk = _tile(math.gcd(seq, groups.dec_seq), HG_CHUNK)
    heads = _tile(n_heads, HG_HEADS_PER_STEP)
    assert dk == dv and d == n_heads * dk
    w = heads * dk
    segs = d // w
    lay = _ScanLayout(n_ctx=groups.ctx_rows // chunk, per_ctx=seq // chunk, per_lat=groups.dec_seq // chunk,
                      total=groups.rows // chunk, batch=batch, reverse=reverse)
    fuse_out = o_fwd is not None

    def seg_spec(seg):
        return pl.BlockSpec((chunk, w), lambda h, g: (lay.pos(g)[0], seg * segs + h))

    n_lb = lb_logits_dir.shape[0]
    in_specs = [seg_spec(0), seg_spec(1), seg_spec(1 if reverse else 0),
                pl.BlockSpec((n_lb, w), lambda h, g: (0, h)),
                pl.BlockSpec((1, heads, dk, dv), lambda h, g: (lay.pos(g)[3], h, 0, 0))]
    args = [proj_qv, proj_qv, proj_fg, lb_logits_dir, s0_lat]
    if fuse_out:
        in_specs += [seg_spec(0), seg_spec(2), pl.BlockSpec((1, dv), lambda h, g: (0, 0))]
        args += [o_fwd, proj_fg, g_norm.reshape(1, dv)]
    out_specs = [
        pl.BlockSpec((chunk, w), lambda h, g: (lay.pos(g)[0], h)),
        pl.BlockSpec((1, heads, dk, dv), lambda h, g: (lay.pos(g)[2], h, 0, 0)),
    ]
    out_shape = [
        jax.ShapeDtypeStruct((groups.rows, d), BF16 if fuse_out else F32),
        jax.ShapeDtypeStruct((batch, n_heads, dk, dv), F32),
    ]
    if mod_rider is not None:
        cond, w_mod, b_mod, mod_layer, col0, ncols = mod_rider
        n_steps = (n_heads // heads) * lay.total
        tc = next(t for t in range(V7X_LANES, ncols + 1, V7X_LANES)
                  if ncols % t == 0 and col0 % t == 0 and ncols // t <= n_steps)

        def tile_of(h, g):
            return jnp.minimum(h * lay.total + g, ncols // tc - 1)

        in_specs += [pl.BlockSpec(cond.shape, lambda h, g: (0, 0)),
                     pl.BlockSpec((None, d, tc), lambda h, g: (mod_layer, 0, col0 // tc + tile_of(h, g))),
                     pl.BlockSpec((None, 1, tc), lambda h, g: (mod_layer, 0, col0 // tc + tile_of(h, g)))]
        args += [cond, w_mod, b_mod]
        out_specs.append(pl.BlockSpec((cond.shape[0], tc), lambda h, g: (0, tile_of(h, g))))
        out_shape.append(jax.ShapeDtypeStruct((cond.shape[0], ncols), F32))
    return pl.pallas_call(
        functools.partial(_hg_scan_kernel, chunk=chunk, heads=heads, dk=dk, dv=dv,
                          layer=layer, lay=lay, fuse_out=fuse_out, mod_rider=mod_rider is not None),
        grid=(n_heads // heads, lay.total),
        in_specs=in_specs,
        out_specs=out_specs,
        out_shape=out_shape,
        scratch_shapes=[pltpu.VMEM((heads, dv, dk), F32)],
        compiler_params=_params(("arbitrary" if mod_rider is not None else "parallel", "arbitrary")),
        name="hgrn2_scan_" + ("bwd" if reverse else "fwd"),
    )(*args)


def _rope_tables(seq, hd):
    half = hd // 2
    rows = seq // GRID_W
    r_pos = jnp.repeat(jnp.arange(rows), GRID_W).astype(F32)
    c_pos = jnp.tile(jnp.arange(GRID_W), rows).astype(F32)
    inv_freq = ROPE_BASE ** (-jnp.arange(0, half, 2, dtype=F32) / half)
    ang_r = r_pos[:, None] * inv_freq
    ang_c = c_pos[:, None] * inv_freq
    cos = jnp.concatenate([jnp.cos(ang_r)] * 2 + [jnp.cos(ang_c)] * 2, axis=-1)
    sin = jnp.concatenate([-jnp.sin(ang_r), jnp.sin(ang_r), -jnp.sin(ang_c), jnp.sin(ang_c)], axis=-1)
    return cos, sin


def _rope_tile(x, cos, sin):
    hd = x.shape[1]
    quarter = hd // 4
    lane = lax.broadcasted_iota(jnp.int32, x.shape, 1)
    first = (lane % (2 * quarter)) < quarter
    partner = jnp.where(first, pltpu.roll(x, hd - quarter, 1), pltpu.roll(x, quarter, 1))
    return x * cos + partner * sin


def _attn_kernel(*refs, hd, heads, tq, sub, lookahead, past, scale, lam_init, latent):
    if latent:
        (lam_ref, q_ref, k_ref, v_ref, kc_ref, vc_ref, cos_ref, sin_ref, g_ref, _,
         o_ref, kall_ref, vall_ref) = refs
    else:
        lam_ref, q_ref, k_ref, v_ref, g_ref, _, o_ref, knew_ref, vnew_ref = refs
        knew_ref[...] = k_ref[...]
        vnew_ref[...] = v_ref[...]
    w = 2 * hd
    lm = lam_ref[...]
    lam = (jnp.exp(jnp.sum(lm[0:1] * lm[1:2], axis=-1, keepdims=True))
           - jnp.exp(jnp.sum(lm[2:3] * lm[3:4], axis=-1, keepdims=True)) + lam_init)
    g = g_ref[...]

    if latent:
        i = pl.program_id(2)

        @pl.when(i == 0)
        def _():
            kall_ref[0:past, :] = kc_ref[...].astype(BF16)
            vall_ref[0:past, :] = vc_ref[...].astype(BF16)
            vall_ref[past:, :] = v_ref[...].astype(BF16)
            for half in range(2):
                hs = slice(half * hd, (half + 1) * hd)
                kall_ref[past:, hs] = _rope_tile(k_ref[:, hs], cos_ref[...], sin_ref[...]).astype(BF16)

    def softmax_parts(qh, kh):
        s = _dot_nt(qh, kh)
        p = jnp.exp2(s - jnp.max(s, axis=-1, keepdims=True))
        return p.astype(BF16), 1.0 / jnp.sum(p, axis=-1, keepdims=True)

    units = [(j, r0) for j in range(heads) for r0 in range(0, tq, sub)]

    def probabilities(j, r0):
        parts = []
        for half in range(2):
            hs = slice(j * w + half * hd, j * w + (half + 1) * hd)
            qh = q_ref[r0:r0 + sub, hs]
            if latent:
                rows = pl.ds(pl.multiple_of(i * tq + r0, sub), sub)
                qh = _rope_tile(qh, cos_ref[rows, :], sin_ref[rows, :])
                kh = kall_ref[:, hs]
            else:
                kh = k_ref[:, hs].astype(BF16)
            parts.append(softmax_parts((qh * (scale * LOG2E)).astype(BF16), kh))
        return parts

    def output(j, r0, parts):
        (p1, r1), (p2, r2) = parts
        js = slice(j * w, (j + 1) * w)
        vj = vall_ref[...] if latent else v_ref[:, js].astype(BF16)
        o = r1 * _dot(p1, vj) - (lam * r2) * _dot(p2, vj)
        y = o * lax.rsqrt(jnp.mean(o * o, axis=-1, keepdims=True) + EPS) * g
        o_ref[r0:r0 + sub, js] = (y * (1.0 - lam_init)).astype(o_ref.dtype)

    pending = [probabilities(*u) for u in units[:lookahead]]
    for n, unit in enumerate(units):
        if n + lookahead < len(units):
            pending.append(probabilities(*units[n + lookahead]))
        output(*unit, pending.pop(0))


def _diff_attention(proj, lam_params, subln_g, *, batch, seq, row_off, n_heads, hd, lam_init,
                    cache_k=None, cache_v=None, prev=None):
    latent = cache_k is not None
    w = 2 * hd
    d = n_heads * w
    tq = _tile(seq, ATTN_TQ)
    nq = seq // tq
    heads = 1 if latent else _tile(n_heads, ATTN_CTX_HEADS_PER_STEP)
    assert row_off % seq == 0
    q_off, kv_off = row_off // tq, row_off // seq
    segs = n_heads // heads
    scale = hd ** -0.5
    past = cache_k.shape[0] // batch if latent else 0

    def kv_spec(rows, seg=0, off=0):
        return pl.BlockSpec((rows, heads * w), lambda b, h, i: (off + b, seg * segs + h))

    in_specs = [pl.BlockSpec(lam_params.shape, lambda b, h, i: (0, 0)),
                pl.BlockSpec((tq, heads * w), lambda b, h, i: (q_off + b * nq + i, h)),
                kv_spec(seq, 1, kv_off), kv_spec(seq, 2, kv_off)]
    args = [lam_params, proj, proj, proj]
    scratch = []
    if latent:
        cos, sin = _rope_tables(seq, hd)
        table = pl.BlockSpec((seq, hd), lambda b, h, i: (0, 0))
        in_specs += [kv_spec(past), kv_spec(past), table, table]
        args += [cache_k, cache_v, cos, sin]
        scratch = [pltpu.VMEM((past + seq, w), BF16), pltpu.VMEM((past + seq, w), BF16)]
    in_specs.append(pl.BlockSpec((1, w), lambda b, h, i: (0, 0)))
    args.append(subln_g.reshape(1, w))
    out_specs = [pl.BlockSpec((tq, heads * w), lambda b, h, i: (q_off + b * nq + i, h))]
    out_shape = [jax.ShapeDtypeStruct((proj.shape[0], d), BF16)]
    in_specs.append(pl.BlockSpec(memory_space=pl.ANY))
    args.append(prev)
    aliases = {len(args) - 1: 0}
    if not latent:
        out_specs += [kv_spec(seq), kv_spec(seq)]
        out_shape += [jax.ShapeDtypeStruct((batch * seq, d), F32)] * 2
    outs = pl.pallas_call(
        functools.partial(_attn_kernel, hd=hd, heads=heads, tq=tq, sub=_tile(tq, ATTN_SUB_TQ),
                          lookahead=1 if latent else 0, past=past, scale=scale, lam_init=lam_init,
                          latent=latent),
        grid=(batch, n_heads // heads, nq),
        in_specs=in_specs,
        out_specs=out_specs,
        out_shape=out_shape,
        scratch_shapes=scratch,
        input_output_aliases=aliases,
        compiler_params=_params(("parallel", "parallel", "arbitrary")),
        name="diff_attention_" + ("latent" if latent else "context"),
    )(*args)
    return outs[0] if latent else outs


def kernel(x_prompt, x_sample, state_hgrn_fwd, state_hgrn_bwd, cache_attn_k, cache_attn_v, c, c_ctx,
           w_mod, b_mod, norm_g, w_ff1, w_ff2, hg_w_in, hg_w_out, hg_g_norm, hg_lb_logits,
           da_w_in, da_w_out, da_subln_g, da_lambda, final_norm_g):
    batch, seq, d = x_prompt.shape
    dec_batch, dec_seq, _ = x_sample.shape
    depth = w_mod.shape[0]
    _, _, hg_heads, hg_dk, hg_dv = state_hgrn_fwd.shape
    _, _, past_len, da_halves, da_hd = cache_attn_k.shape
    da_heads = da_halves // 2
    ctx_rows = batch * seq
    lat_rows = dec_batch * dec_seq
    groups = _Groups(ctx_rows, dec_batch, dec_seq)

    x = (x_prompt.reshape(ctx_rows, d), x_sample.reshape(lat_rows, d))

    n_groups = 1 + dec_batch
    g8 = -(-n_groups // V7X_SUBLANES) * V7X_SUBLANES
    cond = jnp.concatenate([c_ctx[None, :], c, jnp.zeros((g8 - n_groups, d), F32)], axis=0)
    b_mod3 = b_mod.reshape(depth, 1, 6 * d)
    mods = {layer: _modulation(cond, w_mod, b_mod3, layer).reshape(1, g8, 1, 6 * d)
            for layer in range(depth) if layer == 0 or (layer - 1) % N_MIXERS != 0}
    SH1, SC1, GT1, SH2, SC2, GT2 = range(6)

    def mixer_weights(layer):
        j = layer // N_MIXERS
        if layer % N_MIXERS == 0:
            return [_Cast(hg_w_in, j, 0, 2 * d), _Cast(hg_w_in, j, 2 * d, 3 * d)], _Cast(hg_w_out, j)
        return [_Cast(da_w_in, j)], _Cast(da_w_out, j)

    first_parts, _ = mixer_weights(0)
    w_in = [_cast_weight(first_parts[0])] + first_parts[1:]
    new_sf, new_sb, new_k, new_v = [], [], [], []
    for layer in range(depth):
        mod = mods[layer]
        h = _norm_mod(x, norm_g[layer, 0], mod, 0, SC1, SH1, groups)
        j = layer // N_MIXERS
        out_casts = (mixer_weights(layer)[1], _Cast(w_ff1, layer))
        projs = []
        for n in range(len(w_in)):
            last = n + 1 == len(w_in)
            casts = out_casts if last else tuple(w for w in w_in[n + 1:n + 2] if isinstance(w, _Cast))
            out = _matmul(h, w_in[n], F32, casts=casts)
            if last:
                out, w_out, w1 = out
            elif casts:
                out, w_in[n + 1] = out
            projs.append(out)
        if layer % N_MIXERS == 0:
            scan = dict(groups=groups, batch=batch, seq=seq, n_heads=hg_heads, dk=hg_dk, dv=hg_dv, d=d,
                        layer=layer)
            riders = [None, None]
            if layer + 1 < depth:
                riders = [(cond, w_mod, b_mod3, layer + 1, half * 3 * d, 3 * d) for half in range(2)]
            o_fwd, s_fwd, *m_fwd = _hg_scan(*projs, hg_lb_logits[0], state_hgrn_fwd[:, j], reverse=False,
                                            mod_rider=riders[0], **scan)
            mixed, s_bwd, *m_bwd = _hg_scan(*projs, hg_lb_logits[1], state_hgrn_bwd[:, j], reverse=True,
                                            o_fwd=o_fwd, g_norm=hg_g_norm[j], mod_rider=riders[1], **scan)
            if layer + 1 < depth:
                mods[layer + 1] = jnp.concatenate(m_fwd + m_bwd, axis=-1).reshape(1, g8, 1, 6 * d)
            new_sf.append(s_fwd)
            new_sb.append(s_bwd)
        else:
            proj, = projs
            lam_init = 0.8 - 0.6 * math.exp(-0.3 * layer)
            att = dict(n_heads=da_heads, hd=da_hd, lam_init=lam_init)
            mixed, k_ctx, v_ctx = _diff_attention(proj, da_lambda[j], da_subln_g[j], batch=batch, seq=seq,
                                                  row_off=0, prev=h, **att)
            new_k.append(k_ctx.reshape(batch, seq, da_halves, da_hd))
            new_v.append(v_ctx.reshape(batch, seq, da_heads, 2 * da_hd))
            mixed = _diff_attention(proj, da_lambda[j], da_subln_g[j], batch=dec_batch, seq=dec_seq,
                                    row_off=ctx_rows,
                                    cache_k=cache_attn_k[:, j].reshape(dec_batch * past_len, d),
                                    cache_v=cache_attn_v[:, j].reshape(dec_batch * past_len, d),
                                    prev=mixed, **att)
        x = _matmul(mixed, w_out, F32, "residual", res=x, mod=mod, layer=0, gate_chunk=GT1, groups=groups)
        h = _norm_mod(x, norm_g[layer, 1], mod, 0, SC2, SH2, groups)
        next_parts = mixer_weights(layer + 1)[0] if layer + 1 < depth else []
        u, w2, *w_in = _matmul(h, w1, BF16, "relu2", casts=(_Cast(w_ff2, layer), *next_parts))
        x = _matmul(u, w2, F32, "residual", res=x, mod=mod, layer=0, gate_chunk=GT2, groups=groups)

    y_prompt = _final_norm(x, final_norm_g, 0, ctx_rows).reshape(batch, seq, d)
    y_sample = _final_norm(x, final_norm_g, ctx_rows, lat_rows).reshape(dec_batch, dec_seq, d)
    return (y_prompt, y_sample, jnp.stack(new_sf, axis=1), jnp.stack(new_sb, axis=1),
            jnp.stack(new_k, axis=1), jnp.stack(new_v, axis=1))
```

```python
import functools
import math
from typing import NamedTuple

import jax
import jax.numpy as jnp
from jax import lax
from jax.experimental import pallas as pl
from jax.experimental.pallas import tpu as pltpu

F32 = jnp.float32
BF16 = jnp.bfloat16

GRID_W = 64
EPS = 1e-6
ROPE_BASE = 10000.0
N_MIXERS = 2

V7X_SUBLANES = 8
V7X_LANES = 128
V7X_VMEM_BYTES = 64 * 1024 * 1024
COMPILER_RESERVED_VMEM_BYTES = 4 * 1024 * 1024
V7X_VMEM_LIMIT_BYTES = V7X_VMEM_BYTES - COMPILER_RESERVED_VMEM_BYTES

MM_BM = 1024
MM_BN = 1024
MM_VMEM_BUDGET_BYTES = 52 * 1024 * 1024
CAST_VMEM_BUDGET_BYTES = 10 * 1024 * 1024
CAST_TILE_COLS = 1024
ROW_TILE = 512
MOD_BN = 1024
HG_CHUNK = 128
HG_HEADS_PER_STEP = 32
HG_LOOKAHEAD = 1
ATTN_TQ = 1024
ATTN_SUB_TQ = 256
ATTN_CTX_HEADS_PER_STEP = 8

LOG2E = 1.4426950408889634


def _tile(full, want):
    t = min(full, want)
    assert full % t == 0, (full, want)
    return t


def _params(sem):
    return pltpu.CompilerParams(dimension_semantics=sem, vmem_limit_bytes=V7X_VMEM_LIMIT_BYTES)


def _silu(x):
    half = 0.5 * x
    return half + half * jnp.tanh(half)


def _dot(a, b):
    return jnp.dot(a, b, preferred_element_type=F32)


def _dot_nt(a, b):
    return lax.dot_general(a, b, (((1,), (1,)), ((), ())), preferred_element_type=F32)


def _dot_tn(a, b):
    return lax.dot_general(a, b, (((0,), (0,)), ((), ())), preferred_element_type=F32)


class _Groups:
    def __init__(self, ctx_rows, dec_batch, dec_seq):
        self.ctx_rows = ctx_rows
        self.dec_batch = dec_batch
        self.dec_seq = dec_seq
        self.rows = ctx_rows + dec_batch * dec_seq

    def of_tile(self, i, tile):
        assert self.ctx_rows % tile == 0 and self.dec_seq % tile == 0
        g = jnp.int32(0)
        for b in range(self.dec_batch):
            g = g + (i >= (self.ctx_rows + b * self.dec_seq) // tile).astype(jnp.int32)
        return g


def _mod_kernel(c_ref, w_ref, b_ref, o_ref):
    a = _silu(c_ref[...]).astype(BF16)
    o_ref[...] = _dot(a, w_ref[...].astype(BF16)) + b_ref[...]


def _modulation(cond, w_mod, b_mod, layer):
    g8, d = cond.shape
    n = w_mod.shape[2]
    bn = _tile(n, MOD_BN)
    return pl.pallas_call(
        _mod_kernel,
        grid=(n // bn,),
        in_specs=[
            pl.BlockSpec((g8, d), lambda j: (0, 0)),
            pl.BlockSpec((None, d, bn), lambda j: (layer, 0, j)),
            pl.BlockSpec((None, 1, bn), lambda j: (layer, 0, j)),
        ],
        out_specs=pl.BlockSpec((g8, bn), lambda j: (0, j)),
        out_shape=jax.ShapeDtypeStruct((g8, n), F32),
        compiler_params=_params(("arbitrary",)),
        name="adaln_modulation",
    )(cond, w_mod, b_mod)


def _norm_kernel(*refs, modulated, x_split):
    if modulated:
        *x_refs, g_ref, sc_ref, sh_ref, o_ref = refs
    else:
        *x_refs, g_ref, o_ref = refs
    x = x_refs[0][...]
    if len(x_refs) == 2:
        x = jnp.where(pl.program_id(0) < x_split, x, x_refs[1][...])
    y = x * lax.rsqrt(jnp.mean(x * x, axis=-1, keepdims=True) + EPS) * g_ref[...]
    if modulated:
        y = y * (1.0 + sc_ref[...]) + sh_ref[...]
    o_ref[...] = y.astype(o_ref.dtype)


def _norm_mod(x, g, mod, sc_chunk, sh_chunk, groups):
    rows, d = groups.rows, g.shape[0]
    tm = _tile(math.gcd(groups.ctx_rows, groups.dec_seq), ROW_TILE)

    def vec(chunk):
        return pl.BlockSpec((None, 1, d), lambda i: (groups.of_tile(i, tm), 0, chunk))

    if isinstance(x, tuple):
        x_split = x[0].shape[0] // tm
        x_specs = [pl.BlockSpec((tm, d), lambda i: (jnp.minimum(i, x_split - 1), 0)),
                   pl.BlockSpec((tm, d), lambda i: (jnp.maximum(i - x_split, 0), 0))]
        xs = list(x)
    else:
        x_split = 0
        x_specs = [pl.BlockSpec((tm, d), lambda i: (i, 0))]
        xs = [x]
    return pl.pallas_call(
        functools.partial(_norm_kernel, modulated=True, x_split=x_split),
        grid=(rows // tm,),
        in_specs=x_specs + [
            pl.BlockSpec((1, d), lambda i: (0, 0)),
            vec(sc_chunk),
            vec(sh_chunk),
        ],
        out_specs=pl.BlockSpec((tm, d), lambda i: (i, 0)),
        out_shape=jax.ShapeDtypeStruct((rows, d), BF16),
        compiler_params=_params(("parallel",)),
        name="rmsnorm_modulate",
    )(*xs, g.reshape(1, d), mod, mod)


def _final_norm(x, g, row_off, rows):
    d = x.shape[1]
    tm = _tile(math.gcd(rows, row_off) if row_off else rows, ROW_TILE)
    off = row_off // tm
    return pl.pallas_call(
        functools.partial(_norm_kernel, modulated=False, x_split=0),
        grid=(rows // tm,),
        in_specs=[
            pl.BlockSpec((tm, d), lambda i: (i + off, 0)),
            pl.BlockSpec((1, d), lambda i: (0, 0)),
        ],
        out_specs=pl.BlockSpec((tm, d), lambda i: (i, 0)),
        out_shape=jax.ShapeDtypeStruct((rows, d), F32),
        compiler_params=_params(("parallel",)),
        name="final_rmsnorm",
    )(x, g.reshape(1, d))


def _mm_kernel(*refs, nk, epilogue, n_res, res_split, n_casts):
    x_ref, w_ref = refs[:2]
    n_in = 2 + (n_res + 1 if epilogue == "residual" else 0)
    res_refs = refs[2:2 + n_res]
    gate_ref = refs[n_in - 1]
    o_ref = refs[n_in + n_casts]
    for src_ref, dst_ref in zip(refs[n_in:n_in + n_casts], refs[n_in + n_casts + 1:]):
        dst_ref[...] = src_ref[...].astype(dst_ref.dtype)

    def finish(acc):
        if epilogue == "relu2":
            r = jnp.maximum(acc, 0.0)
            acc = r * r
        elif epilogue == "residual":
            res = res_refs[0][...]
            if len(res_refs) == 2:
                res = jnp.where(pl.program_id(0) < res_split, res, res_refs[1][...])
            acc = res + gate_ref[...] * acc
        return acc

    def product():
        return _dot(x_ref[...], w_ref[...])

    if nk == 1:
        o_ref[...] = finish(product()).astype(o_ref.dtype)
        return

    k = pl.program_id(2)

    @pl.when(k == 0)
    def _():
        o_ref[...] = product()

    @pl.when(jnp.logical_and(k > 0, k < nk - 1))
    def _():
        o_ref[...] += product()

    @pl.when(k == nk - 1)
    def _():
        o_ref[...] = finish(o_ref[...] + product())


class _Cast(NamedTuple):
    src: jax.Array
    lead: int
    col0: int = 0
    ncols: int = 0

    @property
    def shape(self):
        return self.src.shape[1], self.ncols or self.src.shape[2]


def _cast_kernel(s_ref, o_ref):
    o_ref[...] = s_ref[...].astype(o_ref.dtype)


def _cast_weight(c):
    rows, cols = c.shape
    tr, tc = _tile(rows, CAST_TILE_COLS), _tile(cols, 2 * CAST_TILE_COLS)
    assert c.col0 % tc == 0
    j0 = c.col0 // tc
    return pl.pallas_call(
        _cast_kernel,
        grid=(rows // tr, cols // tc),
        in_specs=[pl.BlockSpec((None, tr, tc), lambda i, j: (c.lead, i, j0 + j))],
        out_specs=pl.BlockSpec((tr, tc), lambda i, j: (i, j)),
        out_shape=jax.ShapeDtypeStruct((rows, cols), BF16),
        compiler_params=_params(("parallel", "parallel")),
        name="cast_weight",
    )(c.src)


def _matmul(x, w, out_dtype, epilogue="plain", *, res=None, mod=None, gate_chunk=None, groups=None, casts=()):
    m, kdim = x.shape
    n = w.shape[1]
    bm = _tile(m, MM_BM)
    n_res = 0 if epilogue != "residual" else (2 if isinstance(res, tuple) else 1)
    out_bytes = jnp.dtype(out_dtype).itemsize + 4 * n_res

    def fits(bn, bk):
        return 2 * (2 * bk * (bm + bn) + bm * bn * out_bytes) + 4 * bm * bn <= MM_VMEM_BUDGET_BYTES

    bn_full = _tile(n, MM_BN)
    candidates = [(bn_full, kdim), (max(bn_full // 2, V7X_LANES), kdim)]
    candidates += [(bn_full, kdim // s) for s in (2, 4, 8, 16, 32)]
    bn, bk = next((bn_, bk_) for bn_, bk_ in candidates if kdim % bk_ == 0 and fits(bn_, bk_))
    nk = kdim // bk
    nj = n // bn
    assert nk == 1 or out_dtype == F32
    in_specs = [
        pl.BlockSpec((bm, bk), lambda i, j, k: (i, k)),
        pl.BlockSpec((bk, bn), lambda i, j, k: (k, j)),
    ]
    args = [x, w]
    res_split = 0
    if epilogue == "residual":
        nb = n // bn
        if isinstance(res, tuple):
            res_a, res_b = res
            res_split = res_a.shape[0] // bm
            assert res_a.shape[0] % bm == 0 and res_a.shape[0] + res_b.shape[0] == m
            in_specs += [
                pl.BlockSpec((bm, bn), lambda i, j, k: (jnp.minimum(i, res_split - 1),
                                                        jnp.where(i < res_split, j, nb - 1))),
                pl.BlockSpec((bm, bn), lambda i, j, k: (jnp.maximum(i - res_split, 0),
                                                        jnp.where(i < res_split, 0, j))),
            ]
            args += [res_a, res_b]
        else:
            in_specs.append(pl.BlockSpec((bm, bn), lambda i, j, k: (i, j)))
            args.append(res)
        in_specs.append(pl.BlockSpec((None, 1, bn),
                                     lambda i, j, k: (groups.of_tile(i, bm), 0, gate_chunk * nb + j)))
        args.append(mod)

    out_specs = [pl.BlockSpec((bm, bn), lambda i, j, k: (i, j))]
    out_shape = [jax.ShapeDtypeStruct((m, n), out_dtype)]
    n_steps = (m // bm) * nj * nk
    cast_bytes = 0
    for c in casts:
        rows, cols = c.shape
        assert cols % CAST_TILE_COLS == 0 and c.col0 % CAST_TILE_COLS == 0
        tiles_c = cols // CAST_TILE_COLS
        tile_rows = next(tr for tr in (128, 256, 512, 1024, 2048)
                         if rows % tr == 0 and (rows // tr) * tiles_c <= n_steps)
        n_tiles = (rows // tile_rows) * tiles_c
        cast_bytes += 2 * (4 + 2) * tile_rows * CAST_TILE_COLS
        assert cast_bytes <= CAST_VMEM_BUDGET_BYTES

        def tile_of(i, j, k, n_tiles=n_tiles, tiles_c=tiles_c):
            t = jnp.minimum((i * nj + j) * nk + k, n_tiles - 1)
            return t // tiles_c, t % tiles_c

        def src_tile_of(i, j, k, c=c, tile_of=tile_of):
            r, t = tile_of(i, j, k)
            return c.lead, r, c.col0 // CAST_TILE_COLS + t

        in_specs.append(pl.BlockSpec((None, tile_rows, CAST_TILE_COLS), src_tile_of))
        args.append(c.src)
        out_specs.append(pl.BlockSpec((tile_rows, CAST_TILE_COLS), tile_of))
        out_shape.append(jax.ShapeDtypeStruct((rows, cols), BF16))

    outs = pl.pallas_call(
        functools.partial(_mm_kernel, nk=nk, epilogue=epilogue, n_res=n_res, res_split=res_split,
                          n_casts=len(casts)),
        grid=(m // bm, nj, nk),
        in_specs=in_specs,
        out_specs=out_specs,
        out_shape=out_shape,
        compiler_params=_params(("arbitrary", "arbitrary", "arbitrary")),
        name="matmul_" + epilogue,
    )(*args)
    return outs[0] if not casts else outs


class _ScanLayout(NamedTuple):
    n_ctx: int
    per_ctx: int
    per_lat: int
    total: int
    batch: int
    reverse: bool

    def pos(self, g):
        r = self.total - 1 - g if self.reverse else g
        is_lat = r >= self.n_ctx
        rl = jnp.maximum(r - self.n_ctx, 0)
        seq_ctx = jnp.minimum(r // self.per_ctx, self.batch - 1)
        seq_lat = rl // self.per_lat
        pos = jnp.where(is_lat, rl % self.per_lat, r % self.per_ctx)
        per = jnp.where(is_lat, self.per_lat, self.per_ctx)
        start, stop = (per - 1, 0) if self.reverse else (0, per - 1)
        return r, is_lat, seq_ctx, seq_lat, pos == start, pos == stop


def _level_ref(c, half, reverse):
    rows, width = c.shape
    blk = 2 * half
    idx = half if reverse else half - 1
    assert blk % V7X_SUBLANES == 0
    c3 = c.reshape(rows // blk, blk, width)
    ref = jnp.broadcast_to(c3[:, idx:idx + 1, :], c3.shape)
    return ref.reshape(rows, width)


def _exp2_neg_abs(d):
    bits = lax.bitcast_convert_type(d, jnp.uint32) | jnp.uint32(0x80000000)
    return jnp.exp2(lax.bitcast_convert_type(bits, F32))


def _level_weights(c, fg, level, reverse):
    rows = c.shape[0]
    if level > 2:
        w = _exp2_neg_abs(c - _level_ref(c, 1 << (level - 1), reverse))
        return w, w
    if level == 1:
        return fg, None
    pos = lax.broadcasted_iota(jnp.int32, c.shape, 0) & 3
    prev = pltpu.roll(fg, 1, 0)
    nxt = pltpu.roll(fg, rows - 1, 0)
    if reverse:
        return fg * jnp.where(pos == 0, nxt, 1.0), jnp.where(pos == 3, prev, 1.0)
    return fg * jnp.where(pos == 3, prev, 1.0), jnp.where(pos == 0, nxt, 1.0)


def _hg_scan_kernel(*refs, chunk, heads, dk, dv, layer, lay, fuse_out, mod_rider):
    refs = list(refs)
    q_ref, v_ref, f_ref, lbl_ref, s0_ref = refs[:5]
    del refs[:5]
    if fuse_out:
        ofwd_ref, g_ref, gn_ref = refs[:3]
        del refs[:3]
    if mod_rider:
        cond_ref, wmod_ref, bmod_ref = refs[:3]
        del refs[:3]
        o_ref, sfin_ref, modr_ref, st_ref = refs
        modr_ref[...] = _dot(_silu(cond_ref[...]).astype(BF16), wmod_ref[...].astype(BF16)) + bmod_ref[...]
    else:
        o_ref, sfin_ref, st_ref = refs
    reverse = lay.reverse
    _, is_lat, _, _, first, last = lay.pos(pl.program_id(1))
    n_levels = chunk.bit_length() - 1
    low_levels = min(n_levels, V7X_SUBLANES.bit_length() - 1)

    @pl.when(jnp.logical_and(first, is_lat))
    def _():
        for j in range(heads):
            st_ref[j] = s0_ref[0, j].T

    @pl.when(jnp.logical_and(first, jnp.logical_not(is_lat)))
    def _():
        st_ref[...] = jnp.zeros(st_ref.shape, F32)

    lbl = lbl_ref[...]
    e = jnp.exp(lbl - jnp.max(lbl, axis=0, keepdims=True))
    lb = jnp.sum(e[:layer + 1], axis=0, keepdims=True) / jnp.sum(e, axis=0, keepdims=True)

    f = f_ref[...]
    e_f = _exp2_neg_abs(f * LOG2E)
    r = 1.0 / (1.0 + e_f)
    fg_all = lb + (1.0 - lb) * jnp.where(f >= 0.0, r, e_f * r)
    k_all = 1.0 - fg_all
    log2f = jnp.log2(fg_all)

    row = lax.broadcasted_iota(jnp.int32, (chunk, chunk), 0)
    col = lax.broadcasted_iota(jnp.int32, (chunk, chunk), 1)
    before = (col > row) if reverse else (col < row)
    tri = jnp.logical_or(before, col == row).astype(BF16)
    hi = log2f.astype(BF16)
    r1 = log2f - hi.astype(F32)
    mid = r1.astype(BF16)
    lo = (r1 - mid.astype(F32)).astype(BF16)
    c_all = _dot(tri, hi) + _dot(tri, mid) + _dot(tri, lo)

    x = row ^ col
    lvl = jnp.zeros((chunk, chunk), jnp.int32)
    for b in range(n_levels):
        lvl = lvl + (x >= (1 << b)).astype(jnp.int32)
    lvl = jnp.where(jnp.logical_or(before, x == 0), lvl, -1)

    end = 0 if reverse else chunk - 1
    if fuse_out:
        gn = gn_ref[...]

    def pair_weights(j):
        sl = slice(j * dk, (j + 1) * dk)
        q = _silu(q_ref[:, sl])
        k = k_all[:, sl]
        c = c_all[:, sl]
        fg = fg_all[:, sl]
        q_bf, k_bf = q.astype(BF16), k.astype(BF16)
        a = jnp.where(lvl == 0, _dot_nt(q_bf, k_bf), 0.0)
        for level in range(1, low_levels + 1):
            w_q, w_k = _level_weights(c, fg, level, reverse)
            a_l = _dot_nt((q * w_q).astype(BF16), k_bf if w_k is None else (k * w_k).astype(BF16))
            a = jnp.where(lvl == level, a_l, a)
        for level in range(low_levels + 1, n_levels + 1):
            half = 1 << (level - 1)
            w, _ = _level_weights(c, fg, level, reverse)
            q_rows, k_rows = [], []
            for b0 in range(0, chunk, 2 * half):
                early, late = (b0 + half, b0) if reverse else (b0, b0 + half)
                q_rows.append(q[late:late + half] * w[late:late + half])
                k_scaled = k[early:early + half] * w[early:early + half]
                k_rows += [k[late:late + half], k_scaled] if reverse else [k_scaled, k[late:late + half]]
            p = _dot_nt(jnp.concatenate(q_rows, axis=0).astype(BF16),
                        jnp.concatenate(k_rows, axis=0).astype(BF16))
            pieces = []
            for b, b0 in enumerate(range(0, chunk, 2 * half)):
                early, late = (b0 + half, b0) if reverse else (b0, b0 + half)
                upd = jnp.where(lvl[late:late + half] == level, p[b * half:(b + 1) * half], a[late:late + half])
                pieces += [upd, a[early:early + half]] if reverse else [a[early:early + half], upd]
            a = jnp.concatenate(pieces, axis=0)
        return q, k, c, a

    def outputs_and_state(j, q, k, c, a):
        sv = slice(j * dv, (j + 1) * dv)
        v = v_ref[:, sv].astype(BF16)
        c_end = c[end:end + 1, :]
        st = st_ref[j]
        o = _dot(a.astype(BF16), v) + _dot_nt((q * jnp.exp2(c)).astype(BF16), st.astype(BF16))
        if fuse_out:
            o = o + ofwd_ref[:, sv]
            y = o * lax.rsqrt(jnp.mean(o * o, axis=-1, keepdims=True) + EPS) * gn
            o = y * _silu(g_ref[:, sv])
        o_ref[:, sv] = o.astype(o_ref.dtype)
        k_dec = (k * jnp.exp2(c_end - c)).astype(BF16)
        st_ref[j] = st * jnp.exp2(c_end) + _dot_tn(v, k_dec)

    pending = [pair_weights(j) for j in range(min(HG_LOOKAHEAD, heads))]
    for j in range(heads):
        if j + HG_LOOKAHEAD < heads:
            pending.append(pair_weights(j + HG_LOOKAHEAD))
        outputs_and_state(j, *pending.pop(0))

    @pl.when(jnp.logical_and(last, jnp.logical_not(is_lat)))
    def _():
        for j in range(heads):
            sfin_ref[0, j] = st_ref[j].T


def _hg_scan(proj_qv, proj_fg, lb_logits_dir, s0_lat, *, groups, batch, seq, n_heads, dk, dv, d, reverse, layer,
             o_fwd=None, g_norm=None, mod_rider=None):
    chunk = _tile(math.gcd(seq, groups.dec_seq), HG_CHUNK)
    heads = _tile(n_heads, HG_HEADS_PER_STEP)
    assert dk == dv and d == n_heads * dk
    w = heads * dk
    segs = d // w
    lay = _ScanLayout(n_ctx=groups.ctx_rows // chunk, per_ctx=seq // chunk, per_lat=groups.dec_seq // chunk,
                      total=groups.rows // chunk, batch=batch, reverse=reverse)
    fuse_out = o_fwd is not None

    def seg_spec(seg):
        return pl.BlockSpec((chunk, w), lambda h, g: (lay.pos(g)[0], seg * segs + h))

    n_lb = lb_logits_dir.shape[0]
    in_specs = [seg_spec(0), seg_spec(1), seg_spec(1 if reverse else 0),
                pl.BlockSpec((n_lb, w), lambda h, g: (0, h)),
                pl.BlockSpec((1, heads, dk, dv), lambda h, g: (lay.pos(g)[3], h, 0, 0))]
    args = [proj_qv, proj_qv, proj_fg, lb_logits_dir, s0_lat]
    if fuse_out:
        in_specs += [seg_spec(0), seg_spec(2), pl.BlockSpec((1, dv), lambda h, g: (0, 0))]
        args += [o_fwd, proj_fg, g_norm.reshape(1, dv)]
    out_specs = [
        pl.BlockSpec((chunk, w), lambda h, g: (lay.pos(g)[0], h)),
        pl.BlockSpec((1, heads, dk, dv), lambda h, g: (lay.pos(g)[2], h, 0, 0)),
    ]
    out_shape = [
        jax.ShapeDtypeStruct((groups.rows, d), BF16 if fuse_out else F32),
        jax.ShapeDtypeStruct((batch, n_heads, dk, dv), F32),
    ]
    if mod_rider is not None:
        cond, w_mod, b_mod, mod_layer, col0, ncols = mod_rider
        n_steps = (n_heads // heads) * lay.total
        tc = next(t for t in range(V7X_LANES, ncols + 1, V7X_LANES)
                  if ncols % t == 0 and col0 % t == 0 and ncols // t <= n_steps)

        def tile_of(h, g):
            return jnp.minimum(h * lay.total + g, ncols // tc - 1)

        in_specs += [pl.BlockSpec(cond.shape, lambda h, g: (0, 0)),
                     pl.BlockSpec((None, d, tc), lambda h, g: (mod_layer, 0, col0 // tc + tile_of(h, g))),
                     pl.BlockSpec((None, 1, tc), lambda h, g: (mod_layer, 0, col0 // tc + tile_of(h, g)))]
        args += [cond, w_mod, b_mod]
        out_specs.append(pl.BlockSpec((cond.shape[0], tc), lambda h, g: (0, tile_of(h, g))))
        out_shape.append(jax.ShapeDtypeStruct((cond.shape[0], ncols), F32))
    return pl.pallas_call(
        functools.partial(_hg_scan_kernel, chunk=chunk, heads=heads, dk=dk, dv=dv,
                          layer=layer, lay=lay, fuse_out=fuse_out, mod_rider=mod_rider is not None),
        grid=(n_heads // heads, lay.total),
        in_specs=in_specs,
        out_specs=out_specs,
        out_shape=out_shape,
        scratch_shapes=[pltpu.VMEM((heads, dv, dk), F32)],
        compiler_params=_params(("arbitrary" if mod_rider is not None else "parallel", "arbitrary")),
        name="hgrn2_scan_" + ("bwd" if reverse else "fwd"),
    )(*args)


def _rope_tables(seq, hd):
    half = hd // 2
    rows = seq // GRID_W
    r_pos = jnp.repeat(jnp.arange(rows), GRID_W).astype(F32)
    c_pos = jnp.tile(jnp.arange(GRID_W), rows).astype(F32)
    inv_freq = ROPE_BASE ** (-jnp.arange(0, half, 2, dtype=F32) / half)
    ang_r = r_pos[:, None] * inv_freq
    ang_c = c_pos[:, None] * inv_freq
    cos = jnp.concatenate([jnp.cos(ang_r)] * 2 + [jnp.cos(ang_c)] * 2, axis=-1)
    sin = jnp.concatenate([-jnp.sin(ang_r), jnp.sin(ang_r), -jnp.sin(ang_c), jnp.sin(ang_c)], axis=-1)
    return cos, sin


def _rope_tile(x, cos, sin):
    hd = x.shape[1]
    quarter = hd // 4
    lane = lax.broadcasted_iota(jnp.int32, x.shape, 1)
    first = (lane % (2 * quarter)) < quarter
    partner = jnp.where(first, pltpu.roll(x, hd - quarter, 1), pltpu.roll(x, quarter, 1))
    return x * cos + partner * sin


def _attn_kernel(*refs, hd, heads, tq, sub, lookahead, past, scale, lam_init, latent):
    if latent:
        (lam_ref, q_ref, k_ref, v_ref, kc_ref, vc_ref, cos_ref, sin_ref, g_ref, _,
         o_ref, kall_ref, vall_ref) = refs
    else:
        lam_ref, q_ref, k_ref, v_ref, g_ref, _, o_ref, knew_ref, vnew_ref = refs
        knew_ref[...] = k_ref[...]
        vnew_ref[...] = v_ref[...]
    w = 2 * hd
    lm = lam_ref[...]
    lam = (jnp.exp(jnp.sum(lm[0:1] * lm[1:2], axis=-1, keepdims=True))
           - jnp.exp(jnp.sum(lm[2:3] * lm[3:4], axis=-1, keepdims=True)) + lam_init)
    g = g_ref[...]

    if latent:
        i = pl.program_id(2)

        @pl.when(i == 0)
        def _():
            kall_ref[0:past, :] = kc_ref[...].astype(BF16)
            vall_ref[0:past, :] = vc_ref[...].astype(BF16)
            vall_ref[past:, :] = v_ref[...].astype(BF16)
            for half in range(2):
                hs = slice(half * hd, (half + 1) * hd)
                kall_ref[past:, hs] = _rope_tile(k_ref[:, hs], cos_ref[...], sin_ref[...]).astype(BF16)

    def softmax_parts(qh, kh):
        s = _dot_nt(qh, kh)
        p = jnp.exp2(s - jnp.max(s, axis=-1, keepdims=True))
        return p.astype(BF16), 1.0 / jnp.sum(p, axis=-1, keepdims=True)

    units = [(j, r0) for j in range(heads) for r0 in range(0, tq, sub)]

    def probabilities(j, r0):
        parts = []
        for half in range(2):
            hs = slice(j * w + half * hd, j * w + (half + 1) * hd)
            qh = q_ref[r0:r0 + sub, hs]
            if latent:
                rows = pl.ds(pl.multiple_of(i * tq + r0, sub), sub)
                qh = _rope_tile(qh, cos_ref[rows, :], sin_ref[rows, :])
                kh = kall_ref[:, hs]
            else:
                kh = k_ref[:, hs].astype(BF16)
            parts.append(softmax_parts((qh * (scale * LOG2E)).astype(BF16), kh))
        return parts

    def output(j, r0, parts):
        (p1, r1), (p2, r2) = parts
        js = slice(j * w, (j + 1) * w)
        vj = vall_ref[...] if latent else v_ref[:, js].astype(BF16)
        o = r1 * _dot(p1, vj) - (lam * r2) * _dot(p2, vj)
        y = o * lax.rsqrt(jnp.mean(o * o, axis=-1, keepdims=True) + EPS) * g
        o_ref[r0:r0 + sub, js] = (y * (1.0 - lam_init)).astype(o_ref.dtype)

    pending = [probabilities(*u) for u in units[:lookahead]]
    for n, unit in enumerate(units):
        if n + lookahead < len(units):
            pending.append(probabilities(*units[n + lookahead]))
        output(*unit, pending.pop(0))


def _diff_attention(proj, lam_params, subln_g, *, batch, seq, row_off, n_heads, hd, lam_init,
                    cache_k=None, cache_v=None, prev=None):
    latent = cache_k is not None
    w = 2 * hd
    d = n_heads * w
    tq = _tile(seq, ATTN_TQ)
    nq = seq // tq
    heads = 1 if latent else _tile(n_heads, ATTN_CTX_HEADS_PER_STEP)
    assert row_off % seq == 0
    q_off, kv_off = row_off // tq, row_off // seq
    segs = n_heads // heads
    scale = hd ** -0.5
    past = cache_k.shape[0] // batch if latent else 0

    def kv_spec(rows, seg=0, off=0):
        return pl.BlockSpec((rows, heads * w), lambda b, h, i: (off + b, seg * segs + h))

    in_specs = [pl.BlockSpec(lam_params.shape, lambda b, h, i: (0, 0)),
                pl.BlockSpec((tq, heads * w), lambda b, h, i: (q_off + b * nq + i, h)),
                kv_spec(seq, 1, kv_off), kv_spec(seq, 2, kv_off)]
    args = [lam_params, proj, proj, proj]
    scratch = []
    if latent:
        cos, sin = _rope_tables(seq, hd)
        table = pl.BlockSpec((seq, hd), lambda b, h, i: (0, 0))
        in_specs += [kv_spec(past), kv_spec(past), table, table]
        args += [cache_k, cache_v, cos, sin]
        scratch = [pltpu.VMEM((past + seq, w), BF16), pltpu.VMEM((past + seq, w), BF16)]
    in_specs.append(pl.BlockSpec((1, w), lambda b, h, i: (0, 0)))
    args.append(subln_g.reshape(1, w))
    out_specs = [pl.BlockSpec((tq, heads * w), lambda b, h, i: (q_off + b * nq + i, h))]
    out_shape = [jax.ShapeDtypeStruct((proj.shape[0], d), BF16)]
    in_specs.append(pl.BlockSpec(memory_space=pl.ANY))
    args.append(prev)
    aliases = {len(args) - 1: 0}
    if not latent:
        out_specs += [kv_spec(seq), kv_spec(seq)]
        out_shape += [jax.ShapeDtypeStruct((batch * seq, d), F32)] * 2
    outs = pl.pallas_call(
        functools.partial(_attn_kernel, hd=hd, heads=heads, tq=tq, sub=_tile(tq, ATTN_SUB_TQ),
                          lookahead=1 if latent else 0, past=past, scale=scale, lam_init=lam_init,
                          latent=latent),
        grid=(batch, n_heads // heads, nq),
        in_specs=in_specs,
        out_specs=out_specs,
        out_shape=out_shape,
        scratch_shapes=scratch,
        input_output_aliases=aliases,
        compiler_params=_params(("parallel", "parallel", "arbitrary")),
        name="diff_attention_" + ("latent" if latent else "context"),
    )(*args)
    return outs[0] if latent else outs


def kernel(x_prompt, x_sample, state_hgrn_fwd, state_hgrn_bwd, cache_attn_k, cache_attn_v, c, c_ctx,
           w_mod, b_mod, norm_g, w_ff1, w_ff2, hg_w_in, hg_w_out, hg_g_norm, hg_lb_logits,
           da_w_in, da_w_out, da_subln_g, da_lambda, final_norm_g):
    batch, seq, d = x_prompt.shape
    dec_batch, dec_seq, _ = x_sample.shape
    depth = w_mod.shape[0]
    _, _, hg_heads, hg_dk, hg_dv = state_hgrn_fwd.shape
    _, _, past_len, da_halves, da_hd = cache_attn_k.shape
    da_heads = da_halves // 2
    ctx_rows = batch * seq
    lat_rows = dec_batch * dec_seq
    groups = _Groups(ctx_rows, dec_batch, dec_seq)

    x = (x_prompt.reshape(ctx_rows, d), x_sample.reshape(lat_rows, d))

    n_groups = 1 + dec_batch
    g8 = -(-n_groups // V7X_SUBLANES) * V7X_SUBLANES
    cond = jnp.concatenate([c_ctx[None, :], c, jnp.zeros((g8 - n_groups, d), F32)], axis=0)
    b_mod3 = b_mod.reshape(depth, 1, 6 * d)
    mods = {layer: _modulation(cond, w_mod, b_mod3, layer).reshape(g8, 1, 6 * d)
            for layer in range(depth) if layer == 0 or (layer - 1) % N_MIXERS != 0}
    SH1, SC1, GT1, SH2, SC2, GT2 = range(6)

    def mixer_weights(layer):
        j = layer // N_MIXERS
        if layer % N_MIXERS == 0:
            return [_Cast(hg_w_in, j, 0, 2 * d), _Cast(hg_w_in, j, 2 * d, 3 * d)], _Cast(hg_w_out, j)
        return [_Cast(da_w_in, j)], _Cast(da_w_out, j)

    first_parts, _ = mixer_weights(0)
    w_in = [_cast_weight(first_parts[0])] + first_parts[1:]
    new_sf, new_sb, new_k, new_v = [], [], [], []
    for layer in range(depth):
        mod = mods[layer]
        h = _norm_mod(x, norm_g[layer, 0], mod, SC1, SH1, groups)
        j = layer // N_MIXERS
        out_casts = (mixer_weights(layer)[1], _Cast(w_ff1, layer))
        projs = []
        for n in range(len(w_in)):
            last = n + 1 == len(w_in)
            casts = out_casts if last else tuple(w for w in w_in[n + 1:n + 2] if isinstance(w, _Cast))
            out = _matmul(h, w_in[n], F32, casts=casts)
            if last:
                out, w_out, w1 = out
            elif casts:
                out, w_in[n + 1] = out
            projs.append(out)
        if layer % N_MIXERS == 0:
            scan = dict(groups=groups, batch=batch, seq=seq, n_heads=hg_heads, dk=hg_dk, dv=hg_dv, d=d,
                        layer=layer)
            riders = [None, None]
            if layer + 1 < depth:
                riders = [(cond, w_mod, b_mod3, layer + 1, half * 3 * d, 3 * d) for half in range(2)]
            o_fwd, s_fwd, *m_fwd = _hg_scan(*projs, hg_lb_logits[0], state_hgrn_fwd[:, j], reverse=False,
                                            mod_rider=riders[0], **scan)
            mixed, s_bwd, *m_bwd = _hg_scan(*projs, hg_lb_logits[1], state_hgrn_bwd[:, j], reverse=True,
                                            o_fwd=o_fwd, g_norm=hg_g_norm[j], mod_rider=riders[1], **scan)
            if layer + 1 < depth:
                mods[layer + 1] = jnp.concatenate(m_fwd + m_bwd, axis=-1).reshape(g8, 1, 6 * d)
            new_sf.append(s_fwd)
            new_sb.append(s_bwd)
        else:
            proj, = projs
            lam_init = 0.8 - 0.6 * math.exp(-0.3 * layer)
            att = dict(n_heads=da_heads, hd=da_hd, lam_init=lam_init)
            mixed, k_ctx, v_ctx = _diff_attention(proj, da_lambda[j], da_subln_g[j], batch=batch, seq=seq,
                                                  row_off=0, prev=h, **att)
            new_k.append(k_ctx.reshape(batch, seq, da_halves, da_hd))
            new_v.append(v_ctx.reshape(batch, seq, da_heads, 2 * da_hd))
            mixed = _diff_attention(proj, da_lambda[j], da_subln_g[j], batch=dec_batch, seq=dec_seq,
                                    row_off=ctx_rows,
                                    cache_k=cache_attn_k[:, j].reshape(dec_batch * past_len, d),
                                    cache_v=cache_attn_v[:, j].reshape(dec_batch * past_len, d),
                                    prev=mixed, **att)
        x = _matmul(mixed, w_out, F32, "residual", res=x, mod=mod, gate_chunk=GT1, groups=groups)
        h = _norm_mod(x, norm_g[layer, 1], mod, SC2, SH2, groups)
        next_parts = mixer_weights(layer + 1)[0] if layer + 1 < depth else []
        u, w2, *w_in = _matmul(h, w1, BF16, "relu2", casts=(_Cast(w_ff2, layer), *next_parts))
        x = _matmul(u, w2, F32, "residual", res=x, mod=mod, gate_chunk=GT2, groups=groups)

    y_prompt = _final_norm(x, final_norm_g, 0, ctx_rows).reshape(batch, seq, d)
    y_sample = _final_norm(x, final_norm_g, ctx_rows, lat_rows).reshape(dec_batch, dec_seq, d)
    return (y_prompt, y_sample, jnp.stack(new_sf, axis=1), jnp.stack(new_sb, axis=1),
            jnp.stack(new_k, axis=1), jnp.stack(new_v, axis=1))
```

```python
import functools
import math
from typing import NamedTuple

import jax
import jax.numpy as jnp
from jax import lax
from jax.experimental import pallas as pl
from jax.experimental.pallas import tpu as pltpu

F32 = jnp.float32
BF16 = jnp.bfloat16

GRID_W = 64
EPS = 1e-6
ROPE_BASE = 10000.0
N_MIXERS = 2

V7X_SUBLANES = 8
V7X_LANES = 128
V7X_VMEM_BYTES = 64 * 1024 * 1024
COMPILER_RESERVED_VMEM_BYTES = 4 * 1024 * 1024
V7X_VMEM_LIMIT_BYTES = V7X_VMEM_BYTES - COMPILER_RESERVED_VMEM_BYTES

MM_BM = 1024
MM_BN = 1024
MM_VMEM_BUDGET_BYTES = 52 * 1024 * 1024
CAST_VMEM_BUDGET_BYTES = 10 * 1024 * 1024
CAST_TILE_COLS = 1024
ROW_TILE = 512
MOD_BN = 1024
HG_CHUNK = 128
HG_HEADS_PER_STEP = 32
HG_LOOKAHEAD = 1
ATTN_TQ = 1024
ATTN_SUB_TQ = 256
ATTN_CTX_HEADS_PER_STEP = 16

LOG2E = 1.4426950408889634


def _tile(full, want):
    t = min(full, want)
    assert full % t == 0, (full, want)
    return t


def _params(sem):
    return pltpu.CompilerParams(dimension_semantics=sem, vmem_limit_bytes=V7X_VMEM_LIMIT_BYTES)


def _silu(x):
    half = 0.5 * x
    return half + half * jnp.tanh(half)


def _dot(a, b):
    return jnp.dot(a, b, preferred_element_type=F32)


def _dot_nt(a, b):
    return lax.dot_general(a, b, (((1,), (1,)), ((), ())), preferred_element_type=F32)


def _dot_tn(a, b):
    return lax.dot_general(a, b, (((0,), (0,)), ((), ())), preferred_element_type=F32)


class _Groups:
    def __init__(self, ctx_rows, dec_batch, dec_seq):
        self.ctx_rows = ctx_rows
        self.dec_batch = dec_batch
        self.dec_seq = dec_seq
        self.rows = ctx_rows + dec_batch * dec_seq

    def of_tile(self, i, tile):
        assert self.ctx_rows % tile == 0 and self.dec_seq % tile == 0
        g = jnp.int32(0)
        for b in range(self.dec_batch):
            g = g + (i >= (self.ctx_rows + b * self.dec_seq) // tile).astype(jnp.int32)
        return g


def _mod_kernel(c_ref, w_ref, b_ref, o_ref):
    a = _silu(c_ref[...]).astype(BF16)
    o_ref[...] = _dot(a, w_ref[...].astype(BF16)) + b_ref[...]


def _modulation(cond, w_mod, b_mod, layer):
    g8, d = cond.shape
    n = w_mod.shape[2]
    bn = _tile(n, MOD_BN)
    return pl.pallas_call(
        _mod_kernel,
        grid=(n // bn,),
        in_specs=[
            pl.BlockSpec((g8, d), lambda j: (0, 0)),
            pl.BlockSpec((None, d, bn), lambda j: (layer, 0, j)),
            pl.BlockSpec((None, 1, bn), lambda j: (layer, 0, j)),
        ],
        out_specs=pl.BlockSpec((g8, bn), lambda j: (0, j)),
        out_shape=jax.ShapeDtypeStruct((g8, n), F32),
        compiler_params=_params(("arbitrary",)),
        name="adaln_modulation",
    )(cond, w_mod, b_mod)


def _norm_kernel(*refs, modulated, x_split):
    if modulated:
        *x_refs, g_ref, sc_ref, sh_ref, o_ref = refs
    else:
        *x_refs, g_ref, o_ref = refs
    x = x_refs[0][...]
    if len(x_refs) == 2:
        x = jnp.where(pl.program_id(0) < x_split, x, x_refs[1][...])
    y = x * lax.rsqrt(jnp.mean(x * x, axis=-1, keepdims=True) + EPS) * g_ref[...]
    if modulated:
        y = y * (1.0 + sc_ref[...]) + sh_ref[...]
    o_ref[...] = y.astype(o_ref.dtype)


def _norm_mod(x, g, mod, sc_chunk, sh_chunk, groups):
    rows, d = groups.rows, g.shape[0]
    tm = _tile(math.gcd(groups.ctx_rows, groups.dec_seq), ROW_TILE)

    def vec(chunk):
        return pl.BlockSpec((None, 1, d), lambda i: (groups.of_tile(i, tm), 0, chunk))

    if isinstance(x, tuple):
        x_split = x[0].shape[0] // tm
        x_specs = [pl.BlockSpec((tm, d), lambda i: (jnp.minimum(i, x_split - 1), 0)),
                   pl.BlockSpec((tm, d), lambda i: (jnp.maximum(i - x_split, 0), 0))]
        xs = list(x)
    else:
        x_split = 0
        x_specs = [pl.BlockSpec((tm, d), lambda i: (i, 0))]
        xs = [x]
    return pl.pallas_call(
        functools.partial(_norm_kernel, modulated=True, x_split=x_split),
        grid=(rows // tm,),
        in_specs=x_specs + [
            pl.BlockSpec((1, d), lambda i: (0, 0)),
            vec(sc_chunk),
            vec(sh_chunk),
        ],
        out_specs=pl.BlockSpec((tm, d), lambda i: (i, 0)),
        out_shape=jax.ShapeDtypeStruct((rows, d), BF16),
        compiler_params=_params(("parallel",)),
        name="rmsnorm_modulate",
    )(*xs, g.reshape(1, d), mod, mod)


def _final_norm(x, g, row_off, rows):
    d = x.shape[1]
    tm = _tile(math.gcd(rows, row_off) if row_off else rows, ROW_TILE)
    off = row_off // tm
    return pl.pallas_call(
        functools.partial(_norm_kernel, modulated=False, x_split=0),
        grid=(rows // tm,),
        in_specs=[
            pl.BlockSpec((tm, d), lambda i: (i + off, 0)),
            pl.BlockSpec((1, d), lambda i: (0, 0)),
        ],
        out_specs=pl.BlockSpec((tm, d), lambda i: (i, 0)),
        out_shape=jax.ShapeDtypeStruct((rows, d), F32),
        compiler_params=_params(("parallel",)),
        name="final_rmsnorm",
    )(x, g.reshape(1, d))


def _mm_kernel(*refs, nk, epilogue, n_res, res_split, n_casts):
    x_ref, w_ref = refs[:2]
    n_in = 2 + (n_res + 1 if epilogue == "residual" else 0)
    res_refs = refs[2:2 + n_res]
    gate_ref = refs[n_in - 1]
    o_ref = refs[n_in + n_casts]
    for src_ref, dst_ref in zip(refs[n_in:n_in + n_casts], refs[n_in + n_casts + 1:]):
        dst_ref[...] = src_ref[...].astype(dst_ref.dtype)

    def finish(acc):
        if epilogue == "relu2":
            r = jnp.maximum(acc, 0.0)
            acc = r * r
        elif epilogue == "residual":
            res = res_refs[0][...]
            if len(res_refs) == 2:
                res = jnp.where(pl.program_id(0) < res_split, res, res_refs[1][...])
            acc = res + gate_ref[...] * acc
        return acc

    def product():
        return _dot(x_ref[...], w_ref[...])

    if nk == 1:
        o_ref[...] = finish(product()).astype(o_ref.dtype)
        return

    k = pl.program_id(2)

    @pl.when(k == 0)
    def _():
        o_ref[...] = product()

    @pl.when(jnp.logical_and(k > 0, k < nk - 1))
    def _():
        o_ref[...] += product()

    @pl.when(k == nk - 1)
    def _():
        o_ref[...] = finish(o_ref[...] + product())


class _Cast(NamedTuple):
    src: jax.Array
    lead: int
    col0: int = 0
    ncols: int = 0

    @property
    def shape(self):
        return self.src.shape[1], self.ncols or self.src.shape[2]


def _cast_kernel(s_ref, o_ref):
    o_ref[...] = s_ref[...].astype(o_ref.dtype)


def _cast_weight(c):
    rows, cols = c.shape
    tr, tc = _tile(rows, CAST_TILE_COLS), _tile(cols, 2 * CAST_TILE_COLS)
    assert c.col0 % tc == 0
    j0 = c.col0 // tc
    return pl.pallas_call(
        _cast_kernel,
        grid=(rows // tr, cols // tc),
        in_specs=[pl.BlockSpec((None, tr, tc), lambda i, j: (c.lead, i, j0 + j))],
        out_specs=pl.BlockSpec((tr, tc), lambda i, j: (i, j)),
        out_shape=jax.ShapeDtypeStruct((rows, cols), BF16),
        compiler_params=_params(("parallel", "parallel")),
        name="cast_weight",
    )(c.src)


def _matmul(x, w, out_dtype, epilogue="plain", *, res=None, mod=None, gate_chunk=None, groups=None, casts=()):
    m, kdim = x.shape
    n = w.shape[1]
    bm = _tile(m, MM_BM)
    n_res = 0 if epilogue != "residual" else (2 if isinstance(res, tuple) else 1)
    out_bytes = jnp.dtype(out_dtype).itemsize + 4 * n_res

    def fits(bn, bk):
        return 2 * (2 * bk * (bm + bn) + bm * bn * out_bytes) + 4 * bm * bn <= MM_VMEM_BUDGET_BYTES

    bn_full = _tile(n, MM_BN)
    candidates = [(bn_full, kdim), (max(bn_full // 2, V7X_LANES), kdim)]
    candidates += [(bn_full, kdim // s) for s in (2, 4, 8, 16, 32)]
    bn, bk = next((bn_, bk_) for bn_, bk_ in candidates if kdim % bk_ == 0 and fits(bn_, bk_))
    nk = kdim // bk
    nj = n // bn
    assert nk == 1 or out_dtype == F32
    in_specs = [
        pl.BlockSpec((bm, bk), lambda i, j, k: (i, k)),
        pl.BlockSpec((bk, bn), lambda i, j, k: (k, j)),
    ]
    args = [x, w]
    res_split = 0
    if epilogue == "residual":
        nb = n // bn
        if isinstance(res, tuple):
            res_a, res_b = res
            res_split = res_a.shape[0] // bm
            assert res_a.shape[0] % bm == 0 and res_a.shape[0] + res_b.shape[0] == m
            in_specs += [
                pl.BlockSpec((bm, bn), lambda i, j, k: (jnp.minimum(i, res_split - 1),
                                                        jnp.where(i < res_split, j, nb - 1))),
                pl.BlockSpec((bm, bn), lambda i, j, k: (jnp.maximum(i - res_split, 0),
                                                        jnp.where(i < res_split, 0, j))),
            ]
            args += [res_a, res_b]
        else:
            in_specs.append(pl.BlockSpec((bm, bn), lambda i, j, k: (i, j)))
            args.append(res)
        in_specs.append(pl.BlockSpec((None, 1, bn),
                                     lambda i, j, k: (groups.of_tile(i, bm), 0, gate_chunk * nb + j)))
        args.append(mod)

    out_specs = [pl.BlockSpec((bm, bn), lambda i, j, k: (i, j))]
    out_shape = [jax.ShapeDtypeStruct((m, n), out_dtype)]
    n_steps = (m // bm) * nj * nk
    cast_bytes = 0
    for c in casts:
        rows, cols = c.shape
        assert cols % CAST_TILE_COLS == 0 and c.col0 % CAST_TILE_COLS == 0
        tiles_c = cols // CAST_TILE_COLS
        tile_rows = next(tr for tr in (128, 256, 512, 1024, 2048)
                         if rows % tr == 0 and (rows // tr) * tiles_c <= n_steps)
        n_tiles = (rows // tile_rows) * tiles_c
        cast_bytes += 2 * (4 + 2) * tile_rows * CAST_TILE_COLS
        assert cast_bytes <= CAST_VMEM_BUDGET_BYTES

        def tile_of(i, j, k, n_tiles=n_tiles, tiles_c=tiles_c):
            t = jnp.minimum((i * nj + j) * nk + k, n_tiles - 1)
            return t // tiles_c, t % tiles_c

        def src_tile_of(i, j, k, c=c, tile_of=tile_of):
            r, t = tile_of(i, j, k)
            return c.lead, r, c.col0 // CAST_TILE_COLS + t

        in_specs.append(pl.BlockSpec((None, tile_rows, CAST_TILE_COLS), src_tile_of))
        args.append(c.src)
        out_specs.append(pl.BlockSpec((tile_rows, CAST_TILE_COLS), tile_of))
        out_shape.append(jax.ShapeDtypeStruct((rows, cols), BF16))

    outs = pl.pallas_call(
        functools.partial(_mm_kernel, nk=nk, epilogue=epilogue, n_res=n_res, res_split=res_split,
                          n_casts=len(casts)),
        grid=(m // bm, nj, nk),
        in_specs=in_specs,
        out_specs=out_specs,
        out_shape=out_shape,
        compiler_params=_params(("arbitrary", "arbitrary", "arbitrary")),
        name="matmul_" + epilogue,
    )(*args)
    return outs[0] if not casts else outs


class _ScanLayout(NamedTuple):
    n_ctx: int
    per_ctx: int
    per_lat: int
    total: int
    batch: int
    reverse: bool

    def pos(self, g):
        r = self.total - 1 - g if self.reverse else g
        is_lat = r >= self.n_ctx
        rl = jnp.maximum(r - self.n_ctx, 0)
        seq_ctx = jnp.minimum(r // self.per_ctx, self.batch - 1)
        seq_lat = rl // self.per_lat
        pos = jnp.where(is_lat, rl % self.per_lat, r % self.per_ctx)
        per = jnp.where(is_lat, self.per_lat, self.per_ctx)
        start, stop = (per - 1, 0) if self.reverse else (0, per - 1)
        return r, is_lat, seq_ctx, seq_lat, pos == start, pos == stop


def _level_ref(c, half, reverse):
    rows, width = c.shape
    blk = 2 * half
    idx = half if reverse else half - 1
    assert blk % V7X_SUBLANES == 0
    c3 = c.reshape(rows // blk, blk, width)
    ref = jnp.broadcast_to(c3[:, idx:idx + 1, :], c3.shape)
    return ref.reshape(rows, width)


def _exp2_neg_abs(d):
    bits = lax.bitcast_convert_type(d, jnp.uint32) | jnp.uint32(0x80000000)
    return jnp.exp2(lax.bitcast_convert_type(bits, F32))


def _level_weights(c, fg, level, reverse):
    rows = c.shape[0]
    if level > 2:
        w = _exp2_neg_abs(c - _level_ref(c, 1 << (level - 1), reverse))
        return w, w
    if level == 1:
        return fg, None
    pos = lax.broadcasted_iota(jnp.int32, c.shape, 0) & 3
    prev = pltpu.roll(fg, 1, 0)
    nxt = pltpu.roll(fg, rows - 1, 0)
    if reverse:
        return fg * jnp.where(pos == 0, nxt, 1.0), jnp.where(pos == 3, prev, 1.0)
    return fg * jnp.where(pos == 3, prev, 1.0), jnp.where(pos == 0, nxt, 1.0)


def _hg_scan_kernel(*refs, chunk, heads, dk, dv, layer, lay, fuse_out, mod_rider):
    refs = list(refs)
    q_ref, v_ref, f_ref, lbl_ref, s0_ref = refs[:5]
    del refs[:5]
    if fuse_out:
        ofwd_ref, g_ref, gn_ref = refs[:3]
        del refs[:3]
    if mod_rider:
        cond_ref, wmod_ref, bmod_ref = refs[:3]
        del refs[:3]
        o_ref, sfin_ref, modr_ref, st_ref = refs
        modr_ref[...] = _dot(_silu(cond_ref[...]).astype(BF16), wmod_ref[...].astype(BF16)) + bmod_ref[...]
    else:
        o_ref, sfin_ref, st_ref = refs
    reverse = lay.reverse
    _, is_lat, _, _, first, last = lay.pos(pl.program_id(1))
    n_levels = chunk.bit_length() - 1
    low_levels = min(n_levels, V7X_SUBLANES.bit_length() - 1)

    @pl.when(jnp.logical_and(first, is_lat))
    def _():
        for j in range(heads):
            st_ref[j] = s0_ref[0, j].T

    @pl.when(jnp.logical_and(first, jnp.logical_not(is_lat)))
    def _():
        st_ref[...] = jnp.zeros(st_ref.shape, F32)

    lbl = lbl_ref[...]
    e = jnp.exp(lbl - jnp.max(lbl, axis=0, keepdims=True))
    lb = jnp.sum(e[:layer + 1], axis=0, keepdims=True) / jnp.sum(e, axis=0, keepdims=True)

    f = f_ref[...]
    e_f = _exp2_neg_abs(f * LOG2E)
    r = 1.0 / (1.0 + e_f)
    fg_all = lb + (1.0 - lb) * jnp.where(f >= 0.0, r, e_f * r)
    k_all = 1.0 - fg_all
    log2f = jnp.log2(fg_all)

    row = lax.broadcasted_iota(jnp.int32, (chunk, chunk), 0)
    col = lax.broadcasted_iota(jnp.int32, (chunk, chunk), 1)
    before = (col > row) if reverse else (col < row)
    tri = jnp.logical_or(before, col == row).astype(BF16)
    hi = log2f.astype(BF16)
    r1 = log2f - hi.astype(F32)
    mid = r1.astype(BF16)
    lo = (r1 - mid.astype(F32)).astype(BF16)
    c_all = _dot(tri, hi) + _dot(tri, mid) + _dot(tri, lo)

    x = row ^ col
    lvl = jnp.zeros((chunk, chunk), jnp.int32)
    for b in range(n_levels):
        lvl = lvl + (x >= (1 << b)).astype(jnp.int32)
    lvl = jnp.where(jnp.logical_or(before, x == 0), lvl, -1)

    end = 0 if reverse else chunk - 1
    if fuse_out:
        gn = gn_ref[...]

    def pair_weights(j):
        sl = slice(j * dk, (j + 1) * dk)
        q = _silu(q_ref[:, sl])
        k = k_all[:, sl]
        c = c_all[:, sl]
        fg = fg_all[:, sl]
        q_bf, k_bf = q.astype(BF16), k.astype(BF16)
        a = jnp.where(lvl == 0, _dot_nt(q_bf, k_bf), 0.0)
        for level in range(1, low_levels + 1):
            w_q, w_k = _level_weights(c, fg, level, reverse)
            a_l = _dot_nt((q * w_q).astype(BF16), k_bf if w_k is None else (k * w_k).astype(BF16))
            a = jnp.where(lvl == level, a_l, a)
        for level in range(low_levels + 1, n_levels + 1):
            half = 1 << (level - 1)
            w, _ = _level_weights(c, fg, level, reverse)
            q_rows, k_rows = [], []
            for b0 in range(0, chunk, 2 * half):
                early, late = (b0 + half, b0) if reverse else (b0, b0 + half)
                q_rows.append(q[late:late + half] * w[late:late + half])
                k_scaled = k[early:early + half] * w[early:early + half]
                k_rows += [k[late:late + half], k_scaled] if reverse else [k_scaled, k[late:late + half]]
            p = _dot_nt(jnp.concatenate(q_rows, axis=0).astype(BF16),
                        jnp.concatenate(k_rows, axis=0).astype(BF16))
            pieces = []
            for b, b0 in enumerate(range(0, chunk, 2 * half)):
                early, late = (b0 + half, b0) if reverse else (b0, b0 + half)
                upd = jnp.where(lvl[late:late + half] == level, p[b * half:(b + 1) * half], a[late:late + half])
                pieces += [upd, a[early:early + half]] if reverse else [a[early:early + half], upd]
            a = jnp.concatenate(pieces, axis=0)
        return q, k, c, a

    def outputs_and_state(j, q, k, c, a):
        sv = slice(j * dv, (j + 1) * dv)
        v = v_ref[:, sv].astype(BF16)
        c_end = c[end:end + 1, :]
        st = st_ref[j]
        o = _dot(a.astype(BF16), v) + _dot_nt((q * jnp.exp2(c)).astype(BF16), st.astype(BF16))
        if fuse_out:
            o = o + ofwd_ref[:, sv]
            y = o * lax.rsqrt(jnp.mean(o * o, axis=-1, keepdims=True) + EPS) * gn
            o = y * _silu(g_ref[:, sv])
        o_ref[:, sv] = o.astype(o_ref.dtype)
        k_dec = (k * jnp.exp2(c_end - c)).astype(BF16)
        st_ref[j] = st * jnp.exp2(c_end) + _dot_tn(v, k_dec)

    pending = [pair_weights(j) for j in range(min(HG_LOOKAHEAD, heads))]
    for j in range(heads):
        if j + HG_LOOKAHEAD < heads:
            pending.append(pair_weights(j + HG_LOOKAHEAD))
        outputs_and_state(j, *pending.pop(0))

    @pl.when(jnp.logical_and(last, jnp.logical_not(is_lat)))
    def _():
        for j in range(heads):
            sfin_ref[0, j] = st_ref[j].T


def _hg_scan(proj_qv, proj_fg, lb_logits_dir, s0_lat, *, groups, batch, seq, n_heads, dk, dv, d, reverse, layer,
             o_fwd=None, g_norm=None, mod_rider=None):
    chunk = _tile(math.gcd(seq, groups.dec_seq), HG_CHUNK)
    heads = _tile(n_heads, HG_HEADS_PER_STEP)
    assert dk == dv and d == n_heads * dk
    w = heads * dk
    segs = d // w
    lay = _ScanLayout(n_ctx=groups.ctx_rows // chunk, per_ctx=seq // chunk, per_lat=groups.dec_seq // chunk,
                      total=groups.rows // chunk, batch=batch, reverse=reverse)
    fuse_out = o_fwd is not None

    def seg_spec(seg):
        return pl.BlockSpec((chunk, w), lambda h, g: (lay.pos(g)[0], seg * segs + h))

    n_lb = lb_logits_dir.shape[0]
    in_specs = [seg_spec(0), seg_spec(1), seg_spec(1 if reverse else 0),
                pl.BlockSpec((n_lb, w), lambda h, g: (0, h)),
                pl.BlockSpec((1, heads, dk, dv), lambda h, g: (lay.pos(g)[3], h, 0, 0))]
    args = [proj_qv, proj_qv, proj_fg, lb_logits_dir, s0_lat]
    if fuse_out:
        in_specs += [seg_spec(0), seg_spec(2), pl.BlockSpec((1, dv), lambda h, g: (0, 0))]
        args += [o_fwd, proj_fg, g_norm.reshape(1, dv)]
    out_specs = [
        pl.BlockSpec((chunk, w), lambda h, g: (lay.pos(g)[0], h)),
        pl.BlockSpec((1, heads, dk, dv), lambda h, g: (lay.pos(g)[2], h, 0, 0)),
    ]
    out_shape = [
        jax.ShapeDtypeStruct((groups.rows, d), BF16 if fuse_out else F32),
        jax.ShapeDtypeStruct((batch, n_heads, dk, dv), F32),
    ]
    if mod_rider is not None:
        cond, w_mod, b_mod, mod_layer, col0, ncols = mod_rider
        n_steps = (n_heads // heads) * lay.total
        tc = next(t for t in range(V7X_LANES, ncols + 1, V7X_LANES)
                  if ncols % t == 0 and col0 % t == 0 and ncols // t <= n_steps)

        def tile_of(h, g):
            return jnp.minimum(h * lay.total + g, ncols // tc - 1)

        in_specs += [pl.BlockSpec(cond.shape, lambda h, g: (0, 0)),
                     pl.BlockSpec((None, d, tc), lambda h, g: (mod_layer, 0, col0 // tc + tile_of(h, g))),
                     pl.BlockSpec((None, 1, tc), lambda h, g: (mod_layer, 0, col0 // tc + tile_of(h, g)))]
        args += [cond, w_mod, b_mod]
        out_specs.append(pl.BlockSpec((cond.shape[0], tc), lambda h, g: (0, tile_of(h, g))))
        out_shape.append(jax.ShapeDtypeStruct((cond.shape[0], ncols), F32))
    return pl.pallas_call(
        functools.partial(_hg_scan_kernel, chunk=chunk, heads=heads, dk=dk, dv=dv,
                          layer=layer, lay=lay, fuse_out=fuse_out, mod_rider=mod_rider is not None),
        grid=(n_heads // heads, lay.total),
        in_specs=in_specs,
        out_specs=out_specs,
        out_shape=out_shape,
        scratch_shapes=[pltpu.VMEM((heads, dv, dk), F32)],
        compiler_params=_params(("arbitrary" if mod_rider is not None else "parallel", "arbitrary")),
        name="hgrn2_scan_" + ("bwd" if reverse else "fwd"),
    )(*args)


def _rope_tables(seq, hd):
    half = hd // 2
    rows = seq // GRID_W
    r_pos = jnp.repeat(jnp.arange(rows), GRID_W).astype(F32)
    c_pos = jnp.tile(jnp.arange(GRID_W), rows).astype(F32)
    inv_freq = ROPE_BASE ** (-jnp.arange(0, half, 2, dtype=F32) / half)
    ang_r = r_pos[:, None] * inv_freq
    ang_c = c_pos[:, None] * inv_freq
    cos = jnp.concatenate([jnp.cos(ang_r)] * 2 + [jnp.cos(ang_c)] * 2, axis=-1)
    sin = jnp.concatenate([-jnp.sin(ang_r), jnp.sin(ang_r), -jnp.sin(ang_c), jnp.sin(ang_c)], axis=-1)
    return cos, sin


def _rope_tile(x, cos, sin):
    hd = x.shape[1]
    quarter = hd // 4
    lane = lax.broadcasted_iota(jnp.int32, x.shape, 1)
    first = (lane % (2 * quarter)) < quarter
    partner = jnp.where(first, pltpu.roll(x, hd - quarter, 1), pltpu.roll(x, quarter, 1))
    return x * cos + partner * sin


def _attn_kernel(*refs, hd, heads, tq, sub, lookahead, past, scale, lam_init, latent):
    if latent:
        (lam_ref, q_ref, k_ref, v_ref, kc_ref, vc_ref, cos_ref, sin_ref, g_ref, _,
         o_ref, kall_ref, vall_ref) = refs
    else:
        lam_ref, q_ref, k_ref, v_ref, g_ref, _, o_ref, knew_ref, vnew_ref = refs
        knew_ref[...] = k_ref[...]
        vnew_ref[...] = v_ref[...]
    w = 2 * hd
    lm = lam_ref[...]
    lam = (jnp.exp(jnp.sum(lm[0:1] * lm[1:2], axis=-1, keepdims=True))
           - jnp.exp(jnp.sum(lm[2:3] * lm[3:4], axis=-1, keepdims=True)) + lam_init)
    g = g_ref[...]

    if latent:
        i = pl.program_id(2)

        @pl.when(i == 0)
        def _():
            kall_ref[0:past, :] = kc_ref[...].astype(BF16)
            vall_ref[0:past, :] = vc_ref[...].astype(BF16)
            vall_ref[past:, :] = v_ref[...].astype(BF16)
            for half in range(2):
                hs = slice(half * hd, (half + 1) * hd)
                kall_ref[past:, hs] = _rope_tile(k_ref[:, hs], cos_ref[...], sin_ref[...]).astype(BF16)

    def softmax_parts(qh, kh):
        s = _dot_nt(qh, kh)
        p = jnp.exp2(s - jnp.max(s, axis=-1, keepdims=True))
        return p.astype(BF16), 1.0 / jnp.sum(p, axis=-1, keepdims=True)

    units = [(j, r0) for j in range(heads) for r0 in range(0, tq, sub)]

    def probabilities(j, r0):
        parts = []
        for half in range(2):
            hs = slice(j * w + half * hd, j * w + (half + 1) * hd)
            qh = q_ref[r0:r0 + sub, hs]
            if latent:
                rows = pl.ds(pl.multiple_of(i * tq + r0, sub), sub)
                qh = _rope_tile(qh, cos_ref[rows, :], sin_ref[rows, :])
                kh = kall_ref[:, hs]
            else:
                kh = k_ref[:, hs].astype(BF16)
            parts.append(softmax_parts((qh * (scale * LOG2E)).astype(BF16), kh))
        return parts

    def output(j, r0, parts):
        (p1, r1), (p2, r2) = parts
        js = slice(j * w, (j + 1) * w)
        vj = vall_ref[...] if latent else v_ref[:, js].astype(BF16)
        o = r1 * _dot(p1, vj) - (lam * r2) * _dot(p2, vj)
        y = o * lax.rsqrt(jnp.mean(o * o, axis=-1, keepdims=True) + EPS) * g
        o_ref[r0:r0 + sub, js] = (y * (1.0 - lam_init)).astype(o_ref.dtype)

    pending = [probabilities(*u) for u in units[:lookahead]]
    for n, unit in enumerate(units):
        if n + lookahead < len(units):
            pending.append(probabilities(*units[n + lookahead]))
        output(*unit, pending.pop(0))


def _diff_attention(proj, lam_params, subln_g, *, batch, seq, row_off, n_heads, hd, lam_init,
                    cache_k=None, cache_v=None, prev=None):
    latent = cache_k is not None
    w = 2 * hd
    d = n_heads * w
    tq = _tile(seq, ATTN_TQ)
    nq = seq // tq
    heads = 1 if latent else _tile(n_heads, ATTN_CTX_HEADS_PER_STEP)
    assert row_off % seq == 0
    q_off, kv_off = row_off // tq, row_off // seq
    segs = n_heads // heads
    scale = hd ** -0.5
    past = cache_k.shape[0] // batch if latent else 0

    def kv_spec(rows, seg=0, off=0):
        return pl.BlockSpec((rows, heads * w), lambda b, h, i: (off + b, seg * segs + h))

    in_specs = [pl.BlockSpec(lam_params.shape, lambda b, h, i: (0, 0)),
                pl.BlockSpec((tq, heads * w), lambda b, h, i: (q_off + b * nq + i, h)),
                kv_spec(seq, 1, kv_off), kv_spec(seq, 2, kv_off)]
    args = [lam_params, proj, proj, proj]
    scratch = []
    if latent:
        cos, sin = _rope_tables(seq, hd)
        table = pl.BlockSpec((seq, hd), lambda b, h, i: (0, 0))
        in_specs += [kv_spec(past), kv_spec(past), table, table]
        args += [cache_k, cache_v, cos, sin]
        scratch = [pltpu.VMEM((past + seq, w), BF16), pltpu.VMEM((past + seq, w), BF16)]
    in_specs.append(pl.BlockSpec((1, w), lambda b, h, i: (0, 0)))
    args.append(subln_g.reshape(1, w))
    out_specs = [pl.BlockSpec((tq, heads * w), lambda b, h, i: (q_off + b * nq + i, h))]
    out_shape = [jax.ShapeDtypeStruct((proj.shape[0], d), BF16)]
    in_specs.append(pl.BlockSpec(memory_space=pl.ANY))
    args.append(prev)
    aliases = {len(args) - 1: 0}
    if not latent:
        out_specs += [kv_spec(seq), kv_spec(seq)]
        out_shape += [jax.ShapeDtypeStruct((batch * seq, d), F32)] * 2
    outs = pl.pallas_call(
        functools.partial(_attn_kernel, hd=hd, heads=heads, tq=tq, sub=_tile(tq, ATTN_SUB_TQ),
                          lookahead=1 if latent else 0, past=past, scale=scale, lam_init=lam_init,
                          latent=latent),
        grid=(batch, n_heads // heads, nq),
        in_specs=in_specs,
        out_specs=out_specs,
        out_shape=out_shape,
        scratch_shapes=scratch,
        input_output_aliases=aliases,
        compiler_params=_params(("parallel", "parallel", "arbitrary")),
        name="diff_attention_" + ("latent" if latent else "context"),
    )(*args)
    return outs[0] if latent else outs


def kernel(x_prompt, x_sample, state_hgrn_fwd, state_hgrn_bwd, cache_attn_k, cache_attn_v, c, c_ctx,
           w_mod, b_mod, norm_g, w_ff1, w_ff2, hg_w_in, hg_w_out, hg_g_norm, hg_lb_logits,
           da_w_in, da_w_out, da_subln_g, da_lambda, final_norm_g):
    batch, seq, d = x_prompt.shape
    dec_batch, dec_seq, _ = x_sample.shape
    depth = w_mod.shape[0]
    _, _, hg_heads, hg_dk, hg_dv = state_hgrn_fwd.shape
    _, _, past_len, da_halves, da_hd = cache_attn_k.shape
    da_heads = da_halves // 2
    ctx_rows = batch * seq
    lat_rows = dec_batch * dec_seq
    groups = _Groups(ctx_rows, dec_batch, dec_seq)

    x = (x_prompt.reshape(ctx_rows, d), x_sample.reshape(lat_rows, d))

    n_groups = 1 + dec_batch
    g8 = -(-n_groups // V7X_SUBLANES) * V7X_SUBLANES
    cond = jnp.concatenate([c_ctx[None, :], c, jnp.zeros((g8 - n_groups, d), F32)], axis=0)
    b_mod3 = b_mod.reshape(depth, 1, 6 * d)
    mods = {layer: _modulation(cond, w_mod, b_mod3, layer).reshape(g8, 1, 6 * d)
            for layer in range(depth) if layer == 0 or (layer - 1) % N_MIXERS != 0}
    SH1, SC1, GT1, SH2, SC2, GT2 = range(6)

    def mixer_weights(layer):
        j = layer // N_MIXERS
        if layer % N_MIXERS == 0:
            return [_Cast(hg_w_in, j, 0, 2 * d), _Cast(hg_w_in, j, 2 * d, 3 * d)], _Cast(hg_w_out, j)
        return [_Cast(da_w_in, j)], _Cast(da_w_out, j)

    first_parts, _ = mixer_weights(0)
    w_in = [_cast_weight(first_parts[0])] + first_parts[1:]
    new_sf, new_sb, new_k, new_v = [], [], [], []
    for layer in range(depth):
        mod = mods[layer]
        h = _norm_mod(x, norm_g[layer, 0], mod, SC1, SH1, groups)
        j = layer // N_MIXERS
        out_casts = (mixer_weights(layer)[1], _Cast(w_ff1, layer))
        projs = []
        for n in range(len(w_in)):
            last = n + 1 == len(w_in)
            casts = out_casts if last else tuple(w for w in w_in[n + 1:n + 2] if isinstance(w, _Cast))
            out = _matmul(h, w_in[n], F32, casts=casts)
            if last:
                out, w_out, w1 = out
            elif casts:
                out, w_in[n + 1] = out
            projs.append(out)
        if layer % N_MIXERS == 0:
            scan = dict(groups=groups, batch=batch, seq=seq, n_heads=hg_heads, dk=hg_dk, dv=hg_dv, d=d,
                        layer=layer)
            riders = [None, None]
            if layer + 1 < depth:
                riders = [(cond, w_mod, b_mod3, layer + 1, half * 3 * d, 3 * d) for half in range(2)]
            o_fwd, s_fwd, *m_fwd = _hg_scan(*projs, hg_lb_logits[0], state_hgrn_fwd[:, j], reverse=False,
                                            mod_rider=riders[0], **scan)
            mixed, s_bwd, *m_bwd = _hg_scan(*projs, hg_lb_logits[1], state_hgrn_bwd[:, j], reverse=True,
                                            o_fwd=o_fwd, g_norm=hg_g_norm[j], mod_rider=riders[1], **scan)
            if layer + 1 < depth:
                mods[layer + 1] = jnp.concatenate(m_fwd + m_bwd, axis=-1).reshape(g8, 1, 6 * d)
            new_sf.append(s_fwd)
            new_sb.append(s_bwd)
        else:
            proj, = projs
            lam_init = 0.8 - 0.6 * math.exp(-0.3 * layer)
            att = dict(n_heads=da_heads, hd=da_hd, lam_init=lam_init)
            mixed, k_ctx, v_ctx = _diff_attention(proj, da_lambda[j], da_subln_g[j], batch=batch, seq=seq,
                                                  row_off=0, prev=h, **att)
            new_k.append(k_ctx.reshape(batch, seq, da_halves, da_hd))
            new_v.append(v_ctx.reshape(batch, seq, da_heads, 2 * da_hd))
            mixed = _diff_attention(proj, da_lambda[j], da_subln_g[j], batch=dec_batch, seq=dec_seq,
                                    row_off=ctx_rows,
                                    cache_k=cache_attn_k[:, j].reshape(dec_batch * past_len, d),
                                    cache_v=cache_attn_v[:, j].reshape(dec_batch * past_len, d),
                                    prev=mixed, **att)
        x = _matmul(mixed, w_out, F32, "residual", res=x, mod=mod, gate_chunk=GT1, groups=groups)
        h = _norm_mod(x, norm_g[layer, 1], mod, SC2, SH2, groups)
        next_parts = mixer_weights(layer + 1)[0] if layer + 1 < depth else []
        u, w2, *w_in = _matmul(h, w1, BF16, "relu2", casts=(_Cast(w_ff2, layer), *next_parts))
        x = _matmul(u, w2, F32, "residual", res=x, mod=mod, gate_chunk=GT2, groups=groups)

    y_prompt = _final_norm(x, final_norm_g, 0, ctx_rows).reshape(batch, seq, d)
    y_sample = _final_norm(x, final_norm_g, ctx_rows, lat_rows).reshape(dec_batch, dec_seq, d)
    return (y_prompt, y_sample, jnp.stack(new_sf, axis=1), jnp.stack(new_sb, axis=1),
            jnp.stack(new_k, axis=1), jnp.stack(new_v, axis=1))
```

```python
import functools
import math
from typing import NamedTuple

import jax
import jax.numpy as jnp
from jax import lax
from jax.experimental import pallas as pl
from jax.experimental.pallas import tpu as pltpu

F32 = jnp.float32
BF16 = jnp.bfloat16

GRID_W = 64
EPS = 1e-6
ROPE_BASE = 10000.0
N_MIXERS = 2

V7X_SUBLANES = 8
V7X_LANES = 128
V7X_VMEM_BYTES = 64 * 1024 * 1024
COMPILER_RESERVED_VMEM_BYTES = 4 * 1024 * 1024
V7X_VMEM_LIMIT_BYTES = V7X_VMEM_BYTES - COMPILER_RESERVED_VMEM_BYTES

MM_BM = 1024
MM_BN = 1024
MM_VMEM_BUDGET_BYTES = 52 * 1024 * 1024
CAST_VMEM_BUDGET_BYTES = 10 * 1024 * 1024
CAST_TILE_COLS = 1024
ROW_TILE = 512
MOD_BN = 1024
HG_CHUNK = 128
HG_HEADS_PER_STEP = 32
HG_LOOKAHEAD = 1
HG_GATE_SLAB_HEADS = 8
ATTN_TQ = 1024
ATTN_SUB_TQ = 256
ATTN_CTX_HEADS_PER_STEP = 16

LOG2E = 1.4426950408889634


def _tile(full, want):
    t = min(full, want)
    assert full % t == 0, (full, want)
    return t


def _params(sem):
    return pltpu.CompilerParams(dimension_semantics=sem, vmem_limit_bytes=V7X_VMEM_LIMIT_BYTES)


def _silu(x):
    half = 0.5 * x
    return half + half * jnp.tanh(half)


def _dot(a, b):
    return jnp.dot(a, b, preferred_element_type=F32)


def _dot_nt(a, b):
    return lax.dot_general(a, b, (((1,), (1,)), ((), ())), preferred_element_type=F32)


def _dot_tn(a, b):
    return lax.dot_general(a, b, (((0,), (0,)), ((), ())), preferred_element_type=F32)


class _Groups:
    def __init__(self, ctx_rows, dec_batch, dec_seq):
        self.ctx_rows = ctx_rows
        self.dec_batch = dec_batch
        self.dec_seq = dec_seq
        self.rows = ctx_rows + dec_batch * dec_seq

    def of_tile(self, i, tile):
        assert self.ctx_rows % tile == 0 and self.dec_seq % tile == 0
        g = jnp.int32(0)
        for b in range(self.dec_batch):
            g = g + (i >= (self.ctx_rows + b * self.dec_seq) // tile).astype(jnp.int32)
        return g


def _mod_kernel(c_ref, w_ref, b_ref, o_ref):
    a = _silu(c_ref[...]).astype(BF16)
    o_ref[...] = _dot(a, w_ref[...].astype(BF16)) + b_ref[...]


def _modulation(cond, w_mod, b_mod, layer):
    g8, d = cond.shape
    n = w_mod.shape[2]
    bn = _tile(n, MOD_BN)
    return pl.pallas_call(
        _mod_kernel,
        grid=(n // bn,),
        in_specs=[
            pl.BlockSpec((g8, d), lambda j: (0, 0)),
            pl.BlockSpec((None, d, bn), lambda j: (layer, 0, j)),
            pl.BlockSpec((None, 1, bn), lambda j: (layer, 0, j)),
        ],
        out_specs=pl.BlockSpec((g8, bn), lambda j: (0, j)),
        out_shape=jax.ShapeDtypeStruct((g8, n), F32),
        compiler_params=_params(("arbitrary",)),
        name="adaln_modulation",
    )(cond, w_mod, b_mod)


def _norm_kernel(*refs, modulated, x_split):
    if modulated:
        *x_refs, g_ref, sc_ref, sh_ref, o_ref = refs
    else:
        *x_refs, g_ref, o_ref = refs
    x = x_refs[0][...]
    if len(x_refs) == 2:
        x = jnp.where(pl.program_id(0) < x_split, x, x_refs[1][...])
    y = x * lax.rsqrt(jnp.mean(x * x, axis=-1, keepdims=True) + EPS) * g_ref[...]
    if modulated:
        y = y * (1.0 + sc_ref[...]) + sh_ref[...]
    o_ref[...] = y.astype(o_ref.dtype)


def _norm_mod(x, g, mod, sc_chunk, sh_chunk, groups):
    rows, d = groups.rows, g.shape[0]
    tm = _tile(math.gcd(groups.ctx_rows, groups.dec_seq), ROW_TILE)

    def vec(chunk):
        return pl.BlockSpec((None, 1, d), lambda i: (groups.of_tile(i, tm), 0, chunk))

    if isinstance(x, tuple):
        x_split = x[0].shape[0] // tm
        x_specs = [pl.BlockSpec((tm, d), lambda i: (jnp.minimum(i, x_split - 1), 0)),
                   pl.BlockSpec((tm, d), lambda i: (jnp.maximum(i - x_split, 0), 0))]
        xs = list(x)
    else:
        x_split = 0
        x_specs = [pl.BlockSpec((tm, d), lambda i: (i, 0))]
        xs = [x]
    return pl.pallas_call(
        functools.partial(_norm_kernel, modulated=True, x_split=x_split),
        grid=(rows // tm,),
        in_specs=x_specs + [
            pl.BlockSpec((1, d), lambda i: (0, 0)),
            vec(sc_chunk),
            vec(sh_chunk),
        ],
        out_specs=pl.BlockSpec((tm, d), lambda i: (i, 0)),
        out_shape=jax.ShapeDtypeStruct((rows, d), BF16),
        compiler_params=_params(("parallel",)),
        name="rmsnorm_modulate",
    )(*xs, g.reshape(1, d), mod, mod)


def _final_norm(x, g, row_off, rows):
    d = x.shape[1]
    tm = _tile(math.gcd(rows, row_off) if row_off else rows, ROW_TILE)
    off = row_off // tm
    return pl.pallas_call(
        functools.partial(_norm_kernel, modulated=False, x_split=0),
        grid=(rows // tm,),
        in_specs=[
            pl.BlockSpec((tm, d), lambda i: (i + off, 0)),
            pl.BlockSpec((1, d), lambda i: (0, 0)),
        ],
        out_specs=pl.BlockSpec((tm, d), lambda i: (i, 0)),
        out_shape=jax.ShapeDtypeStruct((rows, d), F32),
        compiler_params=_params(("parallel",)),
        name="final_rmsnorm",
    )(x, g.reshape(1, d))


def _mm_kernel(*refs, nk, epilogue, n_res, res_split, n_casts):
    x_ref, w_ref = refs[:2]
    n_in = 2 + (n_res + 1 if epilogue == "residual" else 0)
    res_refs = refs[2:2 + n_res]
    gate_ref = refs[n_in - 1]
    o_ref = refs[n_in + n_casts]
    for src_ref, dst_ref in zip(refs[n_in:n_in + n_casts], refs[n_in + n_casts + 1:]):
        dst_ref[...] = src_ref[...].astype(dst_ref.dtype)

    def finish(acc):
        if epilogue == "relu2":
            r = jnp.maximum(acc, 0.0)
            acc = r * r
        elif epilogue == "residual":
            res = res_refs[0][...]
            if len(res_refs) == 2:
                res = jnp.where(pl.program_id(0) < res_split, res, res_refs[1][...])
            acc = res + gate_ref[...] * acc
        return acc

    def product():
        return _dot(x_ref[...], w_ref[...])

    if nk == 1:
        o_ref[...] = finish(product()).astype(o_ref.dtype)
        return

    k = pl.program_id(2)

    @pl.when(k == 0)
    def _():
        o_ref[...] = product()

    @pl.when(jnp.logical_and(k > 0, k < nk - 1))
    def _():
        o_ref[...] += product()

    @pl.when(k == nk - 1)
    def _():
        o_ref[...] = finish(o_ref[...] + product())


class _Cast(NamedTuple):
    src: jax.Array
    lead: int
    col0: int = 0
    ncols: int = 0

    @property
    def shape(self):
        return self.src.shape[1], self.ncols or self.src.shape[2]


def _cast_kernel(s_ref, o_ref):
    o_ref[...] = s_ref[...].astype(o_ref.dtype)


def _cast_weight(c):
    rows, cols = c.shape
    tr, tc = _tile(rows, CAST_TILE_COLS), _tile(cols, 2 * CAST_TILE_COLS)
    assert c.col0 % tc == 0
    j0 = c.col0 // tc
    return pl.pallas_call(
        _cast_kernel,
        grid=(rows // tr, cols // tc),
        in_specs=[pl.BlockSpec((None, tr, tc), lambda i, j: (c.lead, i, j0 + j))],
        out_specs=pl.BlockSpec((tr, tc), lambda i, j: (i, j)),
        out_shape=jax.ShapeDtypeStruct((rows, cols), BF16),
        compiler_params=_params(("parallel", "parallel")),
        name="cast_weight",
    )(c.src)


def _matmul(x, w, out_dtype, epilogue="plain", *, res=None, mod=None, gate_chunk=None, groups=None, casts=()):
    m, kdim = x.shape
    n = w.shape[1]
    bm = _tile(m, MM_BM)
    n_res = 0 if epilogue != "residual" else (2 if isinstance(res, tuple) else 1)
    out_bytes = jnp.dtype(out_dtype).itemsize + 4 * n_res

    def fits(bn, bk):
        return 2 * (2 * bk * (bm + bn) + bm * bn * out_bytes) + 4 * bm * bn <= MM_VMEM_BUDGET_BYTES

    bn_full = _tile(n, MM_BN)
    candidates = [(bn_full, kdim), (max(bn_full // 2, V7X_LANES), kdim)]
    candidates += [(bn_full, kdim // s) for s in (2, 4, 8, 16, 32)]
    bn, bk = next((bn_, bk_) for bn_, bk_ in candidates if kdim % bk_ == 0 and fits(bn_, bk_))
    nk = kdim // bk
    nj = n // bn
    assert nk == 1 or out_dtype == F32
    in_specs = [
        pl.BlockSpec((bm, bk), lambda i, j, k: (i, k)),
        pl.BlockSpec((bk, bn), lambda i, j, k: (k, j)),
    ]
    args = [x, w]
    res_split = 0
    if epilogue == "residual":
        nb = n // bn
        if isinstance(res, tuple):
            res_a, res_b = res
            res_split = res_a.shape[0] // bm
            assert res_a.shape[0] % bm == 0 and res_a.shape[0] + res_b.shape[0] == m
            in_specs += [
                pl.BlockSpec((bm, bn), lambda i, j, k: (jnp.minimum(i, res_split - 1),
                                                        jnp.where(i < res_split, j, nb - 1))),
                pl.BlockSpec((bm, bn), lambda i, j, k: (jnp.maximum(i - res_split, 0),
                                                        jnp.where(i < res_split, 0, j))),
            ]
            args += [res_a, res_b]
        else:
            in_specs.append(pl.BlockSpec((bm, bn), lambda i, j, k: (i, j)))
            args.append(res)
        in_specs.append(pl.BlockSpec((None, 1, bn),
                                     lambda i, j, k: (groups.of_tile(i, bm), 0, gate_chunk * nb + j)))
        args.append(mod)

    out_specs = [pl.BlockSpec((bm, bn), lambda i, j, k: (i, j))]
    out_shape = [jax.ShapeDtypeStruct((m, n), out_dtype)]
    n_steps = (m // bm) * nj * nk
    cast_bytes = 0
    for c in casts:
        rows, cols = c.shape
        assert cols % CAST_TILE_COLS == 0 and c.col0 % CAST_TILE_COLS == 0
        tiles_c = cols // CAST_TILE_COLS
        tile_rows = next(tr for tr in (128, 256, 512, 1024, 2048)
                         if rows % tr == 0 and (rows // tr) * tiles_c <= n_steps)
        n_tiles = (rows // tile_rows) * tiles_c
        cast_bytes += 2 * (4 + 2) * tile_rows * CAST_TILE_COLS
        assert cast_bytes <= CAST_VMEM_BUDGET_BYTES

        def tile_of(i, j, k, n_tiles=n_tiles, tiles_c=tiles_c):
            t = jnp.minimum((i * nj + j) * nk + k, n_tiles - 1)
            return t // tiles_c, t % tiles_c

        def src_tile_of(i, j, k, c=c, tile_of=tile_of):
            r, t = tile_of(i, j, k)
            return c.lead, r, c.col0 // CAST_TILE_COLS + t

        in_specs.append(pl.BlockSpec((None, tile_rows, CAST_TILE_COLS), src_tile_of))
        args.append(c.src)
        out_specs.append(pl.BlockSpec((tile_rows, CAST_TILE_COLS), tile_of))
        out_shape.append(jax.ShapeDtypeStruct((rows, cols), BF16))

    outs = pl.pallas_call(
        functools.partial(_mm_kernel, nk=nk, epilogue=epilogue, n_res=n_res, res_split=res_split,
                          n_casts=len(casts)),
        grid=(m // bm, nj, nk),
        in_specs=in_specs,
        out_specs=out_specs,
        out_shape=out_shape,
        compiler_params=_params(("arbitrary", "arbitrary", "arbitrary")),
        name="matmul_" + epilogue,
    )(*args)
    return outs[0] if not casts else outs


class _ScanLayout(NamedTuple):
    n_ctx: int
    per_ctx: int
    per_lat: int
    total: int
    batch: int
    reverse: bool

    def pos(self, g):
        r = self.total - 1 - g if self.reverse else g
        is_lat = r >= self.n_ctx
        rl = jnp.maximum(r - self.n_ctx, 0)
        seq_ctx = jnp.minimum(r // self.per_ctx, self.batch - 1)
        seq_lat = rl // self.per_lat
        pos = jnp.where(is_lat, rl % self.per_lat, r % self.per_ctx)
        per = jnp.where(is_lat, self.per_lat, self.per_ctx)
        start, stop = (per - 1, 0) if self.reverse else (0, per - 1)
        return r, is_lat, seq_ctx, seq_lat, pos == start, pos == stop


def _level_ref(c, half, reverse):
    rows, width = c.shape
    blk = 2 * half
    idx = half if reverse else half - 1
    assert blk % V7X_SUBLANES == 0
    c3 = c.reshape(rows // blk, blk, width)
    ref = jnp.broadcast_to(c3[:, idx:idx + 1, :], c3.shape)
    return ref.reshape(rows, width)


def _exp2_neg_abs(d):
    bits = lax.bitcast_convert_type(d, jnp.uint32) | jnp.uint32(0x80000000)
    return jnp.exp2(lax.bitcast_convert_type(bits, F32))


def _level_weights(c, fg, level, reverse):
    rows = c.shape[0]
    if level > 2:
        w = _exp2_neg_abs(c - _level_ref(c, 1 << (level - 1), reverse))
        return w, w
    if level == 1:
        return fg, None
    pos = lax.broadcasted_iota(jnp.int32, c.shape, 0) & 3
    prev = pltpu.roll(fg, 1, 0)
    nxt = pltpu.roll(fg, rows - 1, 0)
    if reverse:
        return fg * jnp.where(pos == 0, nxt, 1.0), jnp.where(pos == 3, prev, 1.0)
    return fg * jnp.where(pos == 3, prev, 1.0), jnp.where(pos == 0, nxt, 1.0)


def _hg_scan_kernel(*refs, chunk, heads, dk, dv, layer, lay, fuse_out, mod_rider):
    refs = list(refs)
    q_ref, v_ref, f_ref, lbl_ref, s0_ref = refs[:5]
    del refs[:5]
    if fuse_out:
        ofwd_ref, g_ref, gn_ref = refs[:3]
        del refs[:3]
    if mod_rider:
        cond_ref, wmod_ref, bmod_ref = refs[:3]
        del refs[:3]
        o_ref, sfin_ref, modr_ref, st_ref = refs
        modr_ref[...] = _dot(_silu(cond_ref[...]).astype(BF16), wmod_ref[...].astype(BF16)) + bmod_ref[...]
    else:
        o_ref, sfin_ref, st_ref = refs
    reverse = lay.reverse
    _, is_lat, _, _, first, last = lay.pos(pl.program_id(1))
    n_levels = chunk.bit_length() - 1
    low_levels = min(n_levels, V7X_SUBLANES.bit_length() - 1)

    @pl.when(jnp.logical_and(first, is_lat))
    def _():
        for j in range(heads):
            st_ref[j] = s0_ref[0, j].T

    @pl.when(jnp.logical_and(first, jnp.logical_not(is_lat)))
    def _():
        st_ref[...] = jnp.zeros(st_ref.shape, F32)

    lbl = lbl_ref[...]
    e = jnp.exp(lbl - jnp.max(lbl, axis=0, keepdims=True))
    lb = jnp.sum(e[:layer + 1], axis=0, keepdims=True) / jnp.sum(e, axis=0, keepdims=True)

    row = lax.broadcasted_iota(jnp.int32, (chunk, chunk), 0)
    col = lax.broadcasted_iota(jnp.int32, (chunk, chunk), 1)
    before = (col > row) if reverse else (col < row)
    tri = jnp.logical_or(before, col == row).astype(BF16)

    slab_heads = min(heads, HG_GATE_SLAB_HEADS)
    slabs = {}

    def gates(si):
        if si not in slabs:
            ls = slice(si * slab_heads * dk, (si + 1) * slab_heads * dk)
            f = f_ref[:, ls]
            lb_s = lb[:, ls]
            e_f = _exp2_neg_abs(f * LOG2E)
            r = 1.0 / (1.0 + e_f)
            fg = lb_s + (1.0 - lb_s) * jnp.where(f >= 0.0, r, e_f * r)
            log2f = jnp.log2(fg)
            hi = log2f.astype(BF16)
            r1 = log2f - hi.astype(F32)
            mid = r1.astype(BF16)
            lo = (r1 - mid.astype(F32)).astype(BF16)
            slabs[si] = fg, 1.0 - fg, _dot(tri, hi) + _dot(tri, mid) + _dot(tri, lo)
        return slabs[si]

    x = row ^ col
    lvl = jnp.zeros((chunk, chunk), jnp.int32)
    for b in range(n_levels):
        lvl = lvl + (x >= (1 << b)).astype(jnp.int32)
    lvl = jnp.where(jnp.logical_or(before, x == 0), lvl, -1)

    end = 0 if reverse else chunk - 1
    if fuse_out:
        gn = gn_ref[...]

    def pair_weights(j):
        sl = slice(j * dk, (j + 1) * dk)
        q = _silu(q_ref[:, sl])
        fg_s, k_s, c_s = gates(j // slab_heads)
        ss = slice(j % slab_heads * dk, (j % slab_heads + 1) * dk)
        k, c, fg = k_s[:, ss], c_s[:, ss], fg_s[:, ss]
        q_bf, k_bf = q.astype(BF16), k.astype(BF16)
        a = jnp.where(lvl == 0, _dot_nt(q_bf, k_bf), 0.0)
        for level in range(1, low_levels + 1):
            w_q, w_k = _level_weights(c, fg, level, reverse)
            a_l = _dot_nt((q * w_q).astype(BF16), k_bf if w_k is None else (k * w_k).astype(BF16))
            a = jnp.where(lvl == level, a_l, a)
        for level in range(low_levels + 1, n_levels + 1):
            half = 1 << (level - 1)
            w, _ = _level_weights(c, fg, level, reverse)
            q_rows, k_rows = [], []
            for b0 in range(0, chunk, 2 * half):
                early, late = (b0 + half, b0) if reverse else (b0, b0 + half)
                q_rows.append(q[late:late + half] * w[late:late + half])
                k_scaled = k[early:early + half] * w[early:early + half]
                k_rows += [k[late:late + half], k_scaled] if reverse else [k_scaled, k[late:late + half]]
            p = _dot_nt(jnp.concatenate(q_rows, axis=0).astype(BF16),
                        jnp.concatenate(k_rows, axis=0).astype(BF16))
            pieces = []
            for b, b0 in enumerate(range(0, chunk, 2 * half)):
                early, late = (b0 + half, b0) if reverse else (b0, b0 + half)
                upd = jnp.where(lvl[late:late + half] == level, p[b * half:(b + 1) * half], a[late:late + half])
                pieces += [upd, a[early:early + half]] if reverse else [a[early:early + half], upd]
            a = jnp.concatenate(pieces, axis=0)
        return q, k, c, a

    def outputs_and_state(j, q, k, c, a):
        sv = slice(j * dv, (j + 1) * dv)
        v = v_ref[:, sv].astype(BF16)
        c_end = c[end:end + 1, :]
        st = st_ref[j]
        o = _dot(a.astype(BF16), v) + _dot_nt((q * jnp.exp2(c)).astype(BF16), st.astype(BF16))
        if fuse_out:
            o = o + ofwd_ref[:, sv]
            y = o * lax.rsqrt(jnp.mean(o * o, axis=-1, keepdims=True) + EPS) * gn
            o = y * _silu(g_ref[:, sv])
        o_ref[:, sv] = o.astype(o_ref.dtype)
        k_dec = (k * jnp.exp2(c_end - c)).astype(BF16)
        st_ref[j] = st * jnp.exp2(c_end) + _dot_tn(v, k_dec)

    pending = [pair_weights(j) for j in range(min(HG_LOOKAHEAD, heads))]
    for j in range(heads):
        if j + HG_LOOKAHEAD < heads:
            pending.append(pair_weights(j + HG_LOOKAHEAD))
        outputs_and_state(j, *pending.pop(0))

    @pl.when(jnp.logical_and(last, jnp.logical_not(is_lat)))
    def _():
        for j in range(heads):
            sfin_ref[0, j] = st_ref[j].T


def _hg_scan(proj_qv, proj_fg, lb_logits_dir, s0_lat, *, groups, batch, seq, n_heads, dk, dv, d, reverse, layer,
             o_fwd=None, g_norm=None, mod_rider=None):
    chunk = _tile(math.gcd(seq, groups.dec_seq), HG_CHUNK)
    heads = _tile(n_heads, HG_HEADS_PER_STEP)
    assert dk == dv and d == n_heads * dk
    w = heads * dk
    segs = d // w
    lay = _ScanLayout(n_ctx=groups.ctx_rows // chunk, per_ctx=seq // chunk, per_lat=groups.dec_seq // chunk,
                      total=groups.rows // chunk, batch=batch, reverse=reverse)
    fuse_out = o_fwd is not None

    def seg_spec(seg):
        return pl.BlockSpec((chunk, w), lambda h, g: (lay.pos(g)[0], seg * segs + h))

    n_lb = lb_logits_dir.shape[0]
    in_specs = [seg_spec(0), seg_spec(1), seg_spec(1 if reverse else 0),
                pl.BlockSpec((n_lb, w), lambda h, g: (0, h)),
                pl.BlockSpec((1, heads, dk, dv), lambda h, g: (lay.pos(g)[3], h, 0, 0))]
    args = [proj_qv, proj_qv, proj_fg, lb_logits_dir, s0_lat]
    if fuse_out:
        in_specs += [seg_spec(0), seg_spec(2), pl.BlockSpec((1, dv), lambda h, g: (0, 0))]
        args += [o_fwd, proj_fg, g_norm.reshape(1, dv)]
    out_specs = [
        pl.BlockSpec((chunk, w), lambda h, g: (lay.pos(g)[0], h)),
        pl.BlockSpec((1, heads, dk, dv), lambda h, g: (lay.pos(g)[2], h, 0, 0)),
    ]
    out_shape = [
        jax.ShapeDtypeStruct((groups.rows, d), BF16 if fuse_out else F32),
        jax.ShapeDtypeStruct((batch, n_heads, dk, dv), F32),
    ]
    if mod_rider is not None:
        cond, w_mod, b_mod, mod_layer, col0, ncols = mod_rider
        n_steps = (n_heads // heads) * lay.total
        tc = next(t for t in range(V7X_LANES, ncols + 1, V7X_LANES)
                  if ncols % t == 0 and col0 % t == 0 and ncols // t <= n_steps)

        def tile_of(h, g):
            return jnp.minimum(h * lay.total + g, ncols // tc - 1)

        in_specs += [pl.BlockSpec(cond.shape, lambda h, g: (0, 0)),
                     pl.BlockSpec((None, d, tc), lambda h, g: (mod_layer, 0, col0 // tc + tile_of(h, g))),
                     pl.BlockSpec((None, 1, tc), lambda h, g: (mod_layer, 0, col0 // tc + tile_of(h, g)))]
        args += [cond, w_mod, b_mod]
        out_specs.append(pl.BlockSpec((cond.shape[0], tc), lambda h, g: (0, tile_of(h, g))))
        out_shape.append(jax.ShapeDtypeStruct((cond.shape[0], ncols), F32))
    return pl.pallas_call(
        functools.partial(_hg_scan_kernel, chunk=chunk, heads=heads, dk=dk, dv=dv,
                          layer=layer, lay=lay, fuse_out=fuse_out, mod_rider=mod_rider is not None),
        grid=(n_heads // heads, lay.total),
        in_specs=in_specs,
        out_specs=out_specs,
        out_shape=out_shape,
        scratch_shapes=[pltpu.VMEM((heads, dv, dk), F32)],
        compiler_params=_params(("arbitrary" if mod_rider is not None else "parallel", "arbitrary")),
        name="hgrn2_scan_" + ("bwd" if reverse else "fwd"),
    )(*args)


def _rope_tables(seq, hd):
    half = hd // 2
    rows = seq // GRID_W
    r_pos = jnp.repeat(jnp.arange(rows), GRID_W).astype(F32)
    c_pos = jnp.tile(jnp.arange(GRID_W), rows).astype(F32)
    inv_freq = ROPE_BASE ** (-jnp.arange(0, half, 2, dtype=F32) / half)
    ang_r = r_pos[:, None] * inv_freq
    ang_c = c_pos[:, None] * inv_freq
    cos = jnp.concatenate([jnp.cos(ang_r)] * 2 + [jnp.cos(ang_c)] * 2, axis=-1)
    sin = jnp.concatenate([-jnp.sin(ang_r), jnp.sin(ang_r), -jnp.sin(ang_c), jnp.sin(ang_c)], axis=-1)
    return cos, sin


def _rope_tile(x, cos, sin):
    hd = x.shape[1]
    quarter = hd // 4
    lane = lax.broadcasted_iota(jnp.int32, x.shape, 1)
    first = (lane % (2 * quarter)) < quarter
    partner = jnp.where(first, pltpu.roll(x, hd - quarter, 1), pltpu.roll(x, quarter, 1))
    return x * cos + partner * sin


def _attn_kernel(*refs, hd, heads, tq, sub, lookahead, past, scale, lam_init, latent):
    if latent:
        (lam_ref, q_ref, k_ref, v_ref, kc_ref, vc_ref, cos_ref, sin_ref, g_ref, _,
         o_ref, kall_ref, vall_ref) = refs
    else:
        lam_ref, q_ref, k_ref, v_ref, g_ref, _, o_ref, knew_ref, vnew_ref = refs
        knew_ref[...] = k_ref[...]
        vnew_ref[...] = v_ref[...]
    w = 2 * hd
    lm = lam_ref[...]
    lam = (jnp.exp(jnp.sum(lm[0:1] * lm[1:2], axis=-1, keepdims=True))
           - jnp.exp(jnp.sum(lm[2:3] * lm[3:4], axis=-1, keepdims=True)) + lam_init)
    g = g_ref[...]

    if latent:
        i = pl.program_id(2)

        @pl.when(i == 0)
        def _():
            kall_ref[0:past, :] = kc_ref[...].astype(BF16)
            vall_ref[0:past, :] = vc_ref[...].astype(BF16)
            vall_ref[past:, :] = v_ref[...].astype(BF16)
            for half in range(2):
                hs = slice(half * hd, (half + 1) * hd)
                kall_ref[past:, hs] = _rope_tile(k_ref[:, hs], cos_ref[...], sin_ref[...]).astype(BF16)

    def softmax_parts(qh, kh):
        s = _dot_nt(qh, kh)
        p = jnp.exp2(s - jnp.max(s, axis=-1, keepdims=True))
        return p.astype(BF16), 1.0 / jnp.sum(p, axis=-1, keepdims=True)

    units = [(j, r0) for j in range(heads) for r0 in range(0, tq, sub)]

    def probabilities(j, r0):
        parts = []
        for half in range(2):
            hs = slice(j * w + half * hd, j * w + (half + 1) * hd)
            qh = q_ref[r0:r0 + sub, hs]
            if latent:
                rows = pl.ds(pl.multiple_of(i * tq + r0, sub), sub)
                qh = _rope_tile(qh, cos_ref[rows, :], sin_ref[rows, :])
                kh = kall_ref[:, hs]
            else:
                kh = k_ref[:, hs].astype(BF16)
            parts.append(softmax_parts((qh * (scale * LOG2E)).astype(BF16), kh))
        return parts

    def output(j, r0, parts):
        (p1, r1), (p2, r2) = parts
        js = slice(j * w, (j + 1) * w)
        vj = vall_ref[...] if latent else v_ref[:, js].astype(BF16)
        o = r1 * _dot(p1, vj) - (lam * r2) * _dot(p2, vj)
        y = o * lax.rsqrt(jnp.mean(o * o, axis=-1, keepdims=True) + EPS) * g
        o_ref[r0:r0 + sub, js] = (y * (1.0 - lam_init)).astype(o_ref.dtype)

    pending = [probabilities(*u) for u in units[:lookahead]]
    for n, unit in enumerate(units):
        if n + lookahead < len(units):
            pending.append(probabilities(*units[n + lookahead]))
        output(*unit, pending.pop(0))


def _diff_attention(proj, lam_params, subln_g, *, batch, seq, row_off, n_heads, hd, lam_init,
                    cache_k=None, cache_v=None, prev=None):
    latent = cache_k is not None
    w = 2 * hd
    d = n_heads * w
    tq = _tile(seq, ATTN_TQ)
    nq = seq // tq
    heads = 1 if latent else _tile(n_heads, ATTN_CTX_HEADS_PER_STEP)
    assert row_off % seq == 0
    q_off, kv_off = row_off // tq, row_off // seq
    segs = n_heads // heads
    scale = hd ** -0.5
    past = cache_k.shape[0] // batch if latent else 0

    def kv_spec(rows, seg=0, off=0):
        return pl.BlockSpec((rows, heads * w), lambda b, h, i: (off + b, seg * segs + h))

    in_specs = [pl.BlockSpec(lam_params.shape, lambda b, h, i: (0, 0)),
                pl.BlockSpec((tq, heads * w), lambda b, h, i: (q_off + b * nq + i, h)),
                kv_spec(seq, 1, kv_off), kv_spec(seq, 2, kv_off)]
    args = [lam_params, proj, proj, proj]
    scratch = []
    if latent:
        cos, sin = _rope_tables(seq, hd)
        table = pl.BlockSpec((seq, hd), lambda b, h, i: (0, 0))
        in_specs += [kv_spec(past), kv_spec(past), table, table]
        args += [cache_k, cache_v, cos, sin]
        scratch = [pltpu.VMEM((past + seq, w), BF16), pltpu.VMEM((past + seq, w), BF16)]
    in_specs.append(pl.BlockSpec((1, w), lambda b, h, i: (0, 0)))
    args.append(subln_g.reshape(1, w))
    out_specs = [pl.BlockSpec((tq, heads * w), lambda b, h, i: (q_off + b * nq + i, h))]
    out_shape = [jax.ShapeDtypeStruct((proj.shape[0], d), BF16)]
    in_specs.append(pl.BlockSpec(memory_space=pl.ANY))
    args.append(prev)
    aliases = {len(args) - 1: 0}
    if not latent:
        out_specs += [kv_spec(seq), kv_spec(seq)]
        out_shape += [jax.ShapeDtypeStruct((batch * seq, d), F32)] * 2
    outs = pl.pallas_call(
        functools.partial(_attn_kernel, hd=hd, heads=heads, tq=tq, sub=_tile(tq, ATTN_SUB_TQ),
                          lookahead=1 if latent else 0, past=past, scale=scale, lam_init=lam_init,
                          latent=latent),
        grid=(batch, n_heads // heads, nq),
        in_specs=in_specs,
        out_specs=out_specs,
        out_shape=out_shape,
        scratch_shapes=scratch,
        input_output_aliases=aliases,
        compiler_params=_params(("parallel", "parallel", "arbitrary")),
        name="diff_attention_" + ("latent" if latent else "context"),
    )(*args)
    return outs[0] if latent else outs


def kernel(x_prompt, x_sample, state_hgrn_fwd, state_hgrn_bwd, cache_attn_k, cache_attn_v, c, c_ctx,
           w_mod, b_mod, norm_g, w_ff1, w_ff2, hg_w_in, hg_w_out, hg_g_norm, hg_lb_logits,
           da_w_in, da_w_out, da_subln_g, da_lambda, final_norm_g):
    batch, seq, d = x_prompt.shape
    dec_batch, dec_seq, _ = x_sample.shape
    depth = w_mod.shape[0]
    _, _, hg_heads, hg_dk, hg_dv = state_hgrn_fwd.shape
    _, _, past_len, da_halves, da_hd = cache_attn_k.shape
    da_heads = da_halves // 2
    ctx_rows = batch * seq
    lat_rows = dec_batch * dec_seq
    groups = _Groups(ctx_rows, dec_batch, dec_seq)

    x = (x_prompt.reshape(ctx_rows, d), x_sample.reshape(lat_rows, d))

    n_groups = 1 + dec_batch
    g8 = -(-n_groups // V7X_SUBLANES) * V7X_SUBLANES
    cond = jnp.concatenate([c_ctx[None, :], c, jnp.zeros((g8 - n_groups, d), F32)], axis=0)
    b_mod3 = b_mod.reshape(depth, 1, 6 * d)
    mods = {layer: _modulation(cond, w_mod, b_mod3, layer).reshape(g8, 1, 6 * d)
            for layer in range(depth) if layer == 0 or (layer - 1) % N_MIXERS != 0}
    SH1, SC1, GT1, SH2, SC2, GT2 = range(6)

    def mixer_weights(layer):
        j = layer // N_MIXERS
        if layer % N_MIXERS == 0:
            return [_Cast(hg_w_in, j, 0, 2 * d), _Cast(hg_w_in, j, 2 * d, 3 * d)], _Cast(hg_w_out, j)
        return [_Cast(da_w_in, j)], _Cast(da_w_out, j)

    first_parts, _ = mixer_weights(0)
    w_in = [_cast_weight(first_parts[0])] + first_parts[1:]
    new_sf, new_sb, new_k, new_v = [], [], [], []
    for layer in range(depth):
        mod = mods[layer]
        h = _norm_mod(x, norm_g[layer, 0], mod, SC1, SH1, groups)
        j = layer // N_MIXERS
        out_casts = (mixer_weights(layer)[1], _Cast(w_ff1, layer))
        projs = []
        for n in range(len(w_in)):
            last = n + 1 == len(w_in)
            casts = out_casts if last else tuple(w for w in w_in[n + 1:n + 2] if isinstance(w, _Cast))
            out = _matmul(h, w_in[n], F32, casts=casts)
            if last:
                out, w_out, w1 = out
            elif casts:
                out, w_in[n + 1] = out
            projs.append(out)
        if layer % N_MIXERS == 0:
            scan = dict(groups=groups, batch=batch, seq=seq, n_heads=hg_heads, dk=hg_dk, dv=hg_dv, d=d,
                        layer=layer)
            riders = [None, None]
            if layer + 1 < depth:
                riders = [(cond, w_mod, b_mod3, layer + 1, half * 3 * d, 3 * d) for half in range(2)]
            o_fwd, s_fwd, *m_fwd = _hg_scan(*projs, hg_lb_logits[0], state_hgrn_fwd[:, j], reverse=False,
                                            mod_rider=riders[0], **scan)
            mixed, s_bwd, *m_bwd = _hg_scan(*projs, hg_lb_logits[1], state_hgrn_bwd[:, j], reverse=True,
                                            o_fwd=o_fwd, g_norm=hg_g_norm[j], mod_rider=riders[1], **scan)
            if layer + 1 < depth:
                mods[layer + 1] = jnp.concatenate(m_fwd + m_bwd, axis=-1).reshape(g8, 1, 6 * d)
            new_sf.append(s_fwd)
            new_sb.append(s_bwd)
        else:
            proj, = projs
            lam_init = 0.8 - 0.6 * math.exp(-0.3 * layer)
            att = dict(n_heads=da_heads, hd=da_hd, lam_init=lam_init)
            mixed, k_ctx, v_ctx = _diff_attention(proj, da_lambda[j], da_subln_g[j], batch=batch, seq=seq,
                                                  row_off=0, prev=h, **att)
            new_k.append(k_ctx.reshape(batch, seq, da_halves, da_hd))
            new_v.append(v_ctx.reshape(batch, seq, da_heads, 2 * da_hd))
            mixed = _diff_attention(proj, da_lambda[j], da_subln_g[j], batch=dec_batch, seq=dec_seq,
                                    row_off=ctx_rows,
                                    cache_k=cache_attn_k[:, j].reshape(dec_batch * past_len, d),
                                    cache_v=cache_attn_v[:, j].reshape(dec_batch * past_len, d),
                                    prev=mixed, **att)
        x = _matmul(mixed, w_out, F32, "residual", res=x, mod=mod, gate_chunk=GT1, groups=groups)
        h = _norm_mod(x, norm_g[layer, 1], mod, SC2, SH2, groups)
        next_parts = mixer_weights(layer + 1)[0] if layer + 1 < depth else []
        u, w2, *w_in = _matmul(h, w1, BF16, "relu2", casts=(_Cast(w_ff2, layer), *next_parts))
        x = _matmul(u, w2, F32, "residual", res=x, mod=mod, gate_chunk=GT2, groups=groups)

    y_prompt = _final_norm(x, final_norm_g, 0, ctx_rows).reshape(batch, seq, d)
    y_sample = _final_norm(x, final_norm_g, ctx_rows, lat_rows).reshape(dec_batch, dec_seq, d)
    return (y_prompt, y_sample, jnp.stack(new_sf, axis=1), jnp.stack(new_sb, axis=1),
            jnp.stack(new_k, axis=1), jnp.stack(new_v, axis=1))
```
